```python
import math
import jax, jax.numpy as jnp
from jax import lax
import numpy as np

D_MODEL = 1024
BATCH = 2
SEQ = 16384
DEPTH = 2
DEC_BATCH = 32
DEC_SEQ = 16
PAST_LEN = 4096

CHUNK = 64
MIX_W = D_MODEL
ATT_HEADS = 8
ATT_KV_HEADS = 2
ATT_GROUP = ATT_HEADS // ATT_KV_HEADS
ATT_HEAD_DIM = 64
ATT_Q_W = ATT_HEADS * ATT_HEAD_DIM
ATT_KV_W = ATT_KV_HEADS * ATT_HEAD_DIM
ATT_SCALE = ATT_HEAD_DIM ** -0.5
WINDOW = 128
BAND_CHUNKS = -(-WINDOW // CHUNK)
BAND = (BAND_CHUNKS + 1) * CHUNK
REL_BUCKETS = 32
REL_MAX_DIST = 128
RET_HEADS = 4
RET_KEY_DIM = 128
RET_VAL_DIM = 128
RET_QK_W = RET_HEADS * RET_KEY_DIM
RET_V_W = RET_HEADS * RET_VAL_DIM
ROPE_BASE = 10000.0
IN_SIZES = (ATT_Q_W, ATT_KV_W, ATT_KV_W, RET_QK_W, RET_QK_W, RET_V_W, RET_V_W)
IN_W = sum(IN_SIZES)
D_FF = 4 * D_MODEL
N_MOD = 6
EPS = 1e-6

kernel_name = 'hymba_swa_sink_retention_stream_step'


def rms_norm(x, g):
    xf = x.astype(jnp.float32)
    y = xf * lax.rsqrt(jnp.mean(xf * xf, axis=-1, keepdims=True) + EPS)
    return (y * g.astype(jnp.float32)).astype(x.dtype)


def modulate(x, g, shift, scale):
    return rms_norm(x, g) * (1.0 + scale[:, None, :]) + shift[:, None, :]


def ada_params(c, w, b):
    m = jnp.einsum('bd,de->be', jax.nn.silu(c), w) + b
    return jnp.split(m, N_MOD, axis=-1)


def rotary(x, pos):
    half = x.shape[-1] // 2
    inv = 1.0 / (ROPE_BASE ** (jnp.arange(half, dtype=jnp.float32) / half))
    ang = pos.astype(jnp.float32)[:, None] * inv[None, :]
    cos = jnp.cos(ang)[:, None, :]
    sin = jnp.sin(ang)[:, None, :]
    xf = x.astype(jnp.float32)
    x1, x2 = xf[..., :half], xf[..., half:]
    return jnp.concatenate([x1 * cos - x2 * sin, x2 * cos + x1 * sin], axis=-1).astype(x.dtype)


def rel_bucket(rel):
    nb = REL_BUCKETS // 2
    n = -rel
    ret = jnp.where(n < 0, nb, 0)
    n = jnp.abs(n)
    max_exact = nb // 2
    nf = jnp.maximum(n, 1).astype(jnp.float32)
    large = max_exact + (jnp.log(nf / max_exact) / math.log(REL_MAX_DIST / max_exact)
                         * (nb - max_exact)).astype(jnp.int32)
    large = jnp.minimum(large, nb - 1)
    return ret + jnp.where(n < max_exact, n, large)


def relative_bias(rel, table):
    b = table.astype(jnp.float32)[rel_bucket(rel)]
    q, j = rel.shape
    return jnp.transpose(b, (2, 0, 1)).reshape(ATT_KV_HEADS, ATT_GROUP, q, j)


def sink_softmax(s, sinks):
    sink = sinks.astype(jnp.float32).reshape(ATT_KV_HEADS, ATT_GROUP, 1, 1)
    m = jnp.maximum(jnp.max(s, axis=-1, keepdims=True), sink)
    e = jnp.exp(s - m)
    return e / (jnp.sum(e, axis=-1, keepdims=True) + jnp.exp(sink - m))


def project(h, w_in, pos):
    bsz, seq = h.shape[:2]
    u = jnp.einsum('bsd,de->bse', h, w_in)
    cuts = [sum(IN_SIZES[:i + 1]) for i in range(len(IN_SIZES) - 1)]
    qa, ka, va, qr, kr, vr, gr = jnp.split(u, cuts, axis=-1)
    qa = qa.reshape(bsz, seq, ATT_HEADS, ATT_HEAD_DIM)
    ka = ka.reshape(bsz, seq, ATT_KV_HEADS, ATT_HEAD_DIM)
    va = va.reshape(bsz, seq, ATT_KV_HEADS, ATT_HEAD_DIM)
    qr = rotary(qr.reshape(bsz, seq, RET_HEADS, RET_KEY_DIM), pos)
    kr = rotary(kr.reshape(bsz, seq, RET_HEADS, RET_KEY_DIM), pos) * (RET_KEY_DIM ** -0.5)
    vr = vr.reshape(bsz, seq, RET_HEADS, RET_VAL_DIM)
    return qa, ka, va, qr, kr, vr, gr


def window_attention_prompt(q, k, v, sinks, rel_table):
    bsz, seq = q.shape[:2]
    nc = seq // CHUNK
    qc = q.reshape(bsz, nc, CHUNK, ATT_KV_HEADS, ATT_GROUP, ATT_HEAD_DIM)
    pad = ((0, 0), (BAND_CHUNKS * CHUNK, 0), (0, 0), (0, 0))
    kp = jnp.pad(k, pad).reshape(bsz, nc + BAND_CHUNKS, CHUNK, ATT_KV_HEADS, ATT_HEAD_DIM)
    vp = jnp.pad(v, pad).reshape(bsz, nc + BAND_CHUNKS, CHUNK, ATT_KV_HEADS, ATT_HEAD_DIM)
    kb = jnp.concatenate([kp[:, j:j + nc] for j in range(BAND_CHUNKS + 1)], axis=2)
    vb = jnp.concatenate([vp[:, j:j + nc] for j in range(BAND_CHUNKS + 1)], axis=2)
    s = jnp.einsum('bcikgd,bcjkd->bckgij', qc, kb).astype(jnp.float32) * ATT_SCALE
    qi = jnp.arange(CHUNK)
    kj = jnp.arange(BAND) - BAND_CHUNKS * CHUNK
    s = s + relative_bias(kj[None, :] - qi[:, None], rel_table)
    key_chunk = jnp.arange(nc)[:, None] + (jnp.arange(BAND) // CHUNK - BAND_CHUNKS)[None, :]
    s = jnp.where((key_chunk >= 0)[None, :, None, None, None, :], s, -jnp.inf)
    p = sink_softmax(s, sinks)
    o = jnp.einsum('bckgij,bcjkd->bcikgd', p.astype(vb.dtype), vb)
    return o.reshape(bsz, seq, ATT_Q_W)


def window_attention_sample(q, k_all, v_all, n_cache, sinks, rel_table):
    dbsz, n = q.shape[:2]
    qg = q.reshape(dbsz, n, ATT_KV_HEADS, ATT_GROUP, ATT_HEAD_DIM)
    s = jnp.einsum('bikgd,bjkd->bkgij', qg, k_all).astype(jnp.float32) * ATT_SCALE
    qpos = PAST_LEN + jnp.arange(n)
    kpos = PAST_LEN - n_cache + jnp.arange(n_cache + n)
    s = s + relative_bias(kpos[None, :] - qpos[:, None], rel_table)
    p = sink_softmax(s, sinks)
    o = jnp.einsum('bkgij,bjkd->bikgd', p.astype(v_all.dtype), v_all)
    return o.reshape(dbsz, n, ATT_Q_W)


def ret_log_decay():
    return jnp.log(1.0 - 2.0 ** (-5.0 - jnp.arange(RET_HEADS, dtype=jnp.float32)))


def causal_decay(n, log_g):
    i = jnp.arange(n)
    diff = (i[:, None] - i[None, :]).astype(jnp.float32)
    mask = diff >= 0
    return jnp.where(mask[None], jnp.exp(jnp.where(mask, diff, 0.0)[None] * log_g[:, None, None]), 0.0)


def retention_prompt(q, k, v):
    bsz, seq = q.shape[:2]
    nc = seq // CHUNK
    log_g = ret_log_decay()
    qc = q.astype(jnp.float32).reshape(bsz, nc, CHUNK, RET_HEADS, RET_KEY_DIM)
    kc = k.astype(jnp.float32).reshape(bsz, nc, CHUNK, RET_HEADS, RET_KEY_DIM)
    vc = v.astype(jnp.float32).reshape(bsz, nc, CHUNK, RET_HEADS, RET_VAL_DIM)
    pos = jnp.arange(CHUNK, dtype=jnp.float32)
    scores = jnp.einsum('bcihd,bcjhd->bchij', qc, kc) * causal_decay(CHUNK, log_g)
    intra = jnp.einsum('bchij,bcjhe->bcihe', scores, vc)
    kdec = jnp.exp((CHUNK - 1.0 - pos)[:, None] * log_g[None, :])
    delta = jnp.einsum('bcjhd,bcjhe->cbhde', kc * kdec[:, :, None], vc)
    chunk_decay = jnp.exp(CHUNK * log_g)[None, :, None, None]

    def step(s_prev, d):
        return chunk_decay * s_prev + d, s_prev

    s0 = jnp.zeros((bsz, RET_HEADS, RET_KEY_DIM, RET_VAL_DIM), jnp.float32)
    s_final, s_before = lax.scan(step, s0, delta)
    qdec = jnp.exp((pos + 1.0)[:, None] * log_g[None, :])
    inter = jnp.einsum('bcihd,cbhde->bcihe', qc * qdec[:, :, None], s_before)
    return (intra + inter).reshape(bsz, seq, RET_HEADS, RET_VAL_DIM), s_final


def retention_sample(q, k, v, s0):
    n = q.shape[1]
    log_g = ret_log_decay()
    qf, kf, vf = q.astype(jnp.float32), k.astype(jnp.float32), v.astype(jnp.float32)
    s0f = s0.astype(jnp.float32)
    pos = jnp.arange(n, dtype=jnp.float32)
    scores = jnp.einsum('bihd,bjhd->bhij', qf, kf) * causal_decay(n, log_g)
    intra = jnp.einsum('bhij,bjhe->bihe', scores, vf)
    qdec = jnp.exp((pos + 1.0)[:, None] * log_g[None, :])
    inter = jnp.einsum('bihd,bhde->bihe', qf * qdec[:, :, None], s0f)
    kdec = jnp.exp((n - 1.0 - pos)[:, None] * log_g[None, :])
    s_new = jnp.exp(n * log_g)[None, :, None, None] * s0f + jnp.einsum('bjhd,bjhe->bhde', kf * kdec[:, :, None], vf)
    return intra + inter, s_new


def merge_heads(att_o, ret_o, gr, w_out):
    bsz, seq = att_o.shape[:2]
    r = ret_o * lax.rsqrt(jnp.mean(ret_o * ret_o, axis=-1, keepdims=True) + EPS)
    r = r.reshape(bsz, seq, RET_V_W).astype(gr.dtype) * jax.nn.silu(gr)
    mixed = jnp.concatenate([att_o.astype(gr.dtype), r], axis=-1)
    return jnp.einsum('bse,ed->bsd', mixed, w_out)


def squared_relu_mlp(h, w_up, w_down):
    u = jnp.einsum('bsd,df->bsf', h, w_up)
    return jnp.einsum('bsf,fd->bsd', jnp.square(jax.nn.relu(u)), w_down)


def setup_inputs(seed: int = 0) -> dict:
    key = jax.random.key(seed)
    ks = jax.random.split(key, 20)
    f32 = jnp.float32
    keep = min(WINDOW, PAST_LEN)

    def nrm(k, shape, scale):
        return jax.random.normal(k, shape, f32) * scale

    return {
        'x_prompt': nrm(ks[0], (BATCH, SEQ, D_MODEL), 1.0),
        'x_sample': nrm(ks[1], (DEC_BATCH, DEC_SEQ, D_MODEL), 1.0),
        'c_prompt': nrm(ks[2], (BATCH, D_MODEL), 1.0),
        'c_sample': nrm(ks[3], (DEC_BATCH, D_MODEL), 1.0),
        'cache_win_k': nrm(ks[4], (DEPTH, DEC_BATCH, keep, ATT_KV_HEADS, ATT_HEAD_DIM), 1.0),
        'cache_win_v': nrm(ks[5], (DEPTH, DEC_BATCH, keep, ATT_KV_HEADS, ATT_HEAD_DIM), 1.0),
        'state_ret': nrm(ks[6], (DEPTH, DEC_BATCH, RET_HEADS, RET_KEY_DIM, RET_VAL_DIM), 0.5),
        'g_mix': 1.0 + nrm(ks[7], (DEPTH, D_MODEL), 0.1),
        'g_mlp': 1.0 + nrm(ks[8], (DEPTH, D_MODEL), 0.1),
        'w_ada': nrm(ks[9], (DEPTH, D_MODEL, N_MOD * D_MODEL), D_MODEL ** -0.5),
        'b_ada': nrm(ks[10], (DEPTH, N_MOD * D_MODEL), 0.1),
        'w_in': nrm(ks[11], (DEPTH, D_MODEL, IN_W), D_MODEL ** -0.5),
        'w_out': nrm(ks[12], (DEPTH, MIX_W, D_MODEL), MIX_W ** -0.5),
        'att_sinks': nrm(ks[13], (DEPTH, ATT_HEADS), 1.0),
        'rel_bias': nrm(ks[14], (REL_BUCKETS, ATT_HEADS), 0.3),
        'w_up': nrm(ks[15], (DEPTH, D_MODEL, D_FF), D_MODEL ** -0.5),
        'w_down': nrm(ks[16], (DEPTH, D_FF, D_MODEL), D_FF ** -0.5),
        'g_final': 1.0 + nrm(ks[17], (D_MODEL,), 0.1),
    }


def reference(x_prompt, x_sample, c_prompt, c_sample, cache_win_k, cache_win_v, state_ret,
              g_mix, g_mlp, w_ada, b_ada, w_in, w_out, att_sinks, rel_bias, w_up, w_down, g_final):
    seq = x_prompt.shape[1]
    dec_seq = x_sample.shape[1]
    n_cache = cache_win_k.shape[2]
    keep_p = min(WINDOW, seq)
    pos_p = jnp.arange(seq)
    pos_s = PAST_LEN + jnp.arange(dec_seq)
    xp, xs = x_prompt, x_sample
    pk, pv, pr, sk, sv, sr = [], [], [], [], [], []
    for l in range(DEPTH):
        sh1p, sc1p, ga1p, sh2p, sc2p, ga2p = ada_params(c_prompt, w_ada[l], b_ada[l])
        sh1s, sc1s, ga1s, sh2s, sc2s, ga2s = ada_params(c_sample, w_ada[l], b_ada[l])

        h = modulate(xp, g_mix[l], sh1p, sc1p)
        qa, ka, va, qr, kr, vr, gr = project(h, w_in[l], pos_p)
        ao = window_attention_prompt(qa, ka, va, att_sinks[l], rel_bias)
        ro, st = retention_prompt(qr, kr, vr)
        xp = xp + ga1p[:, None, :] * merge_heads(ao, ro, gr, w_out[l])
        xp = xp + ga2p[:, None, :] * squared_relu_mlp(modulate(xp, g_mlp[l], sh2p, sc2p), w_up[l], w_down[l])
        pk.append(ka[:, seq - keep_p:])
        pv.append(va[:, seq - keep_p:])
        pr.append(st)

        h = modulate(xs, g_mix[l], sh1s, sc1s)
        qa, ka, va, qr, kr, vr, gr = project(h, w_in[l], pos_s)
        k_all = jnp.concatenate([cache_win_k[l].astype(ka.dtype), ka], axis=1)
        v_all = jnp.concatenate([cache_win_v[l].astype(va.dtype), va], axis=1)
        ao = window_attention_sample(qa, k_all, v_all, n_cache, att_sinks[l], rel_bias)
        ro, st = retention_sample(qr, kr, vr, state_ret[l])
        xs = xs + ga1s[:, None, :] * merge_heads(ao, ro, gr, w_out[l])
        xs = xs + ga2s[:, None, :] * squared_relu_mlp(modulate(xs, g_mlp[l], sh2s, sc2s), w_up[l], w_down[l])
        total = n_cache + dec_seq
        sk.append(k_all[:, total - n_cache:])
        sv.append(v_all[:, total - n_cache:])
        sr.append(st)

    y_prompt = rms_norm(xp, g_final)
    y_sample = rms_norm(xs, g_final)
    return (y_prompt, y_sample, jnp.stack(pk), jnp.stack(pv), jnp.stack(pr), jnp.stack(sk), jnp.stack(sv), jnp.stack(sr))
```

```python
import functools
import math

import jax
import jax.numpy as jnp
from jax import lax
from jax.experimental import pallas as pl
from jax.experimental.pallas import tpu as pltpu

F32 = jnp.float32
BF16 = jnp.bfloat16

D_MODEL = 1024
CHUNK = 64
ATT_HEADS = 8
ATT_KV_HEADS = 2
ATT_HEAD_DIM = 64
ATT_Q_W = ATT_HEADS * ATT_HEAD_DIM
ATT_KV_W = ATT_KV_HEADS * ATT_HEAD_DIM
ATT_SCALE = ATT_HEAD_DIM ** -0.5
WINDOW = 128
BAND_CHUNKS = -(-WINDOW // CHUNK)
BAND = (BAND_CHUNKS + 1) * CHUNK
REL_BUCKETS = 32
REL_MAX_DIST = 128
RET_HEADS = 4
RET_KEY_DIM = 128
RET_VAL_DIM = 128
RET_W = RET_HEADS * RET_KEY_DIM
ROPE_BASE = 10000.0
IN_W = ATT_Q_W + 2 * ATT_KV_W + 4 * RET_W
D_FF = 4 * D_MODEL
N_MOD = 6
EPS = 1e-6
PAST_LEN = 4096

LANES = 128
HALF_LANES = LANES // 2
VMEM_LIMIT_BYTES = 56 * 1024 * 1024

OFF_QA = 0
OFF_KV = OFF_QA + ATT_Q_W
OFF_QR = OFF_KV + 2 * ATT_KV_W
OFF_KR = OFF_QR + RET_W
OFF_VR = OFF_KR + RET_W
OFF_GR = OFF_VR + RET_W

TOKEN_BLOCK = 512
FF_BLOCK = 1024
ADA_COL_BLOCK = 512
SAMPLE_GROUP = 4
SAMPLE_KEYS = 2 * LANES


def _params(*sem):
    return pltpu.CompilerParams(dimension_semantics=sem, vmem_limit_bytes=VMEM_LIMIT_BYTES)


def _const_spec(shape):
    zeros = (0,) * len(shape)
    return pl.BlockSpec(shape, lambda *_: zeros, pipeline_mode=pl.Buffered(1))


def _rms(x):
    return x * lax.rsqrt(jnp.mean(x * x, axis=-1, keepdims=True) + EPS)


def _dot(a, b):
    return jnp.dot(a, b, preferred_element_type=F32)


def _dot_nt(a, b):
    return lax.dot_general(a, b, (((1,), (1,)), ((), ())), preferred_element_type=F32)


def _dot_tn(a, b):
    return lax.dot_general(a, b, (((0,), (0,)), ((), ())), preferred_element_type=F32)


def _ada_kernel(c_ref, w_ref, b_ref, o_ref):
    a = jax.nn.silu(c_ref[...]).astype(BF16)
    o_ref[0] = _dot(a, w_ref[0].astype(BF16)) + b_ref[0]


def _ada_call(c_all, w_ada, b_ada):
    depth, d, n = w_ada.shape
    rows = c_all.shape[0]
    return pl.pallas_call(
        _ada_kernel,
        grid=(depth, n // ADA_COL_BLOCK),
        in_specs=[
            pl.BlockSpec((rows, d), lambda l, j: (0, 0)),
            pl.BlockSpec((1, d, ADA_COL_BLOCK), lambda l, j: (l, 0, j)),
            pl.BlockSpec((1, 1, ADA_COL_BLOCK), lambda l, j: (l, 0, j)),
        ],
        out_specs=pl.BlockSpec((1, rows, ADA_COL_BLOCK), lambda l, j: (l, 0, j)),
        out_shape=jax.ShapeDtypeStruct((depth, rows, n), F32),
        compiler_params=_params("arbitrary", "arbitrary"),
        name="ada",
    )(c_all, w_ada, b_ada.reshape(depth, 1, n))


def _bias_kernel(tab_ref, idx_ref, o_ref):
    h = pl.program_id(0)
    idx = idx_ref[...]
    acc = jnp.zeros(idx.shape, F32)
    for b in range(REL_BUCKETS):
        acc = jnp.where(idx == b, tab_ref[b, h], acc)
    o_ref[0] = acc


def _bias_call(rel_bias, bucket):
    q, j = bucket.shape
    return pl.pallas_call(
        _bias_kernel,
        grid=(ATT_HEADS,),
        in_specs=[
            pl.BlockSpec(memory_space=pltpu.SMEM),
            pl.BlockSpec((q, j), lambda h: (0, 0)),
        ],
        out_specs=pl.BlockSpec((1, q, j), lambda h: (h, 0, 0)),
        out_shape=jax.ShapeDtypeStruct((ATT_HEADS, q, j), F32),
        compiler_params=_params("arbitrary"),
        name="rel_bias",
    )(rel_bias, bucket)


def _rel_bucket(rel):
    nb = REL_BUCKETS // 2
    n = -rel
    ret = jnp.where(n < 0, nb, 0)
    n = jnp.abs(n)
    max_exact = nb // 2
    nf = jnp.maximum(n, 1).astype(F32)
    large = max_exact + (jnp.log(nf / max_exact) / math.log(REL_MAX_DIST / max_exact)
                         * (nb - max_exact)).astype(jnp.int32)
    large = jnp.minimum(large, nb - 1)
    return (ret + jnp.where(n < max_exact, n, large)).astype(jnp.int32)


def _pair_rows(per_head):
    out = []
    for k in range(ATT_KV_HEADS):
        for e in range(2):
            out.append(jnp.concatenate([per_head[4 * k + e], per_head[4 * k + 2 + e]], axis=0))
    return jnp.stack(out)


def _rotary(u, cos2, sin2):
    parts = []
    for h in range(RET_HEADS):
        x = u[:, h * RET_KEY_DIM:(h + 1) * RET_KEY_DIM]
        parts.append(x * cos2 + pltpu.roll(x, HALF_LANES, 1) * sin2)
    return jnp.concatenate(parts, axis=1)


def _inproj_kernel(x_ref, sh_ref, sc_ref, g_ref, w_ref, cos_ref, sin_ref,
                   qa_ref, ka_ref, va_ref, qr_ref, kr_ref, vr_ref, gr_ref, kvt_ref, *, tail):
    i = pl.program_id(1)
    x = x_ref[0]
    h = (_rms(x) * g_ref[...]) * (1.0 + sc_ref[0]) + sh_ref[0]
    hb = h.astype(BF16)
    cos2 = cos_ref[...]
    sin2 = sin_ref[...]

    qa_ref[0] = _dot(hb, w_ref[:, OFF_QA:OFF_KV]).astype(BF16)
    kv = _dot(hb, w_ref[:, OFF_KV:OFF_QR])
    ka_ref[0] = kv[:, :ATT_KV_W].astype(BF16)
    va_ref[0] = kv[:, ATT_KV_W:].astype(BF16)
    qr_ref[0] = _rotary(_dot(hb, w_ref[:, OFF_QR:OFF_KR]), cos2, sin2).astype(BF16)
    kr = _rotary(_dot(hb, w_ref[:, OFF_KR:OFF_VR]), cos2, sin2) * (RET_KEY_DIM ** -0.5)
    kr_ref[0] = kr.astype(BF16)
    vr_ref[0] = _dot(hb, w_ref[:, OFF_VR:OFF_GR]).astype(BF16)
    gr_ref[0] = _dot(hb, w_ref[:, OFF_GR:IN_W]).astype(BF16)

    @pl.when(i == pl.num_programs(1) - 1)
    def _():
        kvt_ref[0] = kv[kv.shape[0] - tail:, :]


def _mod_spec(arr, t):
    r = arr.shape[1]
    if r == 1:
        return pl.BlockSpec((1, 1, D_MODEL), lambda b, i: (b, 0, 0))
    return pl.BlockSpec((1, t, D_MODEL), lambda b, i: (b, i, 0))


def _inproj_call(x, shift, scale, gain, w_in, cos2, sin2, tail):
    bsz, seq, d = x.shape
    t = min(TOKEN_BLOCK, seq)
    nb = seq // t
    tok = lambda w: pl.BlockSpec((1, t, w), lambda b, i: (b, i, 0))
    out_bf = lambda w: jax.ShapeDtypeStruct((bsz, seq, w), BF16)
    return pl.pallas_call(
        functools.partial(_inproj_kernel, tail=tail),
        grid=(bsz, nb),
        in_specs=[
            tok(d), _mod_spec(shift, t), _mod_spec(scale, t),
            _const_spec((1, d)), _const_spec((d, IN_W)),
            pl.BlockSpec((t, LANES), lambda b, i: (i, 0)),
            pl.BlockSpec((t, LANES), lambda b, i: (i, 0)),
        ],
        out_specs=[
            tok(ATT_Q_W), tok(ATT_KV_W), tok(ATT_KV_W), tok(RET_W), tok(RET_W), tok(RET_W), tok(RET_W),
            pl.BlockSpec((1, tail, 2 * ATT_KV_W), lambda b, i: (b, 0, 0)),
        ],
        out_shape=[
            out_bf(ATT_Q_W), out_bf(ATT_KV_W), out_bf(ATT_KV_W), out_bf(RET_W), out_bf(RET_W),
            out_bf(RET_W), out_bf(RET_W),
            jax.ShapeDtypeStruct((bsz, tail, 2 * ATT_KV_W), F32),
        ],
        compiler_params=_params("arbitrary", "arbitrary"),
        name="inproj",
    )(x, shift, scale, gain.reshape(1, d), w_in, cos2, sin2)


def _lane_split(x):
    lo = lax.broadcasted_iota(jnp.int32, x.shape, 1) < HALF_LANES
    xf = x.astype(F32)
    sw = pltpu.roll(xf, HALF_LANES, 1)
    zero = jnp.zeros_like(xf)
    parts = (jnp.where(lo, xf, zero), jnp.where(lo, zero, sw),
             jnp.where(lo, sw, zero), jnp.where(lo, zero, xf))
    return tuple(p.astype(x.dtype) for p in parts)


def _sink_softmax(s, sink):
    m = jnp.maximum(jnp.max(s, axis=-1, keepdims=True), sink)
    e = jnp.exp(s - m)
    den = jnp.sum(e, axis=-1, keepdims=True) + jnp.exp(sink - m)
    return e / den


def _gated_group_norm(ro, gate):
    g = gate.astype(F32)
    return _rms(ro) * jax.nn.silu(g)


def _mix_prompt_kernel(cdec_ref, qa_ref, ka_ref, va_ref, qr_ref, kr_ref, vr_ref, gr_ref,
                       bias_ref, sink_ref, dmask_ref, qdec_ref, kdec_ref,
                       mixed_ref, state_ref, kx_ref, vx_ref, s_ref, *, t):
    i = pl.program_id(1)
    nch = t // CHUNK
    hist = BAND - CHUNK

    @pl.when(i == 0)
    def _():
        kx_ref[:, 0:hist, :] = jnp.zeros((4, hist, LANES), BF16)
        vx_ref[:, 0:hist, :] = jnp.zeros((4, hist, LANES), BF16)
        s_ref[...] = jnp.zeros(s_ref.shape, F32)

    @pl.when(i > 0)
    def _():
        kx_ref[:, 0:hist, :] = kx_ref[:, t:t + hist, :]
        vx_ref[:, 0:hist, :] = vx_ref[:, t:t + hist, :]

    for n, part in enumerate(_lane_split(ka_ref[0])):
        kx_ref[n, hist:hist + t, :] = part
    for n, part in enumerate(_lane_split(va_ref[0])):
        vx_ref[n, hist:hist + t, :] = part

    col = lax.broadcasted_iota(jnp.int32, (2 * CHUNK, BAND), 1)

    def chunk(c, carry):
        r0 = pl.multiple_of(c * CHUNK, CHUNK)
        rows = pl.ds(r0, CHUNK)
        band = pl.ds(r0, BAND)
        first_valid = jnp.maximum(BAND_CHUNKS - (i * nch + c), 0) * CHUNK
        valid = col >= first_valid

        for k in range(ATT_KV_HEADS):
            c0 = 2 * LANES * k
            q2 = jnp.concatenate([qa_ref[0, rows, c0:c0 + LANES],
                                  qa_ref[0, rows, c0 + LANES:c0 + 2 * LANES]], axis=0)
            o = jnp.zeros((2 * CHUNK, LANES), F32)
            for e in range(2):
                n = 2 * k + e
                s = _dot_nt(q2, kx_ref[n, band, :]) * ATT_SCALE + bias_ref[n]
                s = jnp.where(valid, s, -jnp.inf)
                p = _sink_softmax(s, sink_ref[n])
                o = o + _dot(p.astype(BF16), vx_ref[n, band, :])
            mixed_ref[0, rows, c0:c0 + LANES] = o[:CHUNK].astype(BF16)
            mixed_ref[0, rows, c0 + LANES:c0 + 2 * LANES] = o[CHUNK:].astype(BF16)

        for h in range(RET_HEADS):
            cols = slice(h * LANES, (h + 1) * LANES)
            q = qr_ref[0, rows, cols]
            k = kr_ref[0, rows, cols]
            v = vr_ref[0, rows, cols]
            state = s_ref[h]
            scores = _dot_nt(q, k) * dmask_ref[h]
            intra = _dot(scores.astype(BF16), v)
            qd = (q.astype(F32) * qdec_ref[h]).astype(BF16)
            inter = _dot(qd, state.astype(BF16))
            kd = (k.astype(F32) * kdec_ref[h]).astype(BF16)
            s_ref[h] = cdec_ref[h] * state + _dot_tn(kd, v)
            out = _gated_group_norm(intra + inter, gr_ref[0, rows, cols])
            mixed_ref[0, rows, ATT_Q_W + h * LANES:ATT_Q_W + (h + 1) * LANES] = out.astype(BF16)
        return carry

    lax.fori_loop(0, nch, chunk, 0)

    @pl.when(i == pl.num_programs(1) - 1)
    def _():
        state_ref[0] = s_ref[...]


def _mix_prompt_call(qa, ka, va, qr, kr, vr, gr, bias4, sink4, dmask, qdec, kdec, cdec):
    bsz, seq, _ = qa.shape
    t = min(TOKEN_BLOCK, seq)
    hist = BAND - CHUNK
    tok = lambda w: pl.BlockSpec((1, t, w), lambda b, i: (b, i, 0))
    return pl.pallas_call(
        functools.partial(_mix_prompt_kernel, t=t),
        grid=(bsz, seq // t),
        in_specs=[
            pl.BlockSpec(memory_space=pltpu.SMEM),
            tok(ATT_Q_W), tok(ATT_KV_W), tok(ATT_KV_W), tok(RET_W), tok(RET_W), tok(RET_W), tok(RET_W),
            _const_spec(bias4.shape), _const_spec(sink4.shape), _const_spec(dmask.shape),
            _const_spec(qdec.shape), _const_spec(kdec.shape),
        ],
        out_specs=[
            tok(D_MODEL),
            pl.BlockSpec((1, RET_HEADS, RET_KEY_DIM, RET_VAL_DIM), lambda b, i: (b, 0, 0, 0)),
        ],
        out_shape=[
            jax.ShapeDtypeStruct((bsz, seq, D_MODEL), BF16),
            jax.ShapeDtypeStruct((bsz, RET_HEADS, RET_KEY_DIM, RET_VAL_DIM), F32),
        ],
        scratch_shapes=[
            pltpu.VMEM((4, hist + t, LANES), BF16),
            pltpu.VMEM((4, hist + t, LANES), BF16),
            pltpu.VMEM((RET_HEADS, RET_KEY_DIM, RET_VAL_DIM), F32),
        ],
        compiler_params=_params("arbitrary", "arbitrary"),
        name="mix_prompt",
    )(cdec, qa, ka, va, qr, kr, vr, gr, bias4, sink4, dmask, qdec, kdec)


def _mix_sample_kernel(cdec_ref, qa_ref, kvn_ref, ck_ref, cv_ref, qr_ref, kr_ref, vr_ref, gr_ref, s0_ref,
                       bias_ref, sink_ref, dmask_ref, qdec_ref, kdec_ref,
                       mixed_ref, state_ref, *, n_new, n_cache):
    pad = jnp.zeros((SAMPLE_KEYS - n_cache - n_new, LANES), F32)
    for g in range(SAMPLE_GROUP):
        kvn = kvn_ref[g]
        k_all = jnp.concatenate([ck_ref[g], kvn[:, :ATT_KV_W], pad], axis=0)
        v_all = jnp.concatenate([cv_ref[g], kvn[:, ATT_KV_W:], pad], axis=0)
        k4 = [p.astype(BF16) for p in _lane_split(k_all)]
        v4 = [p.astype(BF16) for p in _lane_split(v_all)]
        for k in range(ATT_KV_HEADS):
            c0 = 2 * LANES * k
            q2 = jnp.concatenate([qa_ref[g, :, c0:c0 + LANES], qa_ref[g, :, c0 + LANES:c0 + 2 * LANES]], axis=0)
            o = jnp.zeros((2 * n_new, LANES), F32)
            for e in range(2):
                n = 2 * k + e
                s = _dot_nt(q2, k4[n]) * ATT_SCALE + bias_ref[n]
                p = _sink_softmax(s, sink_ref[n])
                o = o + _dot(p.astype(BF16), v4[n])
            mixed_ref[g, :, c0:c0 + LANES] = o[:n_new].astype(BF16)
            mixed_ref[g, :, c0 + LANES:c0 + 2 * LANES] = o[n_new:].astype(BF16)

        for h in range(RET_HEADS):
            cols = slice(h * LANES, (h + 1) * LANES)
            q = qr_ref[g, :, cols]
            k = kr_ref[g, :, cols]
            v = vr_ref[g, :, cols]
            state = s0_ref[g, h]
            scores = _dot_nt(q, k) * dmask_ref[h]
            intra = _dot(scores.astype(BF16), v)
            qd = (q.astype(F32) * qdec_ref[h]).astype(BF16)
            inter = _dot(qd, state.astype(BF16))
            kd = (k.astype(F32) * kdec_ref[h]).astype(BF16)
            state_ref[g, h] = cdec_ref[h] * state + _dot_tn(kd, v)
            out = _gated_group_norm(intra + inter, gr_ref[g, :, cols])
            mixed_ref[g, :, ATT_Q_W + h * LANES:ATT_Q_W + (h + 1) * LANES] = out.astype(BF16)


def _mix_sample_call(qa, kvn, ck, cv, qr, kr, vr, gr, s0, bias4, sink4, dmask, qdec, kdec, cdec):
    dbsz, n_new, _ = qa.shape
    n_cache = ck.shape[1]
    grp = lambda *tail: pl.BlockSpec((SAMPLE_GROUP,) + tail, lambda b: (b,) + (0,) * len(tail))
    return pl.pallas_call(
        functools.partial(_mix_sample_kernel, n_new=n_new, n_cache=n_cache),
        grid=(dbsz // SAMPLE_GROUP,),
        in_specs=[
            pl.BlockSpec(memory_space=pltpu.SMEM),
            grp(n_new, ATT_Q_W), grp(n_new, 2 * ATT_KV_W), grp(n_cache, ATT_KV_W), grp(n_cache, ATT_KV_W),
            grp(n_new, RET_W), grp(n_new, RET_W), grp(n_new, RET_W), grp(n_new, RET_W),
            grp(RET_HEADS, RET_KEY_DIM, RET_VAL_DIM),
            _const_spec(bias4.shape), _const_spec(sink4.shape), _const_spec(dmask.shape),
            _const_spec(qdec.shape), _const_spec(kdec.shape),
        ],
        out_specs=[grp(n_new, D_MODEL), grp(RET_HEADS, RET_KEY_DIM, RET_VAL_DIM)],
        out_shape=[
            jax.ShapeDtypeStruct((dbsz, n_new, D_MODEL), BF16),
            jax.ShapeDtypeStruct((dbsz, RET_HEADS, RET_KEY_DIM, RET_VAL_DIM), F32),
        ],
        compiler_params=_params("arbitrary"),
        name="mix_sample",
    )(cdec, qa, kvn, ck, cv, qr, kr, vr, gr, s0, bias4, sink4, dmask, qdec, kdec)


def _dense_kernel(x_ref, mixed_ref, ga1_ref, sh2_ref, sc2_ref, ga2_ref, g_ref, gf_ref,
                  wo_ref, wu_ref, wd_ref, o_ref, *, final):
    x1 = x_ref[0] + ga1_ref[0] * _dot(mixed_ref[0], wo_ref[...])
    h = ((_rms(x1) * g_ref[...]) * (1.0 + sc2_ref[0]) + sh2_ref[0]).astype(BF16)
    acc = jnp.zeros(x1.shape, F32)
    for j in range(D_FF // FF_BLOCK):
        u = _dot(h, wu_ref[:, j * FF_BLOCK:(j + 1) * FF_BLOCK])
        u = jnp.square(jnp.maximum(u, 0.0)).astype(BF16)
        acc = acc + _dot(u, wd_ref[j * FF_BLOCK:(j + 1) * FF_BLOCK, :])
    x2 = x1 + ga2_ref[0] * acc
    o_ref[0] = _rms(x2) * gf_ref[...] if final else x2


def _dense_call(x, mixed, ga1, sh2, sc2, ga2, gain, g_final, w_out, w_up, w_down, final):
    bsz, seq, d = x.shape
    t = min(TOKEN_BLOCK, seq)
    tok = lambda w: pl.BlockSpec((1, t, w), lambda b, i: (b, i, 0))
    return pl.pallas_call(
        functools.partial(_dense_kernel, final=final),
        grid=(bsz, seq // t),
        in_specs=[
            tok(d), tok(d), _mod_spec(ga1, t), _mod_spec(sh2, t), _mod_spec(sc2, t), _mod_spec(ga2, t),
            _const_spec((1, d)), _const_spec((1, d)),
            _const_spec(w_out.shape), _const_spec(w_up.shape), _const_spec(w_down.shape),
        ],
        out_specs=tok(d),
        out_shape=jax.ShapeDtypeStruct((bsz, seq, d), F32),
        compiler_params=_params("arbitrary", "arbitrary"),
        name="dense",
    )(x, mixed, ga1, sh2, sc2, ga2, gain.reshape(1, d), g_final.reshape(1, d), w_out, w_up, w_down)


def _rotary_tables(pos):
    half = RET_KEY_DIM // 2
    inv = 1.0 / (ROPE_BASE ** (jnp.arange(half, dtype=F32) / half))
    ang = pos.astype(F32)[:, None] * inv[None, :]
    cos, sin = jnp.cos(ang), jnp.sin(ang)
    return jnp.concatenate([cos, cos], axis=1), jnp.concatenate([-sin, sin], axis=1)


def _retention_tables(n):
    log_g = jnp.log(1.0 - 2.0 ** (-5.0 - jnp.arange(RET_HEADS, dtype=F32)))
    idx = jnp.arange(n)
    diff = (idx[:, None] - idx[None, :]).astype(F32)
    mask = diff >= 0
    dmask = jnp.where(mask[None], jnp.exp(jnp.where(mask, diff, 0.0)[None] * log_g[:, None, None]), 0.0)
    pos = jnp.arange(n, dtype=F32)
    qdec = jnp.exp((pos + 1.0)[None, :] * log_g[:, None])
    kdec = jnp.exp((n - 1.0 - pos)[None, :] * log_g[:, None])
    cdec = jnp.exp(n * log_g)
    wide = lambda a: jnp.broadcast_to(a[:, :, None], (RET_HEADS, n, LANES))
    return dmask, wide(qdec), wide(kdec), cdec


def kernel(x_prompt, x_sample, c_prompt, c_sample, cache_win_k, cache_win_v, state_ret, g_mix, g_mlp, w_ada,
           b_ada, w_in, w_out, att_sinks, rel_bias, w_up, w_down, g_final):
    depth = w_in.shape[0]
    bsz, seq, d = x_prompt.shape
    dbsz, dec_seq, _ = x_sample.shape
    n_cache = cache_win_k.shape[2]
    n_rows = dbsz * dec_seq
    keep_p = min(WINDOW, seq)

    w_in_b, w_out_b, w_up_b, w_down_b = (w.astype(BF16) for w in (w_in, w_out, w_up, w_down))

    c_all = jnp.concatenate([c_prompt, c_sample], axis=0)
    c_all = jnp.pad(c_all, ((0, -c_all.shape[0] % 8), (0, 0)))
    mods = _ada_call(c_all, w_ada, b_ada).reshape(depth, c_all.shape[0], N_MOD, d)
    mods_p = [[mods[l, :bsz, m][:, None, :] for m in range(N_MOD)] for l in range(depth)]
    mods_s = [[jnp.repeat(mods[l, bsz:bsz + dbsz, m], dec_seq, axis=0)[None] for m in range(N_MOD)]
              for l in range(depth)]

    cos_p, sin_p = _rotary_tables(jnp.arange(seq))
    cos_s, sin_s = _rotary_tables(jnp.tile(PAST_LEN + jnp.arange(dec_seq), dbsz))
    ret_p = _retention_tables(CHUNK)
    ret_s = _retention_tables(dec_seq)

    rel_p = (jnp.arange(BAND) - BAND_CHUNKS * CHUNK)[None, :] - jnp.arange(CHUNK)[:, None]
    bias_p = _pair_rows(_bias_call(rel_bias, _rel_bucket(rel_p)))
    kpos = PAST_LEN - n_cache + jnp.arange(n_cache + dec_seq)
    qpos = PAST_LEN + jnp.arange(dec_seq)
    bias_s = _bias_call(rel_bias, _rel_bucket(kpos[None, :] - qpos[:, None]))
    bias_s = jnp.pad(bias_s, ((0, 0), (0, 0), (0, SAMPLE_KEYS - n_cache - dec_seq)), constant_values=-jnp.inf)
    bias_s = _pair_rows(bias_s)

    xp = x_prompt
    xs = x_sample.reshape(1, n_rows, d)
    pk, pv, pr, sk, sv, sr = [], [], [], [], [], []
    for l in range(depth):
        sinks = att_sinks[l].astype(F32)[:, None, None]
        sink_p = _pair_rows(jnp.broadcast_to(sinks, (ATT_HEADS, CHUNK, 1)))
        sink_s = _pair_rows(jnp.broadcast_to(sinks, (ATT_HEADS, dec_seq, 1)))
        final = l == depth - 1

        sh1, sc1, ga1, sh2, sc2, ga2 = mods_p[l]
        qa, ka, va, qr, kr, vr, gr, kvt = _inproj_call(xp, sh1, sc1, g_mix[l], w_in_b[l], cos_p, sin_p, keep_p)
        mixed, st = _mix_prompt_call(qa, ka, va, qr, kr, vr, gr, bias_p, sink_p, *ret_p)
        xp = _dense_call(xp, mixed, ga1, sh2, sc2, ga2, g_mlp[l], g_final, w_out_b[l], w_up_b[l], w_down_b[l],
                         final)
        pk.append(kvt[:, :, :ATT_KV_W].reshape(bsz, keep_p, ATT_KV_HEADS, ATT_HEAD_DIM))
        pv.append(kvt[:, :, ATT_KV_W:].reshape(bsz, keep_p, ATT_KV_HEADS, ATT_HEAD_DIM))
        pr.append(st)

        sh1, sc1, ga1, sh2, sc2, ga2 = mods_s[l]
        qa, _, _, qr, kr, vr, gr, kvn = _inproj_call(xs, sh1, sc1, g_mix[l], w_in_b[l], cos_s, sin_s, n_rows)
        per_seq = lambda a: a.reshape(dbsz, dec_seq, a.shape[-1])
        ck = cache_win_k[l].reshape(dbsz, n_cache, ATT_KV_W)
        cv = cache_win_v[l].reshape(dbsz, n_cache, ATT_KV_W)
        kvn = per_seq(kvn)
        mixed, st = _mix_sample_call(per_seq(qa), kvn, ck, cv, per_seq(qr), per_seq(kr), per_seq(vr), per_seq(gr),
                                     state_ret[l], bias_s, sink_s, *ret_s)
        xs = _dense_call(xs, mixed.reshape(1, n_rows, d), ga1, sh2, sc2, ga2, g_mlp[l], g_final, w_out_b[l],
                         w_up_b[l], w_down_b[l], final)
        k_roll = jnp.concatenate([ck, kvn[:, :, :ATT_KV_W]], axis=1)[:, dec_seq:]
        v_roll = jnp.concatenate([cv, kvn[:, :, ATT_KV_W:]], axis=1)[:, dec_seq:]
        sk.append(k_roll.reshape(dbsz, n_cache, ATT_KV_HEADS, ATT_HEAD_DIM))
        sv.append(v_roll.reshape(dbsz, n_cache, ATT_KV_HEADS, ATT_HEAD_DIM))
        sr.append(st)

    return (xp, xs.reshape(dbsz, dec_seq, d), jnp.stack(pk), jnp.stack(pv), jnp.stack(pr),
            jnp.stack(sk), jnp.stack(sv), jnp.stack(sr))
```

```python
import functools
import math

import jax
import jax.numpy as jnp
from jax import lax
from jax.experimental import pallas as pl
from jax.experimental.pallas import tpu as pltpu

F32 = jnp.float32
BF16 = jnp.bfloat16

D_MODEL = 1024
CHUNK = 64
ATT_HEADS = 8
ATT_KV_HEADS = 2
ATT_HEAD_DIM = 64
ATT_Q_W = ATT_HEADS * ATT_HEAD_DIM
ATT_KV_W = ATT_KV_HEADS * ATT_HEAD_DIM
ATT_SCALE = ATT_HEAD_DIM ** -0.5
WINDOW = 128
BAND_CHUNKS = -(-WINDOW // CHUNK)
BAND = (BAND_CHUNKS + 1) * CHUNK
REL_BUCKETS = 32
REL_MAX_DIST = 128
RET_HEADS = 4
RET_KEY_DIM = 128
RET_VAL_DIM = 128
RET_W = RET_HEADS * RET_KEY_DIM
ROPE_BASE = 10000.0
IN_W = ATT_Q_W + 2 * ATT_KV_W + 4 * RET_W
D_FF = 4 * D_MODEL
N_MOD = 6
EPS = 1e-6
PAST_LEN = 4096

LANES = 128
HALF_LANES = LANES // 2
VMEM_LIMIT_BYTES = 56 * 1024 * 1024

OFF_QA = 0
OFF_KV = OFF_QA + ATT_Q_W
OFF_QR = OFF_KV + 2 * ATT_KV_W
OFF_KR = OFF_QR + RET_W
OFF_VR = OFF_KR + RET_W
OFF_GR = OFF_VR + RET_W

TOKEN_BLOCK = 512
MIX_TOKEN_BLOCK = 1024
GROUP = 128
FF_BLOCK = 1024
ADA_COL_BLOCK = 512
SAMPLE_GROUP = 4
SAMPLE_KEYS = 2 * LANES


def _params(*sem):
    return pltpu.CompilerParams(dimension_semantics=sem, vmem_limit_bytes=VMEM_LIMIT_BYTES)


def _const_spec(shape):
    zeros = (0,) * len(shape)
    return pl.BlockSpec(shape, lambda *_: zeros, pipeline_mode=pl.Buffered(1))


def _rms(x):
    return x * lax.rsqrt(jnp.mean(x * x, axis=-1, keepdims=True) + EPS)


def _dot(a, b):
    return jnp.dot(a, b, preferred_element_type=F32)


def _dot_nt(a, b):
    return lax.dot_general(a, b, (((1,), (1,)), ((), ())), preferred_element_type=F32)


def _dot_tn(a, b):
    return lax.dot_general(a, b, (((0,), (0,)), ((), ())), preferred_element_type=F32)


def _ada_kernel(c_ref, w_ref, b_ref, o_ref):
    a = jax.nn.silu(c_ref[...]).astype(BF16)
    o_ref[0] = _dot(a, w_ref[0].astype(BF16)) + b_ref[0]


def _ada_call(c_all, w_ada, b_ada):
    depth, d, n = w_ada.shape
    rows = c_all.shape[0]
    return pl.pallas_call(
        _ada_kernel,
        grid=(depth, n // ADA_COL_BLOCK),
        in_specs=[
            pl.BlockSpec((rows, d), lambda l, j: (0, 0)),
            pl.BlockSpec((1, d, ADA_COL_BLOCK), lambda l, j: (l, 0, j)),
            pl.BlockSpec((1, 1, ADA_COL_BLOCK), lambda l, j: (l, 0, j)),
        ],
        out_specs=pl.BlockSpec((1, rows, ADA_COL_BLOCK), lambda l, j: (l, 0, j)),
        out_shape=jax.ShapeDtypeStruct((depth, rows, n), F32),
        compiler_params=_params("arbitrary", "arbitrary"),
        name="ada",
    )(c_all, w_ada, b_ada.reshape(depth, 1, n))


def _bias_kernel(tab_ref, idx_ref, o_ref):
    h = pl.program_id(0)
    idx = idx_ref[...]
    acc = jnp.zeros(idx.shape, F32)
    for b in range(REL_BUCKETS):
        acc = jnp.where(idx == b, tab_ref[b, h], acc)
    o_ref[0] = acc


def _bias_call(rel_bias, bucket):
    q, j = bucket.shape
    return pl.pallas_call(
        _bias_kernel,
        grid=(ATT_HEADS,),
        in_specs=[
            pl.BlockSpec(memory_space=pltpu.SMEM),
            pl.BlockSpec((q, j), lambda h: (0, 0)),
        ],
        out_specs=pl.BlockSpec((1, q, j), lambda h: (h, 0, 0)),
        out_shape=jax.ShapeDtypeStruct((ATT_HEADS, q, j), F32),
        compiler_params=_params("arbitrary"),
        name="rel_bias",
    )(rel_bias, bucket)


def _rel_bucket(rel):
    nb = REL_BUCKETS // 2
    n = -rel
    ret = jnp.where(n < 0, nb, 0)
    n = jnp.abs(n)
    max_exact = nb // 2
    nf = jnp.maximum(n, 1).astype(F32)
    large = max_exact + (jnp.log(nf / max_exact) / math.log(REL_MAX_DIST / max_exact)
                         * (nb - max_exact)).astype(jnp.int32)
    large = jnp.minimum(large, nb - 1)
    return (ret + jnp.where(n < max_exact, n, large)).astype(jnp.int32)


def _pair_rows(per_head):
    out = []
    for k in range(ATT_KV_HEADS):
        for e in range(2):
            out.append(jnp.concatenate([per_head[4 * k + e], per_head[4 * k + 2 + e]], axis=0))
    return jnp.stack(out)


def _pair_lanes(per_head):
    out = []
    for k in range(ATT_KV_HEADS):
        for e in range(2):
            out.append(jnp.concatenate([per_head[4 * k + e], per_head[4 * k + 2 + e]], axis=-1))
    return jnp.stack(out)


def _banded_bias(per_head):
    base = _pair_lanes(jnp.swapaxes(per_head, 1, 2))
    key = jnp.arange(BAND)[None, :, None]
    return jnp.stack([jnp.where(key < (BAND_CHUNKS - v) * CHUNK, -jnp.inf, base)
                      for v in range(BAND_CHUNKS + 1)])


def _rotary(u, cos2, sin2):
    parts = []
    for h in range(RET_HEADS):
        x = u[:, h * RET_KEY_DIM:(h + 1) * RET_KEY_DIM]
        parts.append(x * cos2 + pltpu.roll(x, HALF_LANES, 1) * sin2)
    return jnp.concatenate(parts, axis=1)


def _inproj_kernel(x_ref, sh_ref, sc_ref, g_ref, w_ref, cos_ref, sin_ref,
                   qa_ref, ka_ref, va_ref, qr_ref, kr_ref, vr_ref, gr_ref, kvt_ref, *, tail):
    i = pl.program_id(1)
    x = x_ref[0]
    h = (_rms(x) * g_ref[...]) * (1.0 + sc_ref[0]) + sh_ref[0]
    hb = h.astype(BF16)
    cos2 = cos_ref[...]
    sin2 = sin_ref[...]

    qa_ref[0] = _dot(hb, w_ref[:, OFF_QA:OFF_KV]).astype(BF16)
    kv = _dot(hb, w_ref[:, OFF_KV:OFF_QR])
    ka_ref[0] = kv[:, :ATT_KV_W].astype(BF16)
    va_ref[0] = kv[:, ATT_KV_W:].astype(BF16)
    qr_ref[0] = _rotary(_dot(hb, w_ref[:, OFF_QR:OFF_KR]), cos2, sin2).astype(BF16)
    kr = _rotary(_dot(hb, w_ref[:, OFF_KR:OFF_VR]), cos2, sin2) * (RET_KEY_DIM ** -0.5)
    kr_ref[0] = kr.astype(BF16)
    vr_ref[0] = _dot(hb, w_ref[:, OFF_VR:OFF_GR]).astype(BF16)
    gr_ref[0] = _dot(hb, w_ref[:, OFF_GR:IN_W]).astype(BF16)

    @pl.when(i == pl.num_programs(1) - 1)
    def _():
        kvt_ref[0] = kv[kv.shape[0] - tail:, :]


def _mod_spec(arr, t):
    r = arr.shape[1]
    if r == 1:
        return pl.BlockSpec((1, 1, D_MODEL), lambda b, i: (b, 0, 0))
    return pl.BlockSpec((1, t, D_MODEL), lambda b, i: (b, i, 0))


def _inproj_call(x, shift, scale, gain, w_in, cos2, sin2, tail):
    bsz, seq, d = x.shape
    t = min(TOKEN_BLOCK, seq)
    nb = seq // t
    tok = lambda w: pl.BlockSpec((1, t, w), lambda b, i: (b, i, 0))
    out_bf = lambda w: jax.ShapeDtypeStruct((bsz, seq, w), BF16)
    return pl.pallas_call(
        functools.partial(_inproj_kernel, tail=tail),
        grid=(bsz, nb),
        in_specs=[
            tok(d), _mod_spec(shift, t), _mod_spec(scale, t),
            _const_spec((1, d)), _const_spec((d, IN_W)),
            pl.BlockSpec((t, LANES), lambda b, i: (i, 0)),
            pl.BlockSpec((t, LANES), lambda b, i: (i, 0)),
        ],
        out_specs=[
            tok(ATT_Q_W), tok(ATT_KV_W), tok(ATT_KV_W), tok(RET_W), tok(RET_W), tok(RET_W), tok(RET_W),
            pl.BlockSpec((1, tail, 2 * ATT_KV_W), lambda b, i: (b, 0, 0)),
        ],
        out_shape=[
            out_bf(ATT_Q_W), out_bf(ATT_KV_W), out_bf(ATT_KV_W), out_bf(RET_W), out_bf(RET_W),
            out_bf(RET_W), out_bf(RET_W),
            jax.ShapeDtypeStruct((bsz, tail, 2 * ATT_KV_W), F32),
        ],
        compiler_params=_params("arbitrary", "arbitrary"),
        name="inproj",
    )(x, shift, scale, gain.reshape(1, d), w_in, cos2, sin2)


def _lane_split(x, scale=1.0):
    lo = lax.broadcasted_iota(jnp.int32, x.shape, 1) < HALF_LANES
    xf = x.astype(F32) * scale
    sw = pltpu.roll(xf, HALF_LANES, 1)
    zero = jnp.zeros_like(xf)
    parts = (jnp.where(lo, xf, zero), jnp.where(lo, zero, sw),
             jnp.where(lo, sw, zero), jnp.where(lo, zero, xf))
    return tuple(p.astype(x.dtype) for p in parts)


def _sink_softmax(s, sink):
    m = jnp.maximum(jnp.max(s, axis=-1, keepdims=True), sink)
    e = jnp.exp(s - m)
    den = jnp.sum(e, axis=-1, keepdims=True) + jnp.exp(sink - m)
    return e / den


def _gated_group_norm(ro, gate):
    g = gate.astype(F32)
    return _rms(ro) * jax.nn.silu(g)


def _mix_prompt_kernel(cdec_ref, qa_ref, ka_ref, va_ref, qr_ref, kr_ref, vr_ref, gr_ref,
                       bias_ref, sink_ref, dmask_ref, qdec_ref, kdec_ref,
                       mixed_ref, state_ref, kx_ref, vt_ref, s_ref, *, t):
    i = pl.program_id(1)
    n_groups = t // GROUP
    hist = BAND - CHUNK

    @pl.when(i == 0)
    def _():
        kx_ref[:, 0:hist, :] = jnp.zeros((4, hist, LANES), BF16)
        vt_ref[0] = jnp.zeros((LANES, GROUP), BF16)
        s_ref[...] = jnp.zeros(s_ref.shape, F32)

    @pl.when(i > 0)
    def _():
        kx_ref[:, 0:hist, :] = kx_ref[:, t:t + hist, :]
        vt_ref[0] = vt_ref[n_groups]

    for n, part in enumerate(_lane_split(ka_ref[0], ATT_SCALE)):
        kx_ref[n, hist:hist + t, :] = part
    for m in range(n_groups):
        vt_ref[1 + m] = va_ref[0, m * GROUP:(m + 1) * GROUP, :].astype(F32).T.astype(BF16)

    no_keys = jnp.zeros((CHUNK, LANES), BF16)

    def group(g, carry):
        g0 = pl.multiple_of(g * GROUP, GROUP)
        grows = pl.ds(g0, GROUP)
        heads = [slice(h * LANES, (h + 1) * LANES) for h in range(RET_HEADS)]
        att = [(j, k, e) for j in range(GROUP // CHUNK) for k in range(ATT_KV_HEADS) for e in range(2)]

        def chunk_rows(j):
            return pl.ds(pl.multiple_of(g0 + j * CHUNK, CHUNK), CHUNK)

        raw = {}
        for j, k, e in att:
            c0 = 2 * LANES * k
            q2 = jnp.concatenate([qa_ref[0, chunk_rows(j), c0:c0 + LANES],
                                  qa_ref[0, chunk_rows(j), c0 + LANES:c0 + 2 * LANES]], axis=0)
            band = pl.ds(pl.multiple_of(g0 + j * CHUNK, CHUNK), BAND)
            raw[j, k, e] = _dot_nt(kx_ref[2 * k + e, band, :], q2)
        states = [s_ref[h] for h in range(RET_HEADS)]
        ret_q = [qr_ref[0, grows, c] for c in heads]
        ret_k = [kr_ref[0, grows, c] for c in heads]
        ret_v = [vr_ref[0, grows, c] for c in heads]
        ret_scores = [_dot_nt(ret_q[h], ret_k[h]) for h in range(RET_HEADS)]
        ret_inter = [_dot(ret_q[h], states[h].astype(BF16)) for h in range(RET_HEADS)]
        for h in range(RET_HEADS):
            kd = (ret_k[h].astype(F32) * kdec_ref[h]).astype(BF16)
            s_ref[h] = cdec_ref[h] * states[h] + _dot_tn(kd, ret_v[h])

        probs, dens = {}, {}
        for j, k, e in att:
            n = 2 * k + e
            variant = jnp.minimum(i * (t // CHUNK) + 2 * g + j, BAND_CHUNKS)
            s = raw[j, k, e] + bias_ref[variant, n]
            sink = sink_ref[n]
            m = jnp.maximum(jnp.max(s, axis=0, keepdims=True), sink)
            p = jnp.exp(s - m)
            dens[j, k, e] = jnp.sum(p, axis=0, keepdims=True) + jnp.exp(sink - m)
            probs[j, k, e] = p.astype(BF16)
        ret_p = [(ret_scores[h] * dmask_ref[h]).astype(BF16) for h in range(RET_HEADS)]

        out_t = {}
        for j, k, e in att:
            p = probs[j, k, e]
            if j == 0:
                pa, pb = p[:GROUP], jnp.concatenate([p[GROUP:], no_keys], axis=0)
            else:
                pa, pb = jnp.concatenate([no_keys, p[:CHUNK]], axis=0), p[CHUNK:]
            v_rows = slice(k * ATT_HEAD_DIM, (k + 1) * ATT_HEAD_DIM)
            out_t[j, k, e] = _dot(vt_ref[g, v_rows, :], pa) + _dot(vt_ref[g + 1, v_rows, :], pb)
        ret_intra = [_dot(ret_p[h], ret_v[h]) for h in range(RET_HEADS)]

        for j in range(GROUP // CHUNK):
            for k in range(ATT_KV_HEADS):
                c0 = 2 * LANES * k
                o = jnp.concatenate([out_t[j, k, e] * (1.0 / dens[j, k, e]) for e in range(2)], axis=0).T
                mixed_ref[0, chunk_rows(j), c0:c0 + LANES] = o[:CHUNK].astype(BF16)
                mixed_ref[0, chunk_rows(j), c0 + LANES:c0 + 2 * LANES] = o[CHUNK:].astype(BF16)
        for h in range(RET_HEADS):
            ro = ret_intra[h] + ret_inter[h] * qdec_ref[h]
            out = _gated_group_norm(ro, gr_ref[0, grows, heads[h]])
            mixed_ref[0, grows, ATT_Q_W + h * LANES:ATT_Q_W + (h + 1) * LANES] = out.astype(BF16)
        return carry

    lax.fori_loop(0, n_groups, group, 0)

    @pl.when(i == pl.num_programs(1) - 1)
    def _():
        state_ref[0] = s_ref[...]


def _mix_prompt_call(qa, ka, va, qr, kr, vr, gr, bias4, sink4, dmask, qdec, kdec, cdec):
    bsz, seq, _ = qa.shape
    t = min(MIX_TOKEN_BLOCK, seq)
    hist = BAND - CHUNK
    assert hist == GROUP and t % GROUP == 0 and seq % t == 0
    tok = lambda w: pl.BlockSpec((1, t, w), lambda b, i: (b, i, 0))
    return pl.pallas_call(
        functools.partial(_mix_prompt_kernel, t=t),
        grid=(bsz, seq // t),
        in_specs=[
            pl.BlockSpec(memory_space=pltpu.SMEM),
            tok(ATT_Q_W), tok(ATT_KV_W), tok(ATT_KV_W), tok(RET_W), tok(RET_W), tok(RET_W), tok(RET_W),
            _const_spec(bias4.shape), _const_spec(sink4.shape), _const_spec(dmask.shape),
            _const_spec(qdec.shape), _const_spec(kdec.shape),
        ],
        out_specs=[
            tok(D_MODEL),
            pl.BlockSpec((1, RET_HEADS, RET_KEY_DIM, RET_VAL_DIM), lambda b, i: (b, 0, 0, 0)),
        ],
        out_shape=[
            jax.ShapeDtypeStruct((bsz, seq, D_MODEL), BF16),
            jax.ShapeDtypeStruct((bsz, RET_HEADS, RET_KEY_DIM, RET_VAL_DIM), F32),
        ],
        scratch_shapes=[
            pltpu.VMEM((4, hist + t, LANES), BF16),
            pltpu.VMEM((t // GROUP + 1, LANES, GROUP), BF16),
            pltpu.VMEM((RET_HEADS, RET_KEY_DIM, RET_VAL_DIM), F32),
        ],
        compiler_params=_params("arbitrary", "arbitrary"),
        name="mix_prompt",
    )(cdec, qa, ka, va, qr, kr, vr, gr, bias4, sink4, dmask, qdec, kdec)


def _mix_sample_kernel(cdec_ref, qa_ref, kvn_ref, ck_ref, cv_ref, qr_ref, kr_ref, vr_ref, gr_ref, s0_ref,
                       bias_ref, sink_ref, dmask_ref, qdec_ref, kdec_ref,
                       mixed_ref, state_ref, *, n_new, n_cache):
    pad = jnp.zeros((SAMPLE_KEYS - n_cache - n_new, LANES), F32)
    for g in range(SAMPLE_GROUP):
        kvn = kvn_ref[g]
        k_all = jnp.concatenate([ck_ref[g], kvn[:, :ATT_KV_W], pad], axis=0)
        v_all = jnp.concatenate([cv_ref[g], kvn[:, ATT_KV_W:], pad], axis=0)
        k4 = [p.astype(BF16) for p in _lane_split(k_all)]
        v4 = [p.astype(BF16) for p in _lane_split(v_all)]
        for k in range(ATT_KV_HEADS):
            c0 = 2 * LANES * k
            q2 = jnp.concatenate([qa_ref[g, :, c0:c0 + LANES], qa_ref[g, :, c0 + LANES:c0 + 2 * LANES]], axis=0)
            o = jnp.zeros((2 * n_new, LANES), F32)
            for e in range(2):
                n = 2 * k + e
                s = _dot_nt(q2, k4[n]) * ATT_SCALE + bias_ref[n]
                p = _sink_softmax(s, sink_ref[n])
                o = o + _dot(p.astype(BF16), v4[n])
            mixed_ref[g, :, c0:c0 + LANES] = o[:n_new].astype(BF16)
            mixed_ref[g, :, c0 + LANES:c0 + 2 * LANES] = o[n_new:].astype(BF16)

        for h in range(RET_HEADS):
            cols = slice(h * LANES, (h + 1) * LANES)
            q = qr_ref[g, :, cols]
            k = kr_ref[g, :, cols]
            v = vr_ref[g, :, cols]
            state = s0_ref[g, h]
            scores = _dot_nt(q, k) * dmask_ref[h]
            intra = _dot(scores.astype(BF16), v)
            qd = (q.astype(F32) * qdec_ref[h]).astype(BF16)
            inter = _dot(qd, state.astype(BF16))
            kd = (k.astype(F32) * kdec_ref[h]).astype(BF16)
            state_ref[g, h] = cdec_ref[h] * state + _dot_tn(kd, v)
            out = _gated_group_norm(intra + inter, gr_ref[g, :, cols])
            mixed_ref[g, :, ATT_Q_W + h * LANES:ATT_Q_W + (h + 1) * LANES] = out.astype(BF16)


def _mix_sample_call(qa, kvn, ck, cv, qr, kr, vr, gr, s0, bias4, sink4, dmask, qdec, kdec, cdec):
    dbsz, n_new, _ = qa.shape
    n_cache = ck.shape[1]
    grp = lambda *tail: pl.BlockSpec((SAMPLE_GROUP,) + tail, lambda b: (b,) + (0,) * len(tail))
    return pl.pallas_call(
        functools.partial(_mix_sample_kernel, n_new=n_new, n_cache=n_cache),
        grid=(dbsz // SAMPLE_GROUP,),
        in_specs=[
            pl.BlockSpec(memory_space=pltpu.SMEM),
            grp(n_new, ATT_Q_W), grp(n_new, 2 * ATT_KV_W), grp(n_cache, ATT_KV_W), grp(n_cache, ATT_KV_W),
            grp(n_new, RET_W), grp(n_new, RET_W), grp(n_new, RET_W), grp(n_new, RET_W),
            grp(RET_HEADS, RET_KEY_DIM, RET_VAL_DIM),
            _const_spec(bias4.shape), _const_spec(sink4.shape), _const_spec(dmask.shape),
            _const_spec(qdec.shape), _const_spec(kdec.shape),
        ],
        out_specs=[grp(n_new, D_MODEL), grp(RET_HEADS, RET_KEY_DIM, RET_VAL_DIM)],
        out_shape=[
            jax.ShapeDtypeStruct((dbsz, n_new, D_MODEL), BF16),
            jax.ShapeDtypeStruct((dbsz, RET_HEADS, RET_KEY_DIM, RET_VAL_DIM), F32),
        ],
        compiler_params=_params("arbitrary"),
        name="mix_sample",
    )(cdec, qa, kvn, ck, cv, qr, kr, vr, gr, s0, bias4, sink4, dmask, qdec, kdec)


def _dense_kernel(x_ref, mixed_ref, ga1_ref, sh2_ref, sc2_ref, ga2_ref, g_ref, gf_ref,
                  wo_ref, wu_ref, wd_ref, o_ref, *, final):
    x1 = x_ref[0] + ga1_ref[0] * _dot(mixed_ref[0], wo_ref[...])
    h = ((_rms(x1) * g_ref[...]) * (1.0 + sc2_ref[0]) + sh2_ref[0]).astype(BF16)
    acc = jnp.zeros(x1.shape, F32)
    for j in range(D_FF // FF_BLOCK):
        u = _dot(h, wu_ref[:, j * FF_BLOCK:(j + 1) * FF_BLOCK])
        u = jnp.square(jnp.maximum(u, 0.0)).astype(BF16)
        acc = acc + _dot(u, wd_ref[j * FF_BLOCK:(j + 1) * FF_BLOCK, :])
    x2 = x1 + ga2_ref[0] * acc
    o_ref[0] = _rms(x2) * gf_ref[...] if final else x2


def _dense_call(x, mixed, ga1, sh2, sc2, ga2, gain, g_final, w_out, w_up, w_down, final):
    bsz, seq, d = x.shape
    t = min(TOKEN_BLOCK, seq)
    tok = lambda w: pl.BlockSpec((1, t, w), lambda b, i: (b, i, 0))
    return pl.pallas_call(
        functools.partial(_dense_kernel, final=final),
        grid=(bsz, seq // t),
        in_specs=[
            tok(d), tok(d), _mod_spec(ga1, t), _mod_spec(sh2, t), _mod_spec(sc2, t), _mod_spec(ga2, t),
            _const_spec((1, d)), _const_spec((1, d)),
            _const_spec(w_out.shape), _const_spec(w_up.shape), _const_spec(w_down.shape),
        ],
        out_specs=tok(d),
        out_shape=jax.ShapeDtypeStruct((bsz, seq, d), F32),
        compiler_params=_params("arbitrary", "arbitrary"),
        name="dense",
    )(x, mixed, ga1, sh2, sc2, ga2, gain.reshape(1, d), g_final.reshape(1, d), w_out, w_up, w_down)


def _rotary_tables(pos):
    half = RET_KEY_DIM // 2
    inv = 1.0 / (ROPE_BASE ** (jnp.arange(half, dtype=F32) / half))
    ang = pos.astype(F32)[:, None] * inv[None, :]
    cos, sin = jnp.cos(ang), jnp.sin(ang)
    return jnp.concatenate([cos, cos], axis=1), jnp.concatenate([-sin, sin], axis=1)


def _retention_tables(n):
    log_g = jnp.log(1.0 - 2.0 ** (-5.0 - jnp.arange(RET_HEADS, dtype=F32)))
    idx = jnp.arange(n)
    diff = (idx[:, None] - idx[None, :]).astype(F32)
    mask = diff >= 0
    dmask = jnp.where(mask[None], jnp.exp(jnp.where(mask, diff, 0.0)[None] * log_g[:, None, None]), 0.0)
    pos = jnp.arange(n, dtype=F32)
    qdec = jnp.exp((pos + 1.0)[None, :] * log_g[:, None])
    kdec = jnp.exp((n - 1.0 - pos)[None, :] * log_g[:, None])
    cdec = jnp.exp(n * log_g)
    wide = lambda a: jnp.broadcast_to(a[:, :, None], (RET_HEADS, n, LANES))
    return dmask, wide(qdec), wide(kdec), cdec


def kernel(x_prompt, x_sample, c_prompt, c_sample, cache_win_k, cache_win_v, state_ret, g_mix, g_mlp, w_ada,
           b_ada, w_in, w_out, att_sinks, rel_bias, w_up, w_down, g_final):
    depth = w_in.shape[0]
    bsz, seq, d = x_prompt.shape
    dbsz, dec_seq, _ = x_sample.shape
    n_cache = cache_win_k.shape[2]
    n_rows = dbsz * dec_seq
    keep_p = min(WINDOW, seq)

    w_in_b, w_out_b, w_up_b, w_down_b = (w.astype(BF16) for w in (w_in, w_out, w_up, w_down))

    c_all = jnp.concatenate([c_prompt, c_sample], axis=0)
    c_all = jnp.pad(c_all, ((0, -c_all.shape[0] % 8), (0, 0)))
    mods = _ada_call(c_all, w_ada, b_ada).reshape(depth, c_all.shape[0], N_MOD, d)
    mods_p = [[mods[l, :bsz, m][:, None, :] for m in range(N_MOD)] for l in range(depth)]
    mods_s = [[jnp.repeat(mods[l, bsz:bsz + dbsz, m], dec_seq, axis=0)[None] for m in range(N_MOD)]
              for l in range(depth)]

    cos_p, sin_p = _rotary_tables(jnp.arange(seq))
    cos_s, sin_s = _rotary_tables(jnp.tile(PAST_LEN + jnp.arange(dec_seq), dbsz))
    ret_p = _retention_tables(GROUP)
    ret_s = _retention_tables(dec_seq)

    rel_p = (jnp.arange(BAND) - BAND_CHUNKS * CHUNK)[None, :] - jnp.arange(CHUNK)[:, None]
    bias_p = _banded_bias(_bias_call(rel_bias, _rel_bucket(rel_p)))
    kpos = PAST_LEN - n_cache + jnp.arange(n_cache + dec_seq)
    qpos = PAST_LEN + jnp.arange(dec_seq)
    bias_s = _bias_call(rel_bias, _rel_bucket(kpos[None, :] - qpos[:, None]))
    bias_s = jnp.pad(bias_s, ((0, 0), (0, 0), (0, SAMPLE_KEYS - n_cache - dec_seq)), constant_values=-jnp.inf)
    bias_s = _pair_rows(bias_s)

    xp = x_prompt
    xs = x_sample.reshape(1, n_rows, d)
    pk, pv, pr, sk, sv, sr = [], [], [], [], [], []
    for l in range(depth):
        sinks = att_sinks[l].astype(F32)[:, None, None]
        sink_p = _pair_lanes(jnp.broadcast_to(sinks, (ATT_HEADS, 1, CHUNK)))
        sink_s = _pair_rows(jnp.broadcast_to(sinks, (ATT_HEADS, dec_seq, 1)))
        final = l == depth - 1

        sh1, sc1, ga1, sh2, sc2, ga2 = mods_p[l]
        qa, ka, va, qr, kr, vr, gr, kvt = _inproj_call(xp, sh1, sc1, g_mix[l], w_in_b[l], cos_p, sin_p, keep_p)
        mixed, st = _mix_prompt_call(qa, ka, va, qr, kr, vr, gr, bias_p, sink_p, *ret_p)
        xp = _dense_call(xp, mixed, ga1, sh2, sc2, ga2, g_mlp[l], g_final, w_out_b[l], w_up_b[l], w_down_b[l],
                         final)
        pk.append(kvt[:, :, :ATT_KV_W].reshape(bsz, keep_p, ATT_KV_HEADS, ATT_HEAD_DIM))
        pv.append(kvt[:, :, ATT_KV_W:].reshape(bsz, keep_p, ATT_KV_HEADS, ATT_HEAD_DIM))
        pr.append(st)

        sh1, sc1, ga1, sh2, sc2, ga2 = mods_s[l]
        qa, _, _, qr, kr, vr, gr, kvn = _inproj_call(xs, sh1, sc1, g_mix[l], w_in_b[l], cos_s, sin_s, n_rows)
        per_seq = lambda a: a.reshape(dbsz, dec_seq, a.shape[-1])
        ck = cache_win_k[l].reshape(dbsz, n_cache, ATT_KV_W)
        cv = cache_win_v[l].reshape(dbsz, n_cache, ATT_KV_W)
        kvn = per_seq(kvn)
        mixed, st = _mix_sample_call(per_seq(qa), kvn, ck, cv, per_seq(qr), per_seq(kr), per_seq(vr), per_seq(gr),
                                     state_ret[l], bias_s, sink_s, *ret_s)
        xs = _dense_call(xs, mixed.reshape(1, n_rows, d), ga1, sh2, sc2, ga2, g_mlp[l], g_final, w_out_b[l],
                         w_up_b[l], w_down_b[l], final)
        k_roll = jnp.concatenate([ck, kvn[:, :, :ATT_KV_W]], axis=1)[:, dec_seq:]
        v_roll = jnp.concatenate([cv, kvn[:, :, ATT_KV_W:]], axis=1)[:, dec_seq:]
        sk.append(k_roll.reshape(dbsz, n_cache, ATT_KV_HEADS, ATT_HEAD_DIM))
        sv.append(v_roll.reshape(dbsz, n_cache, ATT_KV_HEADS, ATT_HEAD_DIM))
        sr.append(st)

    return (xp, xs.reshape(dbsz, dec_seq, d), jnp.stack(pk), jnp.stack(pv), jnp.stack(pr),
            jnp.stack(sk), jnp.stack(sv), jnp.stack(sr))
```

```python
import functools
import math

import jax
import jax.numpy as jnp
from jax import lax
from jax.experimental import pallas as pl
from jax.experimental.pallas import tpu as pltpu

F32 = jnp.float32
BF16 = jnp.bfloat16

D_MODEL = 1024
CHUNK = 64
ATT_HEADS = 8
ATT_KV_HEADS = 2
ATT_HEAD_DIM = 64
ATT_Q_W = ATT_HEADS * ATT_HEAD_DIM
ATT_KV_W = ATT_KV_HEADS * ATT_HEAD_DIM
ATT_SCALE = ATT_HEAD_DIM ** -0.5
WINDOW = 128
BAND_CHUNKS = -(-WINDOW // CHUNK)
BAND = (BAND_CHUNKS + 1) * CHUNK
REL_BUCKETS = 32
REL_MAX_DIST = 128
RET_HEADS = 4
RET_KEY_DIM = 128
RET_VAL_DIM = 128
RET_W = RET_HEADS * RET_KEY_DIM
ROPE_BASE = 10000.0
IN_W = ATT_Q_W + 2 * ATT_KV_W + 4 * RET_W
D_FF = 4 * D_MODEL
N_MOD = 6
EPS = 1e-6
PAST_LEN = 4096

LANES = 128
HALF_LANES = LANES // 2
VMEM_LIMIT_BYTES = 56 * 1024 * 1024

OFF_QA = 0
OFF_KV = OFF_QA + ATT_Q_W
OFF_QR = OFF_KV + 2 * ATT_KV_W
OFF_KR = OFF_QR + RET_W
OFF_VR = OFF_KR + RET_W
OFF_GR = OFF_VR + RET_W

TOKEN_BLOCK = 512
MIX_TOKEN_BLOCK = 1024
GROUP = 128
FF_BLOCK = 1024
ADA_COL_BLOCK = 1024
SAMPLE_GROUP = 4
SAMPLE_KEYS = 2 * LANES


def _params(*sem):
    return pltpu.CompilerParams(dimension_semantics=sem, vmem_limit_bytes=VMEM_LIMIT_BYTES)


def _const_spec(shape):
    zeros = (0,) * len(shape)
    return pl.BlockSpec(shape, lambda *_: zeros, pipeline_mode=pl.Buffered(1))


def _rms(x):
    return x * lax.rsqrt(jnp.mean(x * x, axis=-1, keepdims=True) + EPS)


def _dot(a, b):
    return jnp.dot(a, b, preferred_element_type=F32)


def _dot_nt(a, b):
    return lax.dot_general(a, b, (((1,), (1,)), ((), ())), preferred_element_type=F32)


def _dot_tn(a, b):
    return lax.dot_general(a, b, (((0,), (0,)), ((), ())), preferred_element_type=F32)


def _ada_kernel(c_ref, w_ref, b_ref, o_ref):
    a = jax.nn.silu(c_ref[...]).astype(BF16)
    o_ref[0] = _dot(a, w_ref[0].astype(BF16)) + b_ref[0]


def _ada_call(c_all, w_ada, b_ada):
    depth, d, n = w_ada.shape
    rows = c_all.shape[0]
    return pl.pallas_call(
        _ada_kernel,
        grid=(depth, n // ADA_COL_BLOCK),
        in_specs=[
            pl.BlockSpec((rows, d), lambda l, j: (0, 0)),
            pl.BlockSpec((1, d, ADA_COL_BLOCK), lambda l, j: (l, 0, j)),
            pl.BlockSpec((1, 1, ADA_COL_BLOCK), lambda l, j: (l, 0, j)),
        ],
        out_specs=pl.BlockSpec((1, rows, ADA_COL_BLOCK), lambda l, j: (l, 0, j)),
        out_shape=jax.ShapeDtypeStruct((depth, rows, n), F32),
        compiler_params=_params("arbitrary", "arbitrary"),
        name="ada",
    )(c_all, w_ada, b_ada.reshape(depth, 1, n))


def _bias_kernel(tab_ref, idx_ref, o_ref):
    h = pl.program_id(0)
    idx = idx_ref[...]
    acc = jnp.zeros(idx.shape, F32)
    for b in range(REL_BUCKETS):
        acc = jnp.where(idx == b, tab_ref[b, h], acc)
    o_ref[0] = acc


def _bias_call(rel_bias, bucket):
    q, j = bucket.shape
    return pl.pallas_call(
        _bias_kernel,
        grid=(ATT_HEADS,),
        in_specs=[
            pl.BlockSpec(memory_space=pltpu.SMEM),
            pl.BlockSpec((q, j), lambda h: (0, 0)),
        ],
        out_specs=pl.BlockSpec((1, q, j), lambda h: (h, 0, 0)),
        out_shape=jax.ShapeDtypeStruct((ATT_HEADS, q, j), F32),
        compiler_params=_params("arbitrary"),
        name="rel_bias",
    )(rel_bias, bucket)


def _rel_bucket(rel):
    nb = REL_BUCKETS // 2
    n = -rel
    ret = jnp.where(n < 0, nb, 0)
    n = jnp.abs(n)
    max_exact = nb // 2
    nf = jnp.maximum(n, 1).astype(F32)
    large = max_exact + (jnp.log(nf / max_exact) / math.log(REL_MAX_DIST / max_exact)
                         * (nb - max_exact)).astype(jnp.int32)
    large = jnp.minimum(large, nb - 1)
    return (ret + jnp.where(n < max_exact, n, large)).astype(jnp.int32)


def _pair_rows(per_head):
    out = []
    for k in range(ATT_KV_HEADS):
        for e in range(2):
            out.append(jnp.concatenate([per_head[4 * k + e], per_head[4 * k + 2 + e]], axis=0))
    return jnp.stack(out)


def _pair_lanes(per_head):
    out = []
    for k in range(ATT_KV_HEADS):
        for e in range(2):
            out.append(jnp.concatenate([per_head[4 * k + e], per_head[4 * k + 2 + e]], axis=-1))
    return jnp.stack(out)


def _banded_bias(per_head):
    base = _pair_lanes(jnp.swapaxes(per_head, 1, 2))
    key = jnp.arange(BAND)[None, :, None]
    return jnp.stack([jnp.where(key < (BAND_CHUNKS - v) * CHUNK, -jnp.inf, base)
                      for v in range(BAND_CHUNKS + 1)])


def _rotary(u, cos2, sin2):
    parts = []
    for h in range(RET_HEADS):
        x = u[:, h * RET_KEY_DIM:(h + 1) * RET_KEY_DIM]
        parts.append(x * cos2 + pltpu.roll(x, HALF_LANES, 1) * sin2)
    return jnp.concatenate(parts, axis=1)


def _inproj_kernel(x_ref, sh_ref, sc_ref, g_ref, w_ref, ac_ref, as_ref, bc_ref, bs_ref, bcn_ref, bsn_ref,
                   qa_ref, ka_ref, va_ref, qr_ref, kr_ref, vr_ref, gr_ref, kvt_ref, *, tail):
    i = pl.program_id(1)
    x = x_ref[0]
    h = (_rms(x) * g_ref[...]) * (1.0 + sc_ref[...]) + sh_ref[...]
    hb = h.astype(BF16)
    cos2 = ac_ref[...] * bc_ref[...] - as_ref[...] * bs_ref[...]
    sin2 = as_ref[...] * bcn_ref[...] + ac_ref[...] * bsn_ref[...]

    qa_ref[0] = _dot(hb, w_ref[:, OFF_QA:OFF_KV]).astype(BF16)
    kv = _dot(hb, w_ref[:, OFF_KV:OFF_QR])
    ka_ref[0] = kv[:, :ATT_KV_W].astype(BF16)
    va_ref[0] = kv[:, ATT_KV_W:].astype(BF16)
    qr_ref[0] = _rotary(_dot(hb, w_ref[:, OFF_QR:OFF_KR]), cos2, sin2).astype(BF16)
    kr = _rotary(_dot(hb, w_ref[:, OFF_KR:OFF_VR]), cos2, sin2) * (RET_KEY_DIM ** -0.5)
    kr_ref[0] = kr.astype(BF16)
    vr_ref[0] = _dot(hb, w_ref[:, OFF_VR:OFF_GR]).astype(BF16)
    gr_ref[0] = _dot(hb, w_ref[:, OFF_GR:IN_W]).astype(BF16)

    @pl.when(i == pl.num_programs(1) - 1)
    def _():
        kvt_ref[0] = kv[kv.shape[0] - tail:, :]


MOD_SHIFT1, MOD_SCALE1, MOD_GATE1, MOD_SHIFT2, MOD_SCALE2, MOD_GATE2 = range(N_MOD)


def _mod_specs(mods, layer, which, t):
    if mods.ndim == 4:
        return [pl.BlockSpec((None, None, 1, D_MODEL), lambda b, i, m=m: (layer, b, 0, m)) for m in which]
    return [pl.BlockSpec((None, t, D_MODEL), lambda b, i, m=m: (layer, i, m)) for m in which]


def _layer_spec(arr, layer):
    tail_zeros = (0,) * (arr.ndim - 1)
    return pl.BlockSpec((None,) + arr.shape[1:], lambda *_: (layer,) + tail_zeros, pipeline_mode=pl.Buffered(1))


def _inproj_call(x, mods, gains, w_in, layer, rot, tail):
    bsz, seq, d = x.shape
    t = min(TOKEN_BLOCK, seq)
    nb = seq // t
    base_cos, base_sin, off = rot
    assert base_cos.shape == (nb, 1, LANES) and off[0].shape == (t, LANES)
    tok = lambda w: pl.BlockSpec((1, t, w), lambda b, i: (b, i, 0))
    out_bf = lambda w: jax.ShapeDtypeStruct((bsz, seq, w), BF16)
    base = pl.BlockSpec((None, 1, LANES), lambda b, i: (i, 0, 0))
    return pl.pallas_call(
        functools.partial(_inproj_kernel, tail=tail),
        grid=(bsz, nb),
        in_specs=[
            tok(d), *_mod_specs(mods, layer, (MOD_SHIFT1, MOD_SCALE1), t),
            _layer_spec(gains, layer), _layer_spec(w_in, layer),
            base, base, *[_const_spec((t, LANES))] * 4,
        ],
        out_specs=[
            tok(ATT_Q_W), tok(ATT_KV_W), tok(ATT_KV_W), tok(RET_W), tok(RET_W), tok(RET_W), tok(RET_W),
            pl.BlockSpec((1, tail, 2 * ATT_KV_W), lambda b, i: (b, 0, 0)),
        ],
        out_shape=[
            out_bf(ATT_Q_W), out_bf(ATT_KV_W), out_bf(ATT_KV_W), out_bf(RET_W), out_bf(RET_W),
            out_bf(RET_W), out_bf(RET_W),
            jax.ShapeDtypeStruct((bsz, tail, 2 * ATT_KV_W), F32),
        ],
        compiler_params=_params("arbitrary", "arbitrary"),
        name="inproj",
    )(x, mods, mods, gains, w_in, base_cos, base_sin, *off)


def _lane_split(x, scale=1.0):
    lo = lax.broadcasted_iota(jnp.int32, x.shape, 1) < HALF_LANES
    xf = x.astype(F32) * scale
    sw = pltpu.roll(xf, HALF_LANES, 1)
    zero = jnp.zeros_like(xf)
    parts = (jnp.where(lo, xf, zero), jnp.where(lo, zero, sw),
             jnp.where(lo, sw, zero), jnp.where(lo, zero, xf))
    return tuple(p.astype(x.dtype) for p in parts)


def _sink_softmax(s, sink):
    m = jnp.maximum(jnp.max(s, axis=-1, keepdims=True), sink)
    e = jnp.exp(s - m)
    den = jnp.sum(e, axis=-1, keepdims=True) + jnp.exp(sink - m)
    return e / den


def _gated_group_norm(ro, gate):
    g = gate.astype(F32)
    return _rms(ro) * jax.nn.silu(g)


def _mix_prompt_kernel(cdec_ref, qa_ref, ka_ref, va_ref, qr_ref, kr_ref, vr_ref, gr_ref,
                       bias_ref, sink_ref, dmask_ref, qdec_ref, kdec_ref,
                       mixed_ref, state_ref, kx_ref, vt_ref, s_ref, *, t):
    i = pl.program_id(1)
    n_groups = t // GROUP
    hist = BAND - CHUNK

    @pl.when(i == 0)
    def _():
        kx_ref[:, 0:hist, :] = jnp.zeros((4, hist, LANES), BF16)
        vt_ref[0] = jnp.zeros((LANES, GROUP), BF16)
        s_ref[...] = jnp.zeros(s_ref.shape, F32)

    @pl.when(i > 0)
    def _():
        kx_ref[:, 0:hist, :] = kx_ref[:, t:t + hist, :]
        vt_ref[0] = vt_ref[n_groups]

    for n, part in enumerate(_lane_split(ka_ref[0], ATT_SCALE)):
        kx_ref[n, hist:hist + t, :] = part
    for m in range(n_groups):
        vt_ref[1 + m] = va_ref[0, m * GROUP:(m + 1) * GROUP, :].astype(F32).T.astype(BF16)

    no_keys = jnp.zeros((CHUNK, LANES), BF16)

    def group(g, carry):
        g0 = pl.multiple_of(g * GROUP, GROUP)
        grows = pl.ds(g0, GROUP)
        heads = [slice(h * LANES, (h + 1) * LANES) for h in range(RET_HEADS)]
        att = [(j, k, e) for j in range(GROUP // CHUNK) for k in range(ATT_KV_HEADS) for e in range(2)]

        def chunk_rows(j):
            return pl.ds(pl.multiple_of(g0 + j * CHUNK, CHUNK), CHUNK)

        raw = {}
        for j, k, e in att:
            c0 = 2 * LANES * k
            q2 = jnp.concatenate([qa_ref[0, chunk_rows(j), c0:c0 + LANES],
                                  qa_ref[0, chunk_rows(j), c0 + LANES:c0 + 2 * LANES]], axis=0)
            band = pl.ds(pl.multiple_of(g0 + j * CHUNK, CHUNK), BAND)
            raw[j, k, e] = _dot_nt(kx_ref[2 * k + e, band, :], q2)
        states = [s_ref[h] for h in range(RET_HEADS)]
        ret_q = [qr_ref[0, grows, c] for c in heads]
        ret_k = [kr_ref[0, grows, c] for c in heads]
        ret_v = [vr_ref[0, grows, c] for c in heads]
        ret_scores = [_dot_nt(ret_q[h], ret_k[h]) for h in range(RET_HEADS)]
        ret_inter = [_dot(ret_q[h], states[h].astype(BF16)) for h in range(RET_HEADS)]
        for h in range(RET_HEADS):
            kd = (ret_k[h].astype(F32) * kdec_ref[h]).astype(BF16)
            s_ref[h] = cdec_ref[h] * states[h] + _dot_tn(kd, ret_v[h])

        probs, dens = {}, {}
        for j, k, e in att:
            n = 2 * k + e
            variant = jnp.minimum(i * (t // CHUNK) + 2 * g + j, BAND_CHUNKS)
            s = raw[j, k, e] + bias_ref[variant, n]
            sink = sink_ref[n]
            m = jnp.maximum(jnp.max(s, axis=0, keepdims=True), sink)
            p = jnp.exp(s - m)
            dens[j, k, e] = jnp.sum(p, axis=0, keepdims=True) + jnp.exp(sink - m)
            probs[j, k, e] = p.astype(BF16)
        ret_p = [(ret_scores[h] * dmask_ref[h]).astype(BF16) for h in range(RET_HEADS)]

        out_t = {}
        for j, k, e in att:
            p = probs[j, k, e]
            if j == 0:
                pa, pb = p[:GROUP], jnp.concatenate([p[GROUP:], no_keys], axis=0)
            else:
                pa, pb = jnp.concatenate([no_keys, p[:CHUNK]], axis=0), p[CHUNK:]
            v_rows = slice(k * ATT_HEAD_DIM, (k + 1) * ATT_HEAD_DIM)
            out_t[j, k, e] = _dot(vt_ref[g, v_rows, :], pa) + _dot(vt_ref[g + 1, v_rows, :], pb)
        ret_intra = [_dot(ret_p[h], ret_v[h]) for h in range(RET_HEADS)]

        for j in range(GROUP // CHUNK):
            for k in range(ATT_KV_HEADS):
                c0 = 2 * LANES * k
                o = jnp.concatenate([out_t[j, k, e] * (1.0 / dens[j, k, e]) for e in range(2)], axis=0).T
                mixed_ref[0, chunk_rows(j), c0:c0 + LANES] = o[:CHUNK].astype(BF16)
                mixed_ref[0, chunk_rows(j), c0 + LANES:c0 + 2 * LANES] = o[CHUNK:].astype(BF16)
        for h in range(RET_HEADS):
            ro = ret_intra[h] + ret_inter[h] * qdec_ref[h]
            out = _gated_group_norm(ro, gr_ref[0, grows, heads[h]])
            mixed_ref[0, grows, ATT_Q_W + h * LANES:ATT_Q_W + (h + 1) * LANES] = out.astype(BF16)
        return carry

    lax.fori_loop(0, n_groups, group, 0, unroll=4)

    @pl.when(i == pl.num_programs(1) - 1)
    def _():
        state_ref[0] = s_ref[...]


def _mix_prompt_call(qa, ka, va, qr, kr, vr, gr, bias4, sink4, dmask, qdec, kdec, cdec):
    bsz, seq, _ = qa.shape
    t = min(MIX_TOKEN_BLOCK, seq)
    hist = BAND - CHUNK
    assert hist == GROUP and t % GROUP == 0 and seq % t == 0
    tok = lambda w: pl.BlockSpec((1, t, w), lambda b, i: (b, i, 0))
    return pl.pallas_call(
        functools.partial(_mix_prompt_kernel, t=t),
        grid=(bsz, seq // t),
        in_specs=[
            pl.BlockSpec(memory_space=pltpu.SMEM),
            tok(ATT_Q_W), tok(ATT_KV_W), tok(ATT_KV_W), tok(RET_W), tok(RET_W), tok(RET_W), tok(RET_W),
            _const_spec(bias4.shape), _const_spec(sink4.shape), _const_spec(dmask.shape),
            _const_spec(qdec.shape), _const_spec(kdec.shape),
        ],
        out_specs=[
            tok(D_MODEL),
            pl.BlockSpec((1, RET_HEADS, RET_KEY_DIM, RET_VAL_DIM), lambda b, i: (b, 0, 0, 0)),
        ],
        out_shape=[
            jax.ShapeDtypeStruct((bsz, seq, D_MODEL), BF16),
            jax.ShapeDtypeStruct((bsz, RET_HEADS, RET_KEY_DIM, RET_VAL_DIM), F32),
        ],
        scratch_shapes=[
            pltpu.VMEM((4, hist + t, LANES), BF16),
            pltpu.VMEM((t // GROUP + 1, LANES, GROUP), BF16),
            pltpu.VMEM((RET_HEADS, RET_KEY_DIM, RET_VAL_DIM), F32),
        ],
        compiler_params=_params("arbitrary", "arbitrary"),
        name="mix_prompt",
    )(cdec, qa, ka, va, qr, kr, vr, gr, bias4, sink4, dmask, qdec, kdec)


def _mix_sample_kernel(cdec_ref, qa_ref, kvn_ref, ck_ref, cv_ref, qr_ref, kr_ref, vr_ref, gr_ref, s0_ref,
                       bias_ref, sink_ref, dmask_ref, qdec_ref, kdec_ref, *rest, n_new, n_cache):
    mixed_ref, *stack_refs = rest[-4:]
    done = 0
    if len(rest) > 4:
        done = rest[0].shape[0]
        for prev_ref, out_ref in zip(rest[:3], stack_refs):
            out_ref[0:done] = prev_ref[...]
    state_ref, wk_ref, wv_ref = (r.at[done] for r in stack_refs)
    pad = jnp.zeros((SAMPLE_KEYS - n_cache - n_new, LANES), F32)
    for g in range(SAMPLE_GROUP):
        kvn = kvn_ref[g]
        k_all = jnp.concatenate([ck_ref[g], kvn[:, :ATT_KV_W], pad], axis=0)
        v_all = jnp.concatenate([cv_ref[g], kvn[:, ATT_KV_W:], pad], axis=0)
        wk_ref[g] = k_all[n_new:n_new + n_cache]
        wv_ref[g] = v_all[n_new:n_new + n_cache]
        k4 = [p.astype(BF16) for p in _lane_split(k_all)]
        v4 = [p.astype(BF16) for p in _lane_split(v_all)]
        for k in range(ATT_KV_HEADS):
            c0 = 2 * LANES * k
            q2 = jnp.concatenate([qa_ref[g, :, c0:c0 + LANES], qa_ref[g, :, c0 + LANES:c0 + 2 * LANES]], axis=0)
            o = jnp.zeros((2 * n_new, LANES), F32)
            for e in range(2):
                n = 2 * k + e
                s = _dot_nt(q2, k4[n]) * ATT_SCALE + bias_ref[n]
                p = _sink_softmax(s, sink_ref[n])
                o = o + _dot(p.astype(BF16), v4[n])
            mixed_ref[g, :, c0:c0 + LANES] = o[:n_new].astype(BF16)
            mixed_ref[g, :, c0 + LANES:c0 + 2 * LANES] = o[n_new:].astype(BF16)

        for h in range(RET_HEADS):
            cols = slice(h * LANES, (h + 1) * LANES)
            q = qr_ref[g, :, cols]
            k = kr_ref[g, :, cols]
            v = vr_ref[g, :, cols]
            state = s0_ref[g, h]
            scores = _dot_nt(q, k) * dmask_ref[h]
            intra = _dot(scores.astype(BF16), v)
            qd = (q.astype(F32) * qdec_ref[h]).astype(BF16)
            inter = _dot(qd, state.astype(BF16))
            kd = (k.astype(F32) * kdec_ref[h]).astype(BF16)
            state_ref[g, h] = cdec_ref[h] * state + _dot_tn(kd, v)
            out = _gated_group_norm(intra + inter, gr_ref[g, :, cols])
            mixed_ref[g, :, ATT_Q_W + h * LANES:ATT_Q_W + (h + 1) * LANES] = out.astype(BF16)


def _mix_sample_call(qa, kvn, cache_k, cache_v, qr, kr, vr, gr, state, layer, stacks,
                     bias4, sink4, dmask, qdec, kdec, cdec):
    dbsz, n_new, _ = qa.shape
    n_cache = cache_k.shape[2]
    grp = lambda *tail: pl.BlockSpec((SAMPLE_GROUP,) + tail, lambda b: (b,) + (0,) * len(tail))
    lay = lambda *tail: pl.BlockSpec((None, SAMPLE_GROUP) + tail, lambda b: (layer, b) + (0,) * len(tail))
    stk = lambda n, *tail: pl.BlockSpec((n, SAMPLE_GROUP) + tail, lambda b: (0, b) + (0,) * len(tail))
    stack_tails = [(RET_HEADS, RET_KEY_DIM, RET_VAL_DIM), (n_cache, ATT_KV_W), (n_cache, ATT_KV_W)]
    operands = [cdec, qa, kvn, cache_k, cache_v, qr, kr, vr, gr, state, bias4, sink4, dmask, qdec, kdec]
    prev_specs = []
    if stacks is not None:
        assert all(s.shape[0] == layer for s in stacks)
        operands += list(stacks)
        prev_specs = [stk(layer, *tail) for tail in stack_tails]
    outs = pl.pallas_call(
        functools.partial(_mix_sample_kernel, n_new=n_new, n_cache=n_cache),
        grid=(dbsz // SAMPLE_GROUP,),
        in_specs=[
            pl.BlockSpec(memory_space=pltpu.SMEM),
            grp(n_new, ATT_Q_W), grp(n_new, 2 * ATT_KV_W), lay(n_cache, ATT_KV_W), lay(n_cache, ATT_KV_W),
            grp(n_new, RET_W), grp(n_new, RET_W), grp(n_new, RET_W), grp(n_new, RET_W),
            lay(RET_HEADS, RET_KEY_DIM, RET_VAL_DIM),
            _const_spec(bias4.shape), _const_spec(sink4.shape), _const_spec(dmask.shape),
            _const_spec(qdec.shape), _const_spec(kdec.shape),
        ] + prev_specs,
        out_specs=[grp(n_new, D_MODEL)] + [stk(layer + 1, *tail) for tail in stack_tails],
        out_shape=[jax.ShapeDtypeStruct((dbsz, n_new, D_MODEL), BF16)]
        + [jax.ShapeDtypeStruct((layer + 1, dbsz) + tail, F32) for tail in stack_tails],
        compiler_params=_params("arbitrary"),
        name="mix_sample",
    )(*operands)
    return outs[0], outs[1:]


def _dense_kernel(x_ref, mixed_ref, ga1_ref, sh2_ref, sc2_ref, ga2_ref, g_ref, gf_ref,
                  wo_ref, wu_ref, wd_ref, o_ref, *, final):
    x1 = x_ref[0] + ga1_ref[...] * _dot(mixed_ref[0], wo_ref[...])
    h = ((_rms(x1) * g_ref[...]) * (1.0 + sc2_ref[...]) + sh2_ref[...]).astype(BF16)
    acc = jnp.zeros(x1.shape, F32)
    for j in range(D_FF // FF_BLOCK):
        u = _dot(h, wu_ref[:, j * FF_BLOCK:(j + 1) * FF_BLOCK])
        u = jnp.square(jnp.maximum(u, 0.0)).astype(BF16)
        acc = acc + _dot(u, wd_ref[j * FF_BLOCK:(j + 1) * FF_BLOCK, :])
    x2 = x1 + ga2_ref[...] * acc
    o_ref[0] = _rms(x2) * gf_ref[...] if final else x2


def _dense_call(x, mixed, mods, gains, g_final, w_out, w_up, w_down, layer, final):
    bsz, seq, d = x.shape
    t = min(TOKEN_BLOCK, seq)
    tok = lambda w: pl.BlockSpec((1, t, w), lambda b, i: (b, i, 0))
    return pl.pallas_call(
        functools.partial(_dense_kernel, final=final),
        grid=(bsz, seq // t),
        in_specs=[
            tok(d), tok(d), *_mod_specs(mods, layer, (MOD_GATE1, MOD_SHIFT2, MOD_SCALE2, MOD_GATE2), t),
            _layer_spec(gains, layer), _const_spec((1, d)),
            _layer_spec(w_out, layer), _layer_spec(w_up, layer), _layer_spec(w_down, layer),
        ],
        out_specs=tok(d),
        out_shape=jax.ShapeDtypeStruct((bsz, seq, d), F32),
        compiler_params=_params("arbitrary", "arbitrary"),
        name="dense",
    )(x, mixed, mods, mods, mods, mods, gains, g_final.reshape(1, d), w_out, w_up, w_down)


def _rotary_tables(base_pos, off_pos):
    half = RET_KEY_DIM // 2
    inv = 1.0 / (ROPE_BASE ** (jnp.arange(half, dtype=F32) / half))
    sign = jnp.concatenate([-jnp.ones((half,), F32), jnp.ones((half,), F32)])

    def both(pos):
        ang = pos.astype(F32)[:, None] * inv[None, :]
        cos, sin = jnp.cos(ang), jnp.sin(ang)
        return jnp.concatenate([cos, cos], axis=1), jnp.concatenate([sin, sin], axis=1)

    base_cos, base_sin = both(base_pos)
    off_cos, off_sin = both(off_pos)
    return base_cos[:, None, :], base_sin[:, None, :], (off_cos, off_sin, sign * off_cos, sign * off_sin)


def _retention_tables(n):
    log_g = jnp.log(1.0 - 2.0 ** (-5.0 - jnp.arange(RET_HEADS, dtype=F32)))
    idx = jnp.arange(n)
    diff = (idx[:, None] - idx[None, :]).astype(F32)
    mask = diff >= 0
    dmask = jnp.where(mask[None], jnp.exp(jnp.where(mask, diff, 0.0)[None] * log_g[:, None, None]), 0.0)
    pos = jnp.arange(n, dtype=F32)
    qdec = jnp.exp((pos + 1.0)[None, :] * log_g[:, None])
    kdec = jnp.exp((n - 1.0 - pos)[None, :] * log_g[:, None])
    cdec = jnp.exp(n * log_g)
    wide = lambda a: jnp.broadcast_to(a[:, :, None], (RET_HEADS, n, LANES))
    return dmask, wide(qdec), wide(kdec), cdec


def kernel(x_prompt, x_sample, c_prompt, c_sample, cache_win_k, cache_win_v, state_ret, g_mix, g_mlp, w_ada,
           b_ada, w_in, w_out, att_sinks, rel_bias, w_up, w_down, g_final):
    depth = w_in.shape[0]
    bsz, seq, d = x_prompt.shape
    dbsz, dec_seq, _ = x_sample.shape
    n_cache = cache_win_k.shape[2]
    n_rows = dbsz * dec_seq
    keep_p = min(WINDOW, seq)

    w_in_b, w_out_b, w_up_b, w_down_b = (w.astype(BF16) for w in (w_in, w_out, w_up, w_down))
    g_mix3 = g_mix.reshape(depth, 1, d)
    g_mlp3 = g_mlp.reshape(depth, 1, d)
    cache_k = cache_win_k.reshape(depth, dbsz, n_cache, ATT_KV_W)
    cache_v = cache_win_v.reshape(depth, dbsz, n_cache, ATT_KV_W)

    c_all = jnp.concatenate([c_prompt, c_sample], axis=0)
    c_all = jnp.pad(c_all, ((0, -c_all.shape[0] % 8), (0, 0)))
    mods = _ada_call(c_all, w_ada, b_ada)
    mods_p = mods[:, :bsz, None, :]
    mods_s = jnp.repeat(mods[:, bsz:bsz + dbsz], dec_seq, axis=1)

    t_p = min(TOKEN_BLOCK, seq)
    rot_p = _rotary_tables(jnp.arange(seq // t_p) * t_p, jnp.arange(t_p))
    rot_s = _rotary_tables(jnp.full((1,), PAST_LEN), jnp.tile(jnp.arange(dec_seq), dbsz))
    ret_p = _retention_tables(GROUP)
    ret_s = _retention_tables(dec_seq)

    rel_p = (jnp.arange(BAND) - BAND_CHUNKS * CHUNK)[None, :] - jnp.arange(CHUNK)[:, None]
    bias_p = _banded_bias(_bias_call(rel_bias, _rel_bucket(rel_p)))
    kpos = PAST_LEN - n_cache + jnp.arange(n_cache + dec_seq)
    qpos = PAST_LEN + jnp.arange(dec_seq)
    bias_s = _bias_call(rel_bias, _rel_bucket(kpos[None, :] - qpos[:, None]))
    bias_s = jnp.pad(bias_s, ((0, 0), (0, 0), (0, SAMPLE_KEYS - n_cache - dec_seq)), constant_values=-jnp.inf)
    bias_s = _pair_rows(bias_s)

    xp = x_prompt
    xs = x_sample.reshape(1, n_rows, d)
    pk, pv, pr = [], [], []
    stacks = None
    for l in range(depth):
        sinks = att_sinks[l].astype(F32)[:, None, None]
        sink_p = _pair_lanes(jnp.broadcast_to(sinks, (ATT_HEADS, 1, CHUNK)))
        sink_s = _pair_rows(jnp.broadcast_to(sinks, (ATT_HEADS, dec_seq, 1)))
        final = l == depth - 1

        qa, ka, va, qr, kr, vr, gr, kvt = _inproj_call(xp, mods_p, g_mix3, w_in_b, l, rot_p, keep_p)
        mixed, st = _mix_prompt_call(qa, ka, va, qr, kr, vr, gr, bias_p, sink_p, *ret_p)
        xp = _dense_call(xp, mixed, mods_p, g_mlp3, g_final, w_out_b, w_up_b, w_down_b, l, final)
        pk.append(kvt[:, :, :ATT_KV_W].reshape(bsz, keep_p, ATT_KV_HEADS, ATT_HEAD_DIM))
        pv.append(kvt[:, :, ATT_KV_W:].reshape(bsz, keep_p, ATT_KV_HEADS, ATT_HEAD_DIM))
        pr.append(st)

        qa, _, _, qr, kr, vr, gr, kvn = _inproj_call(xs, mods_s, g_mix3, w_in_b, l, rot_s, n_rows)
        per_seq = lambda a: a.reshape(dbsz, dec_seq, a.shape[-1])
        mixed, stacks = _mix_sample_call(per_seq(qa), per_seq(kvn), cache_k, cache_v, per_seq(qr), per_seq(kr),
                                         per_seq(vr), per_seq(gr), state_ret, l, stacks, bias_s, sink_s, *ret_s)
        xs = _dense_call(xs, mixed.reshape(1, n_rows, d), mods_s, g_mlp3, g_final, w_out_b, w_up_b, w_down_b, l,
                         final)

    sr, sk, sv = stacks
    win_shape = (depth, dbsz, n_cache, ATT_KV_HEADS, ATT_HEAD_DIM)
    return (xp, xs.reshape(dbsz, dec_seq, d), jnp.stack(pk), jnp.stack(pv), jnp.stack(pr),
            sk.reshape(win_shape), sv.reshape(win_shape), sr)
```

```python
import functools
import math

import jax
import jax.numpy as jnp
from jax import lax
from jax.experimental import pallas as pl
from jax.experimental.pallas import tpu as pltpu

F32 = jnp.float32
BF16 = jnp.bfloat16

D_MODEL = 1024
CHUNK = 64
ATT_HEADS = 8
ATT_KV_HEADS = 2
ATT_HEAD_DIM = 64
ATT_Q_W = ATT_HEADS * ATT_HEAD_DIM
ATT_KV_W = ATT_KV_HEADS * ATT_HEAD_DIM
ATT_SCALE = ATT_HEAD_DIM ** -0.5
WINDOW = 128
BAND_CHUNKS = -(-WINDOW // CHUNK)
BAND = (BAND_CHUNKS + 1) * CHUNK
REL_BUCKETS = 32
REL_MAX_DIST = 128
RET_HEADS = 4
RET_KEY_DIM = 128
RET_VAL_DIM = 128
RET_W = RET_HEADS * RET_KEY_DIM
ROPE_BASE = 10000.0
IN_W = ATT_Q_W + 2 * ATT_KV_W + 4 * RET_W
D_FF = 4 * D_MODEL
N_MOD = 6
EPS = 1e-6
PAST_LEN = 4096
LOG2_E = math.log2(math.e)

LANES = 128
HALF_LANES = LANES // 2
BF16_SUBLANES = 16
VMEM_LIMIT_BYTES = 56 * 1024 * 1024

OFF_QA = 0
OFF_KV = OFF_QA + ATT_Q_W
OFF_QR = OFF_KV + 2 * ATT_KV_W
OFF_KR = OFF_QR + RET_W
OFF_VR = OFF_KR + RET_W
OFF_GR = OFF_VR + RET_W

TOKEN_BLOCK = 1024
MIX_TOKEN_BLOCK = 1024
GROUP = 128
FF_BLOCK = 1024
ADA_COL_BLOCK = 1024
SAMPLE_GROUP = 4
SAMPLE_KEYS = 2 * LANES


def _params(*sem):
    return pltpu.CompilerParams(dimension_semantics=sem, vmem_limit_bytes=VMEM_LIMIT_BYTES)


def _const_spec(shape):
    zeros = (0,) * len(shape)
    return pl.BlockSpec(shape, lambda *_: zeros, pipeline_mode=pl.Buffered(1))


def _rms(x):
    return x * lax.rsqrt(jnp.mean(x * x, axis=-1, keepdims=True) + EPS)


def _dot(a, b):
    return jnp.dot(a, b, preferred_element_type=F32)


def _dot_nt(a, b):
    return lax.dot_general(a, b, (((1,), (1,)), ((), ())), preferred_element_type=F32)


def _dot_tn(a, b):
    return lax.dot_general(a, b, (((0,), (0,)), ((), ())), preferred_element_type=F32)


def _ada_kernel(c_ref, w_ref, b_ref, o_ref):
    a = jax.nn.silu(c_ref[...]).astype(BF16)
    o_ref[0] = _dot(a, w_ref[0].astype(BF16)) + b_ref[0]


def _ada_call(c_all, w_ada, b_ada):
    depth, d, n = w_ada.shape
    rows = c_all.shape[0]
    return pl.pallas_call(
        _ada_kernel,
        grid=(depth, n // ADA_COL_BLOCK),
        in_specs=[
            pl.BlockSpec((rows, d), lambda l, j: (0, 0)),
            pl.BlockSpec((1, d, ADA_COL_BLOCK), lambda l, j: (l, 0, j)),
            pl.BlockSpec((1, 1, ADA_COL_BLOCK), lambda l, j: (l, 0, j)),
        ],
        out_specs=pl.BlockSpec((1, rows, ADA_COL_BLOCK), lambda l, j: (l, 0, j)),
        out_shape=jax.ShapeDtypeStruct((depth, rows, n), F32),
        compiler_params=_params("arbitrary", "arbitrary"),
        name="ada",
    )(c_all, w_ada, b_ada.reshape(depth, 1, n))


def _bias_kernel(tab_ref, idx_ref, o_ref):
    h = pl.program_id(0)
    idx = idx_ref[...]
    acc = jnp.zeros(idx.shape, F32)
    for b in range(REL_BUCKETS):
        acc = jnp.where(idx == b, tab_ref[b, h], acc)
    o_ref[0] = acc


def _bias_call(rel_bias, bucket):
    q, j = bucket.shape
    return pl.pallas_call(
        _bias_kernel,
        grid=(ATT_HEADS,),
        in_specs=[
            pl.BlockSpec(memory_space=pltpu.SMEM),
            pl.BlockSpec((q, j), lambda h: (0, 0)),
        ],
        out_specs=pl.BlockSpec((1, q, j), lambda h: (h, 0, 0)),
        out_shape=jax.ShapeDtypeStruct((ATT_HEADS, q, j), F32),
        compiler_params=_params("arbitrary"),
        name="rel_bias",
    )(rel_bias, bucket)


def _rel_bucket(rel):
    nb = REL_BUCKETS // 2
    n = -rel
    ret = jnp.where(n < 0, nb, 0)
    n = jnp.abs(n)
    max_exact = nb // 2
    nf = jnp.maximum(n, 1).astype(F32)
    large = max_exact + (jnp.log(nf / max_exact) / math.log(REL_MAX_DIST / max_exact)
                         * (nb - max_exact)).astype(jnp.int32)
    large = jnp.minimum(large, nb - 1)
    return (ret + jnp.where(n < max_exact, n, large)).astype(jnp.int32)


def _pair_rows(per_head):
    out = []
    for k in range(ATT_KV_HEADS):
        for e in range(2):
            out.append(jnp.concatenate([per_head[4 * k + e], per_head[4 * k + 2 + e]], axis=0))
    return jnp.stack(out)


def _pair_lanes(per_head):
    out = []
    for k in range(ATT_KV_HEADS):
        for e in range(2):
            out.append(jnp.concatenate([per_head[4 * k + e], per_head[4 * k + 2 + e]], axis=-1))
    return jnp.stack(out)


def _banded_bias(per_head):
    base = _pair_lanes(jnp.swapaxes(per_head, 1, 2))
    key = jnp.arange(BAND)[None, :, None]
    return jnp.stack([jnp.where(key < (BAND_CHUNKS - v) * CHUNK, -jnp.inf, base)
                      for v in range(BAND_CHUNKS + 1)])


def _rotary(u, cos2, sin2):
    parts = []
    for h in range(RET_HEADS):
        x = u[:, h * RET_KEY_DIM:(h + 1) * RET_KEY_DIM]
        parts.append(x * cos2 + pltpu.roll(x, HALF_LANES, 1) * sin2)
    return jnp.concatenate(parts, axis=1)


def _inproj_kernel(x_ref, sh_ref, sc_ref, g_ref, w_ref, ac_ref, as_ref, bc_ref, bs_ref, bcn_ref, bsn_ref,
                   qdec_ref, kdec_ref,
                   qa_ref, kx_ref, va_ref, qd_ref, kg_ref, vr_ref, sg_ref, kvt_ref, *, tail):
    i = pl.program_id(1)
    x = x_ref[0]
    h = (_rms(x) * g_ref[...]) * (1.0 + sc_ref[...]) + sh_ref[...]
    hb = h.astype(BF16)
    cos2 = ac_ref[...] * bc_ref[...] - as_ref[...] * bs_ref[...]
    sin2 = as_ref[...] * bcn_ref[...] + ac_ref[...] * bsn_ref[...]

    qa_ref[0] = (_dot(hb, w_ref[:, OFF_QA:OFF_KV]) * (ATT_SCALE * LOG2_E)).astype(BF16)
    kv = _dot(hb, w_ref[:, OFF_KV:OFF_QR])
    kx_ref[0] = jnp.concatenate(_lane_split(kv[:, :ATT_KV_W]), axis=1).astype(BF16)
    va_ref[0] = kv[:, ATT_KV_W:].astype(BF16)
    qd_ref[0] = (_rotary(_dot(hb, w_ref[:, OFF_QR:OFF_KR]), cos2, sin2) * qdec_ref[...]).astype(BF16)
    kg_ref[0] = (_rotary(_dot(hb, w_ref[:, OFF_KR:OFF_VR]), cos2, sin2) * kdec_ref[...]).astype(BF16)
    vr_ref[0] = _dot(hb, w_ref[:, OFF_VR:OFF_GR]).astype(BF16)
    sg_ref[0] = jax.nn.silu(_dot(hb, w_ref[:, OFF_GR:IN_W])).astype(BF16)

    @pl.when(i == pl.num_programs(1) - 1)
    def _():
        kvt_ref[0] = kv[kv.shape[0] - tail:, :]


MOD_SHIFT1, MOD_SCALE1, MOD_GATE1, MOD_SHIFT2, MOD_SCALE2, MOD_GATE2 = range(N_MOD)


def _mod_specs(mods, layer, which, t):
    if mods.ndim == 4:
        return [pl.BlockSpec((None, None, 1, D_MODEL), lambda b, i, m=m: (layer, b, 0, m)) for m in which]
    return [pl.BlockSpec((None, t, D_MODEL), lambda b, i, m=m: (layer, i, m)) for m in which]


def _layer_spec(arr, layer):
    tail_zeros = (0,) * (arr.ndim - 1)
    return pl.BlockSpec((None,) + arr.shape[1:], lambda *_: (layer,) + tail_zeros, pipeline_mode=pl.Buffered(1))


def _inproj_call(x, mods, gains, w_in, layer, rot, decay, tail):
    bsz, seq, d = x.shape
    t = min(TOKEN_BLOCK, seq)
    nb = seq // t
    base_cos, base_sin, off = rot
    assert base_cos.shape == (nb, 1, LANES) and off[0].shape == (t, LANES)
    assert all(a.shape == (t, RET_W) for a in decay)
    tok = lambda w: pl.BlockSpec((1, t, w), lambda b, i: (b, i, 0))
    out_bf = lambda w: jax.ShapeDtypeStruct((bsz, seq, w), BF16)
    base = pl.BlockSpec((None, 1, LANES), lambda b, i: (i, 0, 0))
    return pl.pallas_call(
        functools.partial(_inproj_kernel, tail=tail),
        grid=(bsz, nb),
        in_specs=[
            tok(d), *_mod_specs(mods, layer, (MOD_SHIFT1, MOD_SCALE1), t),
            _layer_spec(gains, layer), _layer_spec(w_in, layer),
            base, base, *[_const_spec((t, LANES))] * 4, *[_const_spec((t, RET_W))] * 2,
        ],
        out_specs=[
            tok(ATT_Q_W), tok(4 * LANES), tok(ATT_KV_W), tok(RET_W), tok(RET_W), tok(RET_W), tok(RET_W),
            pl.BlockSpec((1, tail, 2 * ATT_KV_W), lambda b, i: (b, 0, 0)),
        ],
        out_shape=[
            out_bf(ATT_Q_W), out_bf(4 * LANES), out_bf(ATT_KV_W), out_bf(RET_W), out_bf(RET_W),
            out_bf(RET_W), out_bf(RET_W),
            jax.ShapeDtypeStruct((bsz, tail, 2 * ATT_KV_W), F32),
        ],
        compiler_params=_params("arbitrary", "arbitrary"),
        name="inproj",
    )(x, mods, mods, gains, w_in, base_cos, base_sin, *off, *decay)


def _lane_split(x):
    lo = lax.broadcasted_iota(jnp.int32, x.shape, 1) < HALF_LANES
    sw = pltpu.roll(x, HALF_LANES, 1)
    zero = jnp.zeros_like(x)
    return (jnp.where(lo, x, zero), jnp.where(lo, zero, sw), jnp.where(lo, sw, zero), jnp.where(lo, zero, x))


def _sink_softmax2(s, sink):
    m = jnp.maximum(jnp.max(s, axis=-1, keepdims=True), sink)
    e = jnp.exp2(s - m)
    den = jnp.sum(e, axis=-1, keepdims=True) + jnp.exp2(sink - m)
    return e / den


def _retention_head(qd, kg, v, state, causal, cdec):
    scores = jnp.where(causal, _dot_nt(qd, kg), 0.0).astype(BF16)
    ro = _dot(scores, v) + _dot(qd, state.astype(BF16))
    return ro, cdec * (state + _dot_tn(kg, v))


def _mix_prompt_kernel(cdec_ref, qa_ref, ka_ref, va_ref, qd_ref, kg_ref, vr_ref, sg_ref, bias_ref, sink_ref,
                       mixed_ref, state_ref, kx_ref, vt_ref, s_ref, *, t):
    i = pl.program_id(1)
    n_groups = t // GROUP
    hist = BAND - CHUNK

    hd = ATT_HEAD_DIM

    @pl.when(i == 0)
    def _():
        kx_ref[:, 0:hist, :] = jnp.zeros((4, hist, LANES), BF16)
        vt_ref[0, :, 0:hd, 0:GROUP] = jnp.zeros((ATT_KV_HEADS, hd, GROUP), BF16)
        vt_ref[:, :, hd:, :] = jnp.ones((n_groups, ATT_KV_HEADS, BF16_SUBLANES, 2 * GROUP), BF16)
        s_ref[...] = jnp.zeros(s_ref.shape, F32)

    @pl.when(i > 0)
    def _():
        kx_ref[:, 0:hist, :] = kx_ref[:, t:t + hist, :]
        vt_ref[0, :, 0:hd, 0:GROUP] = vt_ref[n_groups - 1, :, 0:hd, GROUP:]

    for n in range(4):
        kx_ref[n, hist:hist + t, :] = ka_ref[0, :, n * LANES:(n + 1) * LANES]
    for m in range(n_groups):
        tile = va_ref[0, m * GROUP:(m + 1) * GROUP, :].astype(F32).T.astype(BF16)
        for k in range(ATT_KV_HEADS):
            vt_ref[m, k, 0:hd, GROUP:] = tile[k * hd:(k + 1) * hd]
            if m + 1 < n_groups:
                vt_ref[m + 1, k, 0:hd, 0:GROUP] = tile[k * hd:(k + 1) * hd]

    no_keys = jnp.zeros((CHUNK, LANES), BF16)
    causal = (lax.broadcasted_iota(jnp.int32, (GROUP, GROUP), 0)
              >= lax.broadcasted_iota(jnp.int32, (GROUP, GROUP), 1))

    def group(g, carry):
        g0 = pl.multiple_of(g * GROUP, GROUP)
        grows = pl.ds(g0, GROUP)
        heads = [slice(h * LANES, (h + 1) * LANES) for h in range(RET_HEADS)]
        att = [(j, k, e) for j in range(GROUP // CHUNK) for k in range(ATT_KV_HEADS) for e in range(2)]

        def chunk_rows(j):
            return pl.ds(pl.multiple_of(g0 + j * CHUNK, CHUNK), CHUNK)

        raw = {}
        for j in range(GROUP // CHUNK):
            band = pl.ds(pl.multiple_of(g0 + j * CHUNK, CHUNK), BAND)
            for k in range(ATT_KV_HEADS):
                c0 = 2 * LANES * k
                q2 = jnp.concatenate([qa_ref[0, chunk_rows(j), c0:c0 + LANES],
                                      qa_ref[0, chunk_rows(j), c0 + LANES:c0 + 2 * LANES]], axis=0)
                keys = jnp.concatenate([kx_ref[2 * k, band, :], kx_ref[2 * k + 1, band, :]], axis=0)
                s2 = _dot_nt(keys, q2)
                raw[j, k, 0], raw[j, k, 1] = s2[:BAND], s2[BAND:]
        states = [s_ref[h] for h in range(RET_HEADS)]
        ret_q = [qd_ref[0, grows, c] for c in heads]
        ret_k = [kg_ref[0, grows, c] for c in heads]
        ret_v = [vr_ref[0, grows, c] for c in heads]
        ret_scores = [_dot_nt(ret_q[h], ret_k[h]) for h in range(RET_HEADS)]
        for h in range(RET_HEADS):
            s_ref[h] = cdec_ref[h] * (states[h] + _dot_tn(ret_k[h], ret_v[h]))

        probs, sink_terms = {}, {}
        for j, k, e in att:
            n = 2 * k + e
            variant = jnp.minimum(i * (t // CHUNK) + 2 * g + j, BAND_CHUNKS)
            s = raw[j, k, e] + bias_ref[variant, n]
            sink = sink_ref[n]
            m = jnp.maximum(jnp.max(s, axis=0, keepdims=True), sink)
            probs[j, k, e] = jnp.exp2(s - m).astype(BF16)
            sink_terms[j, k, e] = jnp.exp2(sink - m)
        ret_p = [jnp.where(causal, ret_scores[h], 0.0).astype(BF16) for h in range(RET_HEADS)]

        out_t = {}
        for j, k, e in att:
            frame = [probs[j, k, e], no_keys] if j == 0 else [no_keys, probs[j, k, e]]
            out_t[j, k, e] = _dot(vt_ref[g, k], jnp.concatenate(frame, axis=0))
        ret_out = [_dot(jnp.concatenate([ret_p[h], ret_q[h]], axis=1),
                        jnp.concatenate([ret_v[h], states[h].astype(BF16)], axis=0)) for h in range(RET_HEADS)]

        for j in range(GROUP // CHUNK):
            for k in range(ATT_KV_HEADS):
                c0 = 2 * LANES * k
                halves = []
                for e in range(2):
                    o_t = out_t[j, k, e]
                    den = o_t[ATT_HEAD_DIM:ATT_HEAD_DIM + 1] + sink_terms[j, k, e]
                    halves.append(o_t[:ATT_HEAD_DIM] * (1.0 / den))
                o = jnp.concatenate(halves, axis=0).T
                mixed_ref[0, chunk_rows(j), c0:c0 + LANES] = o[:CHUNK].astype(BF16)
                mixed_ref[0, chunk_rows(j), c0 + LANES:c0 + 2 * LANES] = o[CHUNK:].astype(BF16)
        for h in range(RET_HEADS):
            out = _rms(ret_out[h]) * sg_ref[0, grows, heads[h]].astype(F32)
            mixed_ref[0, grows, ATT_Q_W + h * LANES:ATT_Q_W + (h + 1) * LANES] = out.astype(BF16)
        return carry

    lax.fori_loop(0, n_groups, group, 0, unroll=4)

    @pl.when(i == pl.num_programs(1) - 1)
    def _():
        state_ref[0] = s_ref[...]


def _mix_prompt_call(qa, kx, va, qd, kg, vr, sg, bias4, sink4, cdec):
    bsz, seq, _ = qa.shape
    t = min(MIX_TOKEN_BLOCK, seq)
    hist = BAND - CHUNK
    assert hist == GROUP and t % GROUP == 0 and seq % t == 0
    tok = lambda w: pl.BlockSpec((1, t, w), lambda b, i: (b, i, 0))
    return pl.pallas_call(
        functools.partial(_mix_prompt_kernel, t=t),
        grid=(bsz, seq // t),
        in_specs=[
            pl.BlockSpec(memory_space=pltpu.SMEM),
            tok(ATT_Q_W), tok(4 * LANES), tok(ATT_KV_W), tok(RET_W), tok(RET_W), tok(RET_W), tok(RET_W),
            _const_spec(bias4.shape), _const_spec(sink4.shape),
        ],
        out_specs=[
            tok(D_MODEL),
            pl.BlockSpec((1, RET_HEADS, RET_KEY_DIM, RET_VAL_DIM), lambda b, i: (b, 0, 0, 0)),
        ],
        out_shape=[
            jax.ShapeDtypeStruct((bsz, seq, D_MODEL), BF16),
            jax.ShapeDtypeStruct((bsz, RET_HEADS, RET_KEY_DIM, RET_VAL_DIM), F32),
        ],
        scratch_shapes=[
            pltpu.VMEM((4, hist + t, LANES), BF16),
            pltpu.VMEM((t // GROUP, ATT_KV_HEADS, ATT_HEAD_DIM + BF16_SUBLANES, 2 * GROUP), BF16),
            pltpu.VMEM((RET_HEADS, RET_KEY_DIM, RET_VAL_DIM), F32),
        ],
        compiler_params=_params("arbitrary", "arbitrary"),
        name="mix_prompt",
    )(cdec, qa, kx, va, qd, kg, vr, sg, bias4, sink4)


def _mix_sample_kernel(cdec_ref, qa_ref, kvn_ref, ck_ref, cv_ref, qd_ref, kg_ref, vr_ref, sg_ref, s0_ref,
                       bias_ref, sink_ref, *rest, n_new, n_cache):
    mixed_ref, *stack_refs = rest[-4:]
    done = 0
    if len(rest) > 4:
        done = rest[0].shape[0]
        for prev_ref, out_ref in zip(rest[:3], stack_refs):
            out_ref[0:done] = prev_ref[...]
    state_ref, wk_ref, wv_ref = (r.at[done] for r in stack_refs)
    pad = jnp.zeros((SAMPLE_KEYS - n_cache - n_new, LANES), F32)
    causal = (lax.broadcasted_iota(jnp.int32, (n_new, n_new), 0)
              >= lax.broadcasted_iota(jnp.int32, (n_new, n_new), 1))
    for g in range(SAMPLE_GROUP):
        kvn = kvn_ref[g]
        k_all = jnp.concatenate([ck_ref[g], kvn[:, :ATT_KV_W], pad], axis=0)
        v_all = jnp.concatenate([cv_ref[g], kvn[:, ATT_KV_W:], pad], axis=0)
        wk_ref[g] = k_all[n_new:n_new + n_cache]
        wv_ref[g] = v_all[n_new:n_new + n_cache]
        k4 = [p.astype(BF16) for p in _lane_split(k_all)]
        v4 = [p.astype(BF16) for p in _lane_split(v_all)]
        for k in range(ATT_KV_HEADS):
            c0 = 2 * LANES * k
            q2 = jnp.concatenate([qa_ref[g, :, c0:c0 + LANES], qa_ref[g, :, c0 + LANES:c0 + 2 * LANES]], axis=0)
            o = jnp.zeros((2 * n_new, LANES), F32)
            for e in range(2):
                n = 2 * k + e
                s = _dot_nt(q2, k4[n]) + bias_ref[n]
                p = _sink_softmax2(s, sink_ref[n])
                o = o + _dot(p.astype(BF16), v4[n])
            mixed_ref[g, :, c0:c0 + LANES] = o[:n_new].astype(BF16)
            mixed_ref[g, :, c0 + LANES:c0 + 2 * LANES] = o[n_new:].astype(BF16)

        for h in range(RET_HEADS):
            cols = slice(h * LANES, (h + 1) * LANES)
            ro, state_ref[g, h] = _retention_head(qd_ref[g, :, cols], kg_ref[g, :, cols], vr_ref[g, :, cols],
                                                  s0_ref[g, h], causal, cdec_ref[h])
            out = _rms(ro) * sg_ref[g, :, cols].astype(F32)
            mixed_ref[g, :, ATT_Q_W + h * LANES:ATT_Q_W + (h + 1) * LANES] = out.astype(BF16)


def _mix_sample_call(qa, kvn, cache_k, cache_v, qd, kg, vr, sg, state, layer, stacks, bias4, sink4, cdec):
    dbsz, n_new, _ = qa.shape
    n_cache = cache_k.shape[2]
    grp = lambda *tail: pl.BlockSpec((SAMPLE_GROUP,) + tail, lambda b: (b,) + (0,) * len(tail))
    lay = lambda *tail: pl.BlockSpec((None, SAMPLE_GROUP) + tail, lambda b: (layer, b) + (0,) * len(tail))
    stk = lambda n, *tail: pl.BlockSpec((n, SAMPLE_GROUP) + tail, lambda b: (0, b) + (0,) * len(tail))
    stack_tails = [(RET_HEADS, RET_KEY_DIM, RET_VAL_DIM), (n_cache, ATT_KV_W), (n_cache, ATT_KV_W)]
    operands = [cdec, qa, kvn, cache_k, cache_v, qd, kg, vr, sg, state, bias4, sink4]
    prev_specs = []
    if stacks is not None:
        assert all(s.shape[0] == layer for s in stacks)
        operands += list(stacks)
        prev_specs = [stk(layer, *tail) for tail in stack_tails]
    outs = pl.pallas_call(
        functools.partial(_mix_sample_kernel, n_new=n_new, n_cache=n_cache),
        grid=(dbsz // SAMPLE_GROUP,),
        in_specs=[
            pl.BlockSpec(memory_space=pltpu.SMEM),
            grp(n_new, ATT_Q_W), grp(n_new, 2 * ATT_KV_W), lay(n_cache, ATT_KV_W), lay(n_cache, ATT_KV_W),
            grp(n_new, RET_W), grp(n_new, RET_W), grp(n_new, RET_W), grp(n_new, RET_W),
            lay(RET_HEADS, RET_KEY_DIM, RET_VAL_DIM),
            _const_spec(bias4.shape), _const_spec(sink4.shape),
        ] + prev_specs,
        out_specs=[grp(n_new, D_MODEL)] + [stk(layer + 1, *tail) for tail in stack_tails],
        out_shape=[jax.ShapeDtypeStruct((dbsz, n_new, D_MODEL), BF16)]
        + [jax.ShapeDtypeStruct((layer + 1, dbsz) + tail, F32) for tail in stack_tails],
        compiler_params=_params("arbitrary"),
        name="mix_sample",
    )(*operands)
    return outs[0], outs[1:]


def _dense_kernel(x_ref, mixed_ref, ga1_ref, sh2_ref, sc2_ref, ga2_ref, g_ref, gf_ref,
                  wo_ref, wu_ref, wd_ref, o_ref, *, final):
    x1 = x_ref[0] + ga1_ref[...] * _dot(mixed_ref[0], wo_ref[...])
    h = ((_rms(x1) * g_ref[...]) * (1.0 + sc2_ref[...]) + sh2_ref[...]).astype(BF16)
    acc = jnp.zeros(x1.shape, F32)
    for j in range(D_FF // FF_BLOCK):
        u = _dot(h, wu_ref[:, j * FF_BLOCK:(j + 1) * FF_BLOCK])
        u = jnp.square(jnp.maximum(u, 0.0)).astype(BF16)
        acc = acc + _dot(u, wd_ref[j * FF_BLOCK:(j + 1) * FF_BLOCK, :])
    x2 = x1 + ga2_ref[...] * acc
    o_ref[0] = _rms(x2) * gf_ref[...] if final else x2


def _dense_call(x, mixed, mods, gains, g_final, w_out, w_up, w_down, layer, final):
    bsz, seq, d = x.shape
    t = min(TOKEN_BLOCK, seq)
    tok = lambda w: pl.BlockSpec((1, t, w), lambda b, i: (b, i, 0))
    return pl.pallas_call(
        functools.partial(_dense_kernel, final=final),
        grid=(bsz, seq // t),
        in_specs=[
            tok(d), tok(d), *_mod_specs(mods, layer, (MOD_GATE1, MOD_SHIFT2, MOD_SCALE2, MOD_GATE2), t),
            _layer_spec(gains, layer), _const_spec((1, d)),
            _layer_spec(w_out, layer), _layer_spec(w_up, layer), _layer_spec(w_down, layer),
        ],
        out_specs=tok(d),
        out_shape=jax.ShapeDtypeStruct((bsz, seq, d), F32),
        compiler_params=_params("arbitrary", "arbitrary"),
        name="dense",
    )(x, mixed, mods, mods, mods, mods, gains, g_final.reshape(1, d), w_out, w_up, w_down)


def _rotary_tables(base_pos, off_pos):
    half = RET_KEY_DIM // 2
    inv = 1.0 / (ROPE_BASE ** (jnp.arange(half, dtype=F32) / half))
    sign = jnp.concatenate([-jnp.ones((half,), F32), jnp.ones((half,), F32)])

    def both(pos):
        ang = pos.astype(F32)[:, None] * inv[None, :]
        cos, sin = jnp.cos(ang), jnp.sin(ang)
        return jnp.concatenate([cos, cos], axis=1), jnp.concatenate([sin, sin], axis=1)

    base_cos, base_sin = both(base_pos)
    off_cos, off_sin = both(off_pos)
    return base_cos[:, None, :], base_sin[:, None, :], (off_cos, off_sin, sign * off_cos, sign * off_sin)


def _retention_tables(n, rows):
    log_g = jnp.log(1.0 - 2.0 ** (-5.0 - jnp.arange(RET_HEADS, dtype=F32)))
    steps = (jnp.arange(rows) % n).astype(F32) + 1.0
    expo = steps[:, None] * jnp.repeat(log_g, RET_KEY_DIM)[None, :]
    return (jnp.exp(expo), jnp.exp(-expo) * (RET_KEY_DIM ** -0.5)), jnp.exp(n * log_g)


def kernel(x_prompt, x_sample, c_prompt, c_sample, cache_win_k, cache_win_v, state_ret, g_mix, g_mlp, w_ada,
           b_ada, w_in, w_out, att_sinks, rel_bias, w_up, w_down, g_final):
    depth = w_in.shape[0]
    bsz, seq, d = x_prompt.shape
    dbsz, dec_seq, _ = x_sample.shape
    n_cache = cache_win_k.shape[2]
    n_rows = dbsz * dec_seq
    keep_p = min(WINDOW, seq)

    w_in_b, w_out_b, w_up_b, w_down_b = (w.astype(BF16) for w in (w_in, w_out, w_up, w_down))
    g_mix3 = g_mix.reshape(depth, 1, d)
    g_mlp3 = g_mlp.reshape(depth, 1, d)
    cache_k = cache_win_k.reshape(depth, dbsz, n_cache, ATT_KV_W)
    cache_v = cache_win_v.reshape(depth, dbsz, n_cache, ATT_KV_W)

    c_all = jnp.concatenate([c_prompt, c_sample], axis=0)
    c_all = jnp.pad(c_all, ((0, -c_all.shape[0] % 8), (0, 0)))
    mods = _ada_call(c_all, w_ada, b_ada)
    mods_p = mods[:, :bsz, None, :]
    mods_s = jnp.repeat(mods[:, bsz:bsz + dbsz], dec_seq, axis=1)

    t_p = min(TOKEN_BLOCK, seq)
    rot_p = _rotary_tables(jnp.arange(seq // t_p) * t_p, jnp.arange(t_p))
    rot_s = _rotary_tables(jnp.full((1,), PAST_LEN), jnp.tile(jnp.arange(dec_seq), dbsz))
    decay_p, cdec_p = _retention_tables(GROUP, t_p)
    decay_s, cdec_s = _retention_tables(dec_seq, n_rows)

    rel_p = (jnp.arange(BAND) - BAND_CHUNKS * CHUNK)[None, :] - jnp.arange(CHUNK)[:, None]
    bias_p = _banded_bias(_bias_call(rel_bias, _rel_bucket(rel_p)) * LOG2_E)
    kpos = PAST_LEN - n_cache + jnp.arange(n_cache + dec_seq)
    qpos = PAST_LEN + jnp.arange(dec_seq)
    bias_s = _bias_call(rel_bias, _rel_bucket(kpos[None, :] - qpos[:, None])) * LOG2_E
    bias_s = jnp.pad(bias_s, ((0, 0), (0, 0), (0, SAMPLE_KEYS - n_cache - dec_seq)), constant_values=-jnp.inf)
    bias_s = _pair_rows(bias_s)

    xp = x_prompt
    xs = x_sample.reshape(1, n_rows, d)
    pk, pv, pr = [], [], []
    stacks = None
    for l in range(depth):
        sinks = att_sinks[l].astype(F32)[:, None, None] * LOG2_E
        sink_p = _pair_lanes(jnp.broadcast_to(sinks, (ATT_HEADS, 1, CHUNK)))
        sink_s = _pair_rows(jnp.broadcast_to(sinks, (ATT_HEADS, dec_seq, 1)))
        final = l == depth - 1

        qa, kx, va, qd, kg, vr, sg, kvt = _inproj_call(xp, mods_p, g_mix3, w_in_b, l, rot_p, decay_p, keep_p)
        mixed, st = _mix_prompt_call(qa, kx, va, qd, kg, vr, sg, bias_p, sink_p, cdec_p)
        xp = _dense_call(xp, mixed, mods_p, g_mlp3, g_final, w_out_b, w_up_b, w_down_b, l, final)
        pk.append(kvt[:, :, :ATT_KV_W].reshape(bsz, keep_p, ATT_KV_HEADS, ATT_HEAD_DIM))
        pv.append(kvt[:, :, ATT_KV_W:].reshape(bsz, keep_p, ATT_KV_HEADS, ATT_HEAD_DIM))
        pr.append(st)

        qa, _, _, qd, kg, vr, sg, kvn = _inproj_call(xs, mods_s, g_mix3, w_in_b, l, rot_s, decay_s, n_rows)
        per_seq = lambda a: a.reshape(dbsz, dec_seq, a.shape[-1])
        mixed, stacks = _mix_sample_call(per_seq(qa), per_seq(kvn), cache_k, cache_v, per_seq(qd), per_seq(kg),
                                         per_seq(vr), per_seq(sg), state_ret, l, stacks, bias_s, sink_s, cdec_s)
        xs = _dense_call(xs, mixed.reshape(1, n_rows, d), mods_s, g_mlp3, g_final, w_out_b, w_up_b, w_down_b, l,
                         final)

    sr, sk, sv = stacks
    win_shape = (depth, dbsz, n_cache, ATT_KV_HEADS, ATT_HEAD_DIM)
    return (xp, xs.reshape(dbsz, dec_seq, d), jnp.stack(pk), jnp.stack(pv), jnp.stack(pr),
            sk.reshape(win_shape), sv.reshape(win_shape), sr)
```

```python
import functools
import math

import jax
import jax.numpy as jnp
from jax import lax
from jax.experimental import pallas as pl
from jax.experimental.pallas import tpu as pltpu

F32 = jnp.float32
BF16 = jnp.bfloat16

D_MODEL = 1024
CHUNK = 64
ATT_HEADS = 8
ATT_KV_HEADS = 2
ATT_HEAD_DIM = 64
ATT_Q_W = ATT_HEADS * ATT_HEAD_DIM
ATT_KV_W = ATT_KV_HEADS * ATT_HEAD_DIM
ATT_SCALE = ATT_HEAD_DIM ** -0.5
WINDOW = 128
BAND_CHUNKS = -(-WINDOW // CHUNK)
BAND = (BAND_CHUNKS + 1) * CHUNK
REL_BUCKETS = 32
REL_MAX_DIST = 128
RET_HEADS = 4
RET_KEY_DIM = 128
RET_VAL_DIM = 128
RET_W = RET_HEADS * RET_KEY_DIM
ROPE_BASE = 10000.0
IN_W = ATT_Q_W + 2 * ATT_KV_W + 4 * RET_W
D_FF = 4 * D_MODEL
N_MOD = 6
EPS = 1e-6
PAST_LEN = 4096
LOG2_E = math.log2(math.e)

LANES = 128
HALF_LANES = LANES // 2
BF16_SUBLANES = 16
VMEM_LIMIT_BYTES = 56 * 1024 * 1024

OFF_QA = 0
OFF_KV = OFF_QA + ATT_Q_W
OFF_QR = OFF_KV + 2 * ATT_KV_W
OFF_KR = OFF_QR + RET_W
OFF_VR = OFF_KR + RET_W
OFF_GR = OFF_VR + RET_W

TOKEN_BLOCK = 1024
INPROJ_SUB_BLOCK = 256
MIX_TOKEN_BLOCK = 1024
GROUP = 128
FF_BLOCK = 1024
ADA_COL_BLOCK = 1024
SAMPLE_GROUP = 4
SAMPLE_KEYS = 2 * LANES


def _params(*sem):
    return pltpu.CompilerParams(dimension_semantics=sem, vmem_limit_bytes=VMEM_LIMIT_BYTES)


def _const_spec(shape):
    zeros = (0,) * len(shape)
    return pl.BlockSpec(shape, lambda *_: zeros, pipeline_mode=pl.Buffered(1))


def _rms(x):
    return x * lax.rsqrt(jnp.mean(x * x, axis=-1, keepdims=True) + EPS)


def _dot(a, b):
    return jnp.dot(a, b, preferred_element_type=F32)


def _dot_nt(a, b):
    return lax.dot_general(a, b, (((1,), (1,)), ((), ())), preferred_element_type=F32)


def _dot_tn(a, b):
    return lax.dot_general(a, b, (((0,), (0,)), ((), ())), preferred_element_type=F32)


def _ada_kernel(c_ref, w_ref, b_ref, o_ref):
    a = jax.nn.silu(c_ref[...]).astype(BF16)
    o_ref[0] = _dot(a, w_ref[0].astype(BF16)) + b_ref[0]


def _ada_call(c_all, w_ada, b_ada):
    depth, d, n = w_ada.shape
    rows = c_all.shape[0]
    return pl.pallas_call(
        _ada_kernel,
        grid=(depth, n // ADA_COL_BLOCK),
        in_specs=[
            pl.BlockSpec((rows, d), lambda l, j: (0, 0)),
            pl.BlockSpec((1, d, ADA_COL_BLOCK), lambda l, j: (l, 0, j)),
            pl.BlockSpec((1, 1, ADA_COL_BLOCK), lambda l, j: (l, 0, j)),
        ],
        out_specs=pl.BlockSpec((1, rows, ADA_COL_BLOCK), lambda l, j: (l, 0, j)),
        out_shape=jax.ShapeDtypeStruct((depth, rows, n), F32),
        compiler_params=_params("arbitrary", "arbitrary"),
        name="ada",
    )(c_all, w_ada, b_ada.reshape(depth, 1, n))


def _bias_kernel(tab_ref, idx_ref, o_ref):
    h = pl.program_id(0)
    idx = idx_ref[...]
    acc = jnp.zeros(idx.shape, F32)
    for b in range(REL_BUCKETS):
        acc = jnp.where(idx == b, tab_ref[b, h], acc)
    o_ref[0] = acc


def _bias_call(rel_bias, bucket):
    q, j = bucket.shape
    return pl.pallas_call(
        _bias_kernel,
        grid=(ATT_HEADS,),
        in_specs=[
            pl.BlockSpec(memory_space=pltpu.SMEM),
            pl.BlockSpec((q, j), lambda h: (0, 0)),
        ],
        out_specs=pl.BlockSpec((1, q, j), lambda h: (h, 0, 0)),
        out_shape=jax.ShapeDtypeStruct((ATT_HEADS, q, j), F32),
        compiler_params=_params("arbitrary"),
        name="rel_bias",
    )(rel_bias, bucket)


def _rel_bucket(rel):
    nb = REL_BUCKETS // 2
    n = -rel
    ret = jnp.where(n < 0, nb, 0)
    n = jnp.abs(n)
    max_exact = nb // 2
    nf = jnp.maximum(n, 1).astype(F32)
    large = max_exact + (jnp.log(nf / max_exact) / math.log(REL_MAX_DIST / max_exact)
                         * (nb - max_exact)).astype(jnp.int32)
    large = jnp.minimum(large, nb - 1)
    return (ret + jnp.where(n < max_exact, n, large)).astype(jnp.int32)


def _pair_rows(per_head):
    out = []
    for k in range(ATT_KV_HEADS):
        for e in range(2):
            out.append(jnp.concatenate([per_head[4 * k + e], per_head[4 * k + 2 + e]], axis=0))
    return jnp.stack(out)


def _pair_lanes(per_head):
    out = []
    for k in range(ATT_KV_HEADS):
        for e in range(2):
            out.append(jnp.concatenate([per_head[4 * k + e], per_head[4 * k + 2 + e]], axis=-1))
    return jnp.stack(out)


def _banded_bias(per_head):
    base = _pair_lanes(jnp.swapaxes(per_head, 1, 2))
    key = jnp.arange(BAND)[None, :, None]
    return jnp.stack([jnp.where(key < (BAND_CHUNKS - v) * CHUNK, -jnp.inf, base)
                      for v in range(BAND_CHUNKS + 1)])


def _rotary(u, cos2, sin2):
    parts = []
    for h in range(RET_HEADS):
        x = u[:, h * RET_KEY_DIM:(h + 1) * RET_KEY_DIM]
        parts.append(x * cos2 + pltpu.roll(x, HALF_LANES, 1) * sin2)
    return jnp.concatenate(parts, axis=1)


def _inproj_kernel(x_ref, sh_ref, sc_ref, g_ref, w_ref, ac_ref, as_ref, bc_ref, bs_ref, bcn_ref, bsn_ref,
                   qdec_ref, kdec_ref,
                   qa_ref, kx_ref, va_ref, qd_ref, kg_ref, vr_ref, sg_ref, kvt_ref, *, tail):
    i = pl.program_id(1)
    t = x_ref.shape[1]
    sub = min(INPROJ_SUB_BLOCK, t)
    for r0 in range(0, t, sub):
        rows = slice(r0, r0 + sub)
        per_row = lambda ref: ref[rows] if ref.shape[0] == t else ref[...]
        x = x_ref[0, rows]
        h = (_rms(x) * g_ref[...]) * (1.0 + per_row(sc_ref)) + per_row(sh_ref)
        hb = h.astype(BF16)
        cos2 = ac_ref[...] * bc_ref[rows] - as_ref[...] * bs_ref[rows]
        sin2 = as_ref[...] * bcn_ref[rows] + ac_ref[...] * bsn_ref[rows]

        qa_ref[0, rows] = (_dot(hb, w_ref[:, OFF_QA:OFF_KV]) * (ATT_SCALE * LOG2_E)).astype(BF16)
        kv = _dot(hb, w_ref[:, OFF_KV:OFF_QR])
        kx_ref[0, rows] = jnp.concatenate(_lane_split(kv[:, :ATT_KV_W]), axis=1).astype(BF16)
        va_ref[0, rows] = kv[:, ATT_KV_W:].astype(BF16)
        qd_ref[0, rows] = (_rotary(_dot(hb, w_ref[:, OFF_QR:OFF_KR]), cos2, sin2) * qdec_ref[rows]).astype(BF16)
        kg_ref[0, rows] = (_rotary(_dot(hb, w_ref[:, OFF_KR:OFF_VR]), cos2, sin2) * kdec_ref[rows]).astype(BF16)
        vr_ref[0, rows] = _dot(hb, w_ref[:, OFF_VR:OFF_GR]).astype(BF16)
        sg_ref[0, rows] = jax.nn.silu(_dot(hb, w_ref[:, OFF_GR:IN_W])).astype(BF16)

        lo = max(r0, t - tail)
        if lo < r0 + sub:
            @pl.when(i == pl.num_programs(1) - 1)
            def _(kv=kv, lo=lo, r0=r0):
                kvt_ref[0, lo - (t - tail):r0 + sub - (t - tail)] = kv[lo - r0:]


MOD_SHIFT1, MOD_SCALE1, MOD_GATE1, MOD_SHIFT2, MOD_SCALE2, MOD_GATE2 = range(N_MOD)


def _mod_specs(mods, layer, which, t):
    if mods.ndim == 4:
        return [pl.BlockSpec((None, None, 1, D_MODEL), lambda b, i, m=m: (layer, b, 0, m)) for m in which]
    return [pl.BlockSpec((None, t, D_MODEL), lambda b, i, m=m: (layer, i, m)) for m in which]


def _layer_spec(arr, layer):
    tail_zeros = (0,) * (arr.ndim - 1)
    return pl.BlockSpec((None,) + arr.shape[1:], lambda *_: (layer,) + tail_zeros, pipeline_mode=pl.Buffered(1))


def _inproj_call(x, mods, gains, w_in, layer, rot, decay, tail):
    bsz, seq, d = x.shape
    t = min(TOKEN_BLOCK, seq)
    nb = seq // t
    base_cos, base_sin, off = rot
    assert base_cos.shape == (nb, 1, LANES) and off[0].shape == (t, LANES)
    assert all(a.shape == (t, RET_W) for a in decay)
    tok = lambda w: pl.BlockSpec((1, t, w), lambda b, i: (b, i, 0))
    out_bf = lambda w: jax.ShapeDtypeStruct((bsz, seq, w), BF16)
    base = pl.BlockSpec((None, 1, LANES), lambda b, i: (i, 0, 0))
    return pl.pallas_call(
        functools.partial(_inproj_kernel, tail=tail),
        grid=(bsz, nb),
        in_specs=[
            tok(d), *_mod_specs(mods, layer, (MOD_SHIFT1, MOD_SCALE1), t),
            _layer_spec(gains, layer), _layer_spec(w_in, layer),
            base, base, *[_const_spec((t, LANES))] * 4, *[_const_spec((t, RET_W))] * 2,
        ],
        out_specs=[
            tok(ATT_Q_W), tok(4 * LANES), tok(ATT_KV_W), tok(RET_W), tok(RET_W), tok(RET_W), tok(RET_W),
            pl.BlockSpec((1, tail, 2 * ATT_KV_W), lambda b, i: (b, 0, 0)),
        ],
        out_shape=[
            out_bf(ATT_Q_W), out_bf(4 * LANES), out_bf(ATT_KV_W), out_bf(RET_W), out_bf(RET_W),
            out_bf(RET_W), out_bf(RET_W),
            jax.ShapeDtypeStruct((bsz, tail, 2 * ATT_KV_W), F32),
        ],
        compiler_params=_params("arbitrary", "arbitrary"),
        name="inproj",
    )(x, mods, mods, gains, w_in, base_cos, base_sin, *off, *decay)


def _lane_split(x):
    lo = lax.broadcasted_iota(jnp.int32, x.shape, 1) < HALF_LANES
    sw = pltpu.roll(x, HALF_LANES, 1)
    zero = jnp.zeros_like(x)
    return (jnp.where(lo, x, zero), jnp.where(lo, zero, sw), jnp.where(lo, sw, zero), jnp.where(lo, zero, x))


def _sink_softmax2(s, sink):
    m = jnp.maximum(jnp.max(s, axis=-1, keepdims=True), sink)
    e = jnp.exp2(s - m)
    den = jnp.sum(e, axis=-1, keepdims=True) + jnp.exp2(sink - m)
    return e / den


def _retention_head(qd, kg, v, state, causal, cdec):
    scores = jnp.where(causal, _dot_nt(qd, kg), 0.0).astype(BF16)
    ro = _dot(scores, v) + _dot(qd, state.astype(BF16))
    return ro, cdec * (state + _dot_tn(kg, v))


def _mix_prompt_kernel(cdec_ref, qa_ref, ka_ref, va_ref, qd_ref, kg_ref, vr_ref, sg_ref, bias_ref, sink_ref,
                       mixed_ref, state_ref, kx_ref, vt_ref, s_ref, *, t):
    i = pl.program_id(1)
    n_groups = t // GROUP
    hist = BAND - CHUNK

    hd = ATT_HEAD_DIM

    @pl.when(i == 0)
    def _():
        kx_ref[:, 0:hist, :] = jnp.zeros((4, hist, LANES), BF16)
        vt_ref[0, :, 0:hd, 0:GROUP] = jnp.zeros((ATT_KV_HEADS, hd, GROUP), BF16)
        vt_ref[:, :, hd:, :] = jnp.ones((n_groups, ATT_KV_HEADS, BF16_SUBLANES, 2 * GROUP), BF16)
        s_ref[...] = jnp.zeros(s_ref.shape, F32)

    @pl.when(i > 0)
    def _():
        kx_ref[:, 0:hist, :] = kx_ref[:, t:t + hist, :]
        vt_ref[0, :, 0:hd, 0:GROUP] = vt_ref[n_groups - 1, :, 0:hd, GROUP:]

    for n in range(4):
        kx_ref[n, hist:hist + t, :] = ka_ref[0, :, n * LANES:(n + 1) * LANES]
    for m in range(n_groups):
        tile = va_ref[0, m * GROUP:(m + 1) * GROUP, :].astype(F32).T.astype(BF16)
        for k in range(ATT_KV_HEADS):
            vt_ref[m, k, 0:hd, GROUP:] = tile[k * hd:(k + 1) * hd]
            if m + 1 < n_groups:
                vt_ref[m + 1, k, 0:hd, 0:GROUP] = tile[k * hd:(k + 1) * hd]

    no_keys = jnp.zeros((CHUNK, LANES), BF16)
    causal = (lax.broadcasted_iota(jnp.int32, (GROUP, GROUP), 0)
              >= lax.broadcasted_iota(jnp.int32, (GROUP, GROUP), 1))

    def group(g, carry):
        g0 = pl.multiple_of(g * GROUP, GROUP)
        grows = pl.ds(g0, GROUP)
        heads = [slice(h * LANES, (h + 1) * LANES) for h in range(RET_HEADS)]
        att = [(j, k, e) for j in range(GROUP // CHUNK) for k in range(ATT_KV_HEADS) for e in range(2)]

        def chunk_rows(j):
            return pl.ds(pl.multiple_of(g0 + j * CHUNK, CHUNK), CHUNK)

        raw = {}
        for j in range(GROUP // CHUNK):
            band = pl.ds(pl.multiple_of(g0 + j * CHUNK, CHUNK), BAND)
            for k in range(ATT_KV_HEADS):
                c0 = 2 * LANES * k
                q2 = jnp.concatenate([qa_ref[0, chunk_rows(j), c0:c0 + LANES],
                                      qa_ref[0, chunk_rows(j), c0 + LANES:c0 + 2 * LANES]], axis=0)
                keys = jnp.concatenate([kx_ref[2 * k, band, :], kx_ref[2 * k + 1, band, :]], axis=0)
                s2 = _dot_nt(keys, q2)
                raw[j, k, 0], raw[j, k, 1] = s2[:BAND], s2[BAND:]
        states = [s_ref[h] for h in range(RET_HEADS)]
        ret_q = [qd_ref[0, grows, c] for c in heads]
        ret_k = [kg_ref[0, grows, c] for c in heads]
        ret_v = [vr_ref[0, grows, c] for c in heads]
        ret_scores = [_dot_nt(ret_q[h], ret_k[h]) for h in range(RET_HEADS)]
        for h in range(RET_HEADS):
            s_ref[h] = cdec_ref[h] * (states[h] + _dot_tn(ret_k[h], ret_v[h]))

        probs, sink_terms = {}, {}
        for j, k, e in att:
            n = 2 * k + e
            variant = jnp.minimum(i * (t // CHUNK) + 2 * g + j, BAND_CHUNKS)
            s = raw[j, k, e] + bias_ref[variant, n]
            sink = sink_ref[n]
            m = jnp.maximum(jnp.max(s, axis=0, keepdims=True), sink)
            probs[j, k, e] = jnp.exp2(s - m).astype(BF16)
            sink_terms[j, k, e] = jnp.exp2(sink - m)
        ret_p = [jnp.where(causal, ret_scores[h], 0.0).astype(BF16) for h in range(RET_HEADS)]

        out_t = {}
        for j, k, e in att:
            frame = [probs[j, k, e], no_keys] if j == 0 else [no_keys, probs[j, k, e]]
            out_t[j, k, e] = _dot(vt_ref[g, k], jnp.concatenate(frame, axis=0))
        ret_out = [_dot(jnp.concatenate([ret_p[h], ret_q[h]], axis=1),
                        jnp.concatenate([ret_v[h], states[h].astype(BF16)], axis=0)) for h in range(RET_HEADS)]

        for j in range(GROUP // CHUNK):
            for k in range(ATT_KV_HEADS):
                c0 = 2 * LANES * k
                halves = []
                for e in range(2):
                    o_t = out_t[j, k, e]
                    den = o_t[ATT_HEAD_DIM:ATT_HEAD_DIM + 1] + sink_terms[j, k, e]
                    halves.append(o_t[:ATT_HEAD_DIM] * (1.0 / den))
                o = jnp.concatenate(halves, axis=0).T
                mixed_ref[0, chunk_rows(j), c0:c0 + LANES] = o[:CHUNK].astype(BF16)
                mixed_ref[0, chunk_rows(j), c0 + LANES:c0 + 2 * LANES] = o[CHUNK:].astype(BF16)
        for h in range(RET_HEADS):
            out = _rms(ret_out[h]) * sg_ref[0, grows, heads[h]].astype(F32)
            mixed_ref[0, grows, ATT_Q_W + h * LANES:ATT_Q_W + (h + 1) * LANES] = out.astype(BF16)
        return carry

    lax.fori_loop(0, n_groups, group, 0, unroll=True)

    @pl.when(i == pl.num_programs(1) - 1)
    def _():
        state_ref[0] = s_ref[...]


def _mix_prompt_call(qa, kx, va, qd, kg, vr, sg, bias4, sink4, cdec):
    bsz, seq, _ = qa.shape
    t = min(MIX_TOKEN_BLOCK, seq)
    hist = BAND - CHUNK
    assert hist == GROUP and t % GROUP == 0 and seq % t == 0
    tok = lambda w: pl.BlockSpec((1, t, w), lambda b, i: (b, i, 0))
    return pl.pallas_call(
        functools.partial(_mix_prompt_kernel, t=t),
        grid=(bsz, seq // t),
        in_specs=[
            pl.BlockSpec(memory_space=pltpu.SMEM),
            tok(ATT_Q_W), tok(4 * LANES), tok(ATT_KV_W), tok(RET_W), tok(RET_W), tok(RET_W), tok(RET_W),
            _const_spec(bias4.shape), _const_spec(sink4.shape),
        ],
        out_specs=[
            tok(D_MODEL),
            pl.BlockSpec((1, RET_HEADS, RET_KEY_DIM, RET_VAL_DIM), lambda b, i: (b, 0, 0, 0)),
        ],
        out_shape=[
            jax.ShapeDtypeStruct((bsz, seq, D_MODEL), BF16),
            jax.ShapeDtypeStruct((bsz, RET_HEADS, RET_KEY_DIM, RET_VAL_DIM), F32),
        ],
        scratch_shapes=[
            pltpu.VMEM((4, hist + t, LANES), BF16),
            pltpu.VMEM((t // GROUP, ATT_KV_HEADS, ATT_HEAD_DIM + BF16_SUBLANES, 2 * GROUP), BF16),
            pltpu.VMEM((RET_HEADS, RET_KEY_DIM, RET_VAL_DIM), F32),
        ],
        compiler_params=_params("arbitrary", "arbitrary"),
        name="mix_prompt",
    )(cdec, qa, kx, va, qd, kg, vr, sg, bias4, sink4)


def _mix_sample_kernel(cdec_ref, qa_ref, kvn_ref, ck_ref, cv_ref, qd_ref, kg_ref, vr_ref, sg_ref, s0_ref,
                       bias_ref, sink_ref, *rest, n_new, n_cache):
    mixed_ref, *stack_refs = rest[-4:]
    done = 0
    if len(rest) > 4:
        done = rest[0].shape[0]
        for prev_ref, out_ref in zip(rest[:3], stack_refs):
            out_ref[0:done] = prev_ref[...]
    state_ref, wk_ref, wv_ref = (r.at[done] for r in stack_refs)
    pad = jnp.zeros((SAMPLE_KEYS - n_cache - n_new, LANES), F32)
    causal = (lax.broadcasted_iota(jnp.int32, (n_new, n_new), 0)
              >= lax.broadcasted_iota(jnp.int32, (n_new, n_new), 1))
    for g in range(SAMPLE_GROUP):
        kvn = kvn_ref[g]
        k_all = jnp.concatenate([ck_ref[g], kvn[:, :ATT_KV_W], pad], axis=0)
        v_all = jnp.concatenate([cv_ref[g], kvn[:, ATT_KV_W:], pad], axis=0)
        wk_ref[g] = k_all[n_new:n_new + n_cache]
        wv_ref[g] = v_all[n_new:n_new + n_cache]
        k4 = [p.astype(BF16) for p in _lane_split(k_all)]
        v4 = [p.astype(BF16) for p in _lane_split(v_all)]
        for k in range(ATT_KV_HEADS):
            c0 = 2 * LANES * k
            q2 = jnp.concatenate([qa_ref[g, :, c0:c0 + LANES], qa_ref[g, :, c0 + LANES:c0 + 2 * LANES]], axis=0)
            o = jnp.zeros((2 * n_new, LANES), F32)
            for e in range(2):
                n = 2 * k + e
                s = _dot_nt(q2, k4[n]) + bias_ref[n]
                p = _sink_softmax2(s, sink_ref[n])
                o = o + _dot(p.astype(BF16), v4[n])
            mixed_ref[g, :, c0:c0 + LANES] = o[:n_new].astype(BF16)
            mixed_ref[g, :, c0 + LANES:c0 + 2 * LANES] = o[n_new:].astype(BF16)

        for h in range(RET_HEADS):
            cols = slice(h * LANES, (h + 1) * LANES)
            ro, state_ref[g, h] = _retention_head(qd_ref[g, :, cols], kg_ref[g, :, cols], vr_ref[g, :, cols],
                                                  s0_ref[g, h], causal, cdec_ref[h])
            out = _rms(ro) * sg_ref[g, :, cols].astype(F32)
            mixed_ref[g, :, ATT_Q_W + h * LANES:ATT_Q_W + (h + 1) * LANES] = out.astype(BF16)


def _mix_sample_call(qa, kvn, cache_k, cache_v, qd, kg, vr, sg, state, layer, stacks, bias4, sink4, cdec):
    dbsz, n_new, _ = qa.shape
    n_cache = cache_k.shape[2]
    grp = lambda *tail: pl.BlockSpec((SAMPLE_GROUP,) + tail, lambda b: (b,) + (0,) * len(tail))
    lay = lambda *tail: pl.BlockSpec((None, SAMPLE_GROUP) + tail, lambda b: (layer, b) + (0,) * len(tail))
    stk = lambda n, *tail: pl.BlockSpec((n, SAMPLE_GROUP) + tail, lambda b: (0, b) + (0,) * len(tail))
    stack_tails = [(RET_HEADS, RET_KEY_DIM, RET_VAL_DIM), (n_cache, ATT_KV_W), (n_cache, ATT_KV_W)]
    operands = [cdec, qa, kvn, cache_k, cache_v, qd, kg, vr, sg, state, bias4, sink4]
    prev_specs = []
    if stacks is not None:
        assert all(s.shape[0] == layer for s in stacks)
        operands += list(stacks)
        prev_specs = [stk(layer, *tail) for tail in stack_tails]
    outs = pl.pallas_call(
        functools.partial(_mix_sample_kernel, n_new=n_new, n_cache=n_cache),
        grid=(dbsz // SAMPLE_GROUP,),
        in_specs=[
            pl.BlockSpec(memory_space=pltpu.SMEM),
            grp(n_new, ATT_Q_W), grp(n_new, 2 * ATT_KV_W), lay(n_cache, ATT_KV_W), lay(n_cache, ATT_KV_W),
            grp(n_new, RET_W), grp(n_new, RET_W), grp(n_new, RET_W), grp(n_new, RET_W),
            lay(RET_HEADS, RET_KEY_DIM, RET_VAL_DIM),
            _const_spec(bias4.shape), _const_spec(sink4.shape),
        ] + prev_specs,
        out_specs=[grp(n_new, D_MODEL)] + [stk(layer + 1, *tail) for tail in stack_tails],
        out_shape=[jax.ShapeDtypeStruct((dbsz, n_new, D_MODEL), BF16)]
        + [jax.ShapeDtypeStruct((layer + 1, dbsz) + tail, F32) for tail in stack_tails],
        compiler_params=_params("arbitrary"),
        name="mix_sample",
    )(*operands)
    return outs[0], outs[1:]


def _dense_kernel(x_ref, mixed_ref, ga1_ref, sh2_ref, sc2_ref, ga2_ref, g_ref, gf_ref,
                  wo_ref, wu_ref, wd_ref, o_ref, *, final):
    x1 = x_ref[0] + ga1_ref[...] * _dot(mixed_ref[0], wo_ref[...])
    h = ((_rms(x1) * g_ref[...]) * (1.0 + sc2_ref[...]) + sh2_ref[...]).astype(BF16)
    acc = jnp.zeros(x1.shape, F32)
    for j in range(D_FF // FF_BLOCK):
        u = _dot(h, wu_ref[:, j * FF_BLOCK:(j + 1) * FF_BLOCK])
        u = jnp.square(jnp.maximum(u, 0.0)).astype(BF16)
        acc = acc + _dot(u, wd_ref[j * FF_BLOCK:(j + 1) * FF_BLOCK, :])
    x2 = x1 + ga2_ref[...] * acc
    o_ref[0] = _rms(x2) * gf_ref[...] if final else x2


def _dense_call(x, mixed, mods, gains, g_final, w_out, w_up, w_down, layer, final):
    bsz, seq, d = x.shape
    t = min(TOKEN_BLOCK, seq)
    tok = lambda w: pl.BlockSpec((1, t, w), lambda b, i: (b, i, 0))
    return pl.pallas_call(
        functools.partial(_dense_kernel, final=final),
        grid=(bsz, seq // t),
        in_specs=[
            tok(d), tok(d), *_mod_specs(mods, layer, (MOD_GATE1, MOD_SHIFT2, MOD_SCALE2, MOD_GATE2), t),
            _layer_spec(gains, layer), _const_spec((1, d)),
            _layer_spec(w_out, layer), _layer_spec(w_up, layer), _layer_spec(w_down, layer),
        ],
        out_specs=tok(d),
        out_shape=jax.ShapeDtypeStruct((bsz, seq, d), F32),
        compiler_params=_params("arbitrary", "arbitrary"),
        name="dense",
    )(x, mixed, mods, mods, mods, mods, gains, g_final.reshape(1, d), w_out, w_up, w_down)


def _rotary_tables(base_pos, off_pos):
    half = RET_KEY_DIM // 2
    inv = 1.0 / (ROPE_BASE ** (jnp.arange(half, dtype=F32) / half))
    sign = jnp.concatenate([-jnp.ones((half,), F32), jnp.ones((half,), F32)])

    def both(pos):
        ang = pos.astype(F32)[:, None] * inv[None, :]
        cos, sin = jnp.cos(ang), jnp.sin(ang)
        return jnp.concatenate([cos, cos], axis=1), jnp.concatenate([sin, sin], axis=1)

    base_cos, base_sin = both(base_pos)
    off_cos, off_sin = both(off_pos)
    return base_cos[:, None, :], base_sin[:, None, :], (off_cos, off_sin, sign * off_cos, sign * off_sin)


def _retention_tables(n, rows):
    log_g = jnp.log(1.0 - 2.0 ** (-5.0 - jnp.arange(RET_HEADS, dtype=F32)))
    steps = (jnp.arange(rows) % n).astype(F32) + 1.0
    expo = steps[:, None] * jnp.repeat(log_g, RET_KEY_DIM)[None, :]
    return (jnp.exp(expo), jnp.exp(-expo) * (RET_KEY_DIM ** -0.5)), jnp.exp(n * log_g)


def kernel(x_prompt, x_sample, c_prompt, c_sample, cache_win_k, cache_win_v, state_ret, g_mix, g_mlp, w_ada,
           b_ada, w_in, w_out, att_sinks, rel_bias, w_up, w_down, g_final):
    depth = w_in.shape[0]
    bsz, seq, d = x_prompt.shape
    dbsz, dec_seq, _ = x_sample.shape
    n_cache = cache_win_k.shape[2]
    n_rows = dbsz * dec_seq
    keep_p = min(WINDOW, seq)

    w_in_b, w_out_b, w_up_b, w_down_b = (w.astype(BF16) for w in (w_in, w_out, w_up, w_down))
    g_mix3 = g_mix.reshape(depth, 1, d)
    g_mlp3 = g_mlp.reshape(depth, 1, d)
    cache_k = cache_win_k.reshape(depth, dbsz, n_cache, ATT_KV_W)
    cache_v = cache_win_v.reshape(depth, dbsz, n_cache, ATT_KV_W)

    c_all = jnp.concatenate([c_prompt, c_sample], axis=0)
    c_all = jnp.pad(c_all, ((0, -c_all.shape[0] % 8), (0, 0)))
    mods = _ada_call(c_all, w_ada, b_ada)
    mods_p = mods[:, :bsz, None, :]
    mods_s = jnp.repeat(mods[:, bsz:bsz + dbsz], dec_seq, axis=1)

    t_p = min(TOKEN_BLOCK, seq)
    rot_p = _rotary_tables(jnp.arange(seq // t_p) * t_p, jnp.arange(t_p))
    rot_s = _rotary_tables(jnp.full((1,), PAST_LEN), jnp.tile(jnp.arange(dec_seq), dbsz))
    decay_p, cdec_p = _retention_tables(GROUP, t_p)
    decay_s, cdec_s = _retention_tables(dec_seq, n_rows)

    rel_p = (jnp.arange(BAND) - BAND_CHUNKS * CHUNK)[None, :] - jnp.arange(CHUNK)[:, None]
    bias_p = _banded_bias(_bias_call(rel_bias, _rel_bucket(rel_p)) * LOG2_E)
    kpos = PAST_LEN - n_cache + jnp.arange(n_cache + dec_seq)
    qpos = PAST_LEN + jnp.arange(dec_seq)
    bias_s = _bias_call(rel_bias, _rel_bucket(kpos[None, :] - qpos[:, None])) * LOG2_E
    bias_s = jnp.pad(bias_s, ((0, 0), (0, 0), (0, SAMPLE_KEYS - n_cache - dec_seq)), constant_values=-jnp.inf)
    bias_s = _pair_rows(bias_s)

    xp = x_prompt
    xs = x_sample.reshape(1, n_rows, d)
    pk, pv, pr = [], [], []
    stacks = None
    for l in range(depth):
        sinks = att_sinks[l].astype(F32)[:, None, None] * LOG2_E
        sink_p = _pair_lanes(jnp.broadcast_to(sinks, (ATT_HEADS, 1, CHUNK)))
        sink_s = _pair_rows(jnp.broadcast_to(sinks, (ATT_HEADS, dec_seq, 1)))
        final = l == depth - 1

        qa, kx, va, qd, kg, vr, sg, kvt = _inproj_call(xp, mods_p, g_mix3, w_in_b, l, rot_p, decay_p, keep_p)
        mixed, st = _mix_prompt_call(qa, kx, va, qd, kg, vr, sg, bias_p, sink_p, cdec_p)
        xp = _dense_call(xp, mixed, mods_p, g_mlp3, g_final, w_out_b, w_up_b, w_down_b, l, final)
        pk.append(kvt[:, :, :ATT_KV_W].reshape(bsz, keep_p, ATT_KV_HEADS, ATT_HEAD_DIM))
        pv.append(kvt[:, :, ATT_KV_W:].reshape(bsz, keep_p, ATT_KV_HEADS, ATT_HEAD_DIM))
        pr.append(st)

        qa, _, _, qd, kg, vr, sg, kvn = _inproj_call(xs, mods_s, g_mix3, w_in_b, l, rot_s, decay_s, n_rows)
        per_seq = lambda a: a.reshape(dbsz, dec_seq, a.shape[-1])
        mixed, stacks = _mix_sample_call(per_seq(qa), per_seq(kvn), cache_k, cache_v, per_seq(qd), per_seq(kg),
                                         per_seq(vr), per_seq(sg), state_ret, l, stacks, bias_s, sink_s, cdec_s)
        xs = _dense_call(xs, mixed.reshape(1, n_rows, d), mods_s, g_mlp3, g_final, w_out_b, w_up_b, w_down_b, l,
                         final)

    sr, sk, sv = stacks
    win_shape = (depth, dbsz, n_cache, ATT_KV_HEADS, ATT_HEAD_DIM)
    return (xp, xs.reshape(dbsz, dec_seq, d), jnp.stack(pk), jnp.stack(pv), jnp.stack(pr),
            sk.reshape(win_shape), sv.reshape(win_shape), sr)
```

```python
import functools
import math

import jax
import jax.numpy as jnp
from jax import lax
from jax.experimental import pallas as pl
from jax.experimental.pallas import tpu as pltpu

F32 = jnp.float32
BF16 = jnp.bfloat16

D_MODEL = 1024
CHUNK = 64
ATT_HEADS = 8
ATT_KV_HEADS = 2
ATT_HEAD_DIM = 64
ATT_Q_W = ATT_HEADS * ATT_HEAD_DIM
ATT_KV_W = ATT_KV_HEADS * ATT_HEAD_DIM
ATT_SCALE = ATT_HEAD_DIM ** -0.5
WINDOW = 128
BAND_CHUNKS = -(-WINDOW // CHUNK)
BAND = (BAND_CHUNKS + 1) * CHUNK
REL_BUCKETS = 32
REL_MAX_DIST = 128
RET_HEADS = 4
RET_KEY_DIM = 128
RET_VAL_DIM = 128
RET_W = RET_HEADS * RET_KEY_DIM
ROPE_BASE = 10000.0
IN_W = ATT_Q_W + 2 * ATT_KV_W + 4 * RET_W
D_FF = 4 * D_MODEL
N_MOD = 6
EPS = 1e-6
PAST_LEN = 4096
LOG2_E = math.log2(math.e)

LANES = 128
HALF_LANES = LANES // 2
BF16_SUBLANES = 16
VMEM_LIMIT_BYTES = 56 * 1024 * 1024

OFF_QA = 0
OFF_KV = OFF_QA + ATT_Q_W
OFF_QR = OFF_KV + 2 * ATT_KV_W
OFF_KR = OFF_QR + RET_W
OFF_VR = OFF_KR + RET_W
OFF_GR = OFF_VR + RET_W

TOKEN_BLOCK = 1024
INPROJ_SUB_BLOCK = 256
MIX_TOKEN_BLOCK = 1024
GROUP = 128
FF_BLOCK = 1024
ADA_COL_BLOCK = 1024
SAMPLE_GROUP = 4
SAMPLE_KEYS = 2 * LANES


def _params(*sem):
    return pltpu.CompilerParams(dimension_semantics=sem, vmem_limit_bytes=VMEM_LIMIT_BYTES)


def _const_spec(shape):
    zeros = (0,) * len(shape)
    return pl.BlockSpec(shape, lambda *_: zeros, pipeline_mode=pl.Buffered(1))


def _rms(x):
    return x * lax.rsqrt(jnp.mean(x * x, axis=-1, keepdims=True) + EPS)


def _dot(a, b):
    return jnp.dot(a, b, preferred_element_type=F32)


def _dot_nt(a, b):
    return lax.dot_general(a, b, (((1,), (1,)), ((), ())), preferred_element_type=F32)


def _dot_tn(a, b):
    return lax.dot_general(a, b, (((0,), (0,)), ((), ())), preferred_element_type=F32)


def _ada_kernel(c_ref, w_ref, b_ref, o_ref):
    a = jax.nn.silu(c_ref[...]).astype(BF16)
    o_ref[0] = _dot(a, w_ref[0].astype(BF16)) + b_ref[0]


def _ada_call(c_all, w_ada, b_ada):
    depth, d, n = w_ada.shape
    rows = c_all.shape[0]
    return pl.pallas_call(
        _ada_kernel,
        grid=(depth, n // ADA_COL_BLOCK),
        in_specs=[
            pl.BlockSpec((rows, d), lambda l, j: (0, 0)),
            pl.BlockSpec((1, d, ADA_COL_BLOCK), lambda l, j: (l, 0, j)),
            pl.BlockSpec((1, 1, ADA_COL_BLOCK), lambda l, j: (l, 0, j)),
        ],
        out_specs=pl.BlockSpec((1, rows, ADA_COL_BLOCK), lambda l, j: (l, 0, j)),
        out_shape=jax.ShapeDtypeStruct((depth, rows, n), F32),
        compiler_params=_params("arbitrary", "arbitrary"),
        name="ada",
    )(c_all, w_ada, b_ada.reshape(depth, 1, n))


def _bias_kernel(tab_ref, idx_ref, o_ref):
    h = pl.program_id(0)
    idx = idx_ref[...]
    acc = jnp.zeros(idx.shape, F32)
    for b in range(REL_BUCKETS):
        acc = jnp.where(idx == b, tab_ref[b, h], acc)
    o_ref[0] = acc


def _bias_call(rel_bias, bucket):
    q, j = bucket.shape
    return pl.pallas_call(
        _bias_kernel,
        grid=(ATT_HEADS,),
        in_specs=[
            pl.BlockSpec(memory_space=pltpu.SMEM),
            pl.BlockSpec((q, j), lambda h: (0, 0)),
        ],
        out_specs=pl.BlockSpec((1, q, j), lambda h: (h, 0, 0)),
        out_shape=jax.ShapeDtypeStruct((ATT_HEADS, q, j), F32),
        compiler_params=_params("arbitrary"),
        name="rel_bias",
    )(rel_bias, bucket)


def _rel_bucket(rel):
    nb = REL_BUCKETS // 2
    n = -rel
    ret = jnp.where(n < 0, nb, 0)
    n = jnp.abs(n)
    max_exact = nb // 2
    nf = jnp.maximum(n, 1).astype(F32)
    large = max_exact + (jnp.log(nf / max_exact) / math.log(REL_MAX_DIST / max_exact)
                         * (nb - max_exact)).astype(jnp.int32)
    large = jnp.minimum(large, nb - 1)
    return (ret + jnp.where(n < max_exact, n, large)).astype(jnp.int32)


def _pair_rows(per_head):
    out = []
    for k in range(ATT_KV_HEADS):
        for e in range(2):
            out.append(jnp.concatenate([per_head[4 * k + e], per_head[4 * k + 2 + e]], axis=0))
    return jnp.stack(out)


def _pair_lanes(per_head):
    out = []
    for k in range(ATT_KV_HEADS):
        for e in range(2):
            out.append(jnp.concatenate([per_head[4 * k + e], per_head[4 * k + 2 + e]], axis=-1))
    return jnp.stack(out)


def _banded_bias(per_head):
    base = _pair_lanes(jnp.swapaxes(per_head, 1, 2))
    key = jnp.arange(BAND)[None, :, None]
    return jnp.stack([jnp.where(key < (BAND_CHUNKS - v) * CHUNK, -jnp.inf, base)
                      for v in range(BAND_CHUNKS + 1)])


def _rotary(u, cos2, sin2):
    parts = []
    for h in range(RET_HEADS):
        x = u[:, h * RET_KEY_DIM:(h + 1) * RET_KEY_DIM]
        parts.append(x * cos2 + pltpu.roll(x, HALF_LANES, 1) * sin2)
    return jnp.concatenate(parts, axis=1)


def _inproj_kernel(x_ref, sh_ref, sc_ref, g_ref, w_ref, ac_ref, as_ref, bc_ref, bs_ref, bcn_ref, bsn_ref,
                   qdec_ref, kdec_ref,
                   qa_ref, kx_ref, va_ref, qd_ref, kg_ref, vr_ref, sg_ref, kvt_ref, *, tail):
    i = pl.program_id(1)
    t = x_ref.shape[1]
    sub = min(INPROJ_SUB_BLOCK, t)
    for r0 in range(0, t, sub):
        rows = slice(r0, r0 + sub)
        per_row = lambda ref: ref[rows] if ref.shape[0] == t else ref[...]
        x = x_ref[0, rows]
        h = (_rms(x) * g_ref[...]) * (1.0 + per_row(sc_ref)) + per_row(sh_ref)
        hb = h.astype(BF16)
        cos2 = ac_ref[...] * bc_ref[rows] - as_ref[...] * bs_ref[rows]
        sin2 = as_ref[...] * bcn_ref[rows] + ac_ref[...] * bsn_ref[rows]

        qa_ref[0, rows] = (_dot(hb, w_ref[:, OFF_QA:OFF_KV]) * (ATT_SCALE * LOG2_E)).astype(BF16)
        kv = _dot(hb, w_ref[:, OFF_KV:OFF_QR])
        kx_ref[0, rows] = jnp.concatenate(_lane_split(kv[:, :ATT_KV_W]), axis=1).astype(BF16)
        va_ref[0, rows] = kv[:, ATT_KV_W:].astype(BF16)
        qd_ref[0, rows] = (_rotary(_dot(hb, w_ref[:, OFF_QR:OFF_KR]), cos2, sin2) * qdec_ref[rows]).astype(BF16)
        kg_ref[0, rows] = (_rotary(_dot(hb, w_ref[:, OFF_KR:OFF_VR]), cos2, sin2) * kdec_ref[rows]).astype(BF16)
        vr_ref[0, rows] = _dot(hb, w_ref[:, OFF_VR:OFF_GR]).astype(BF16)
        sg_ref[0, rows] = jax.nn.silu(_dot(hb, w_ref[:, OFF_GR:IN_W])).astype(BF16)

        lo = max(r0, t - tail)
        if lo < r0 + sub:
            @pl.when(i == pl.num_programs(1) - 1)
            def _(kv=kv, lo=lo, r0=r0):
                kvt_ref[0, lo - (t - tail):r0 + sub - (t - tail)] = kv[lo - r0:]


MOD_SHIFT1, MOD_SCALE1, MOD_GATE1, MOD_SHIFT2, MOD_SCALE2, MOD_GATE2 = range(N_MOD)


def _mod_specs(mods, layer, which, t):
    if mods.ndim == 4:
        return [pl.BlockSpec((None, None, 1, D_MODEL), lambda b, i, m=m: (layer, b, 0, m)) for m in which]
    return [pl.BlockSpec((None, t, D_MODEL), lambda b, i, m=m: (layer, i, m)) for m in which]


def _layer_spec(arr, layer):
    tail_zeros = (0,) * (arr.ndim - 1)
    return pl.BlockSpec((None,) + arr.shape[1:], lambda *_: (layer,) + tail_zeros, pipeline_mode=pl.Buffered(1))


def _inproj_call(x, mods, gains, w_in, layer, rot, decay, tail):
    bsz, seq, d = x.shape
    t = min(TOKEN_BLOCK, seq)
    nb = seq // t
    base_cos, base_sin, off = rot
    assert base_cos.shape == (nb, 1, LANES) and off[0].shape == (t, LANES)
    assert all(a.shape == (t, RET_W) for a in decay)
    tok = lambda w: pl.BlockSpec((1, t, w), lambda b, i: (b, i, 0))
    out_bf = lambda w: jax.ShapeDtypeStruct((bsz, seq, w), BF16)
    base = pl.BlockSpec((None, 1, LANES), lambda b, i: (i, 0, 0))
    return pl.pallas_call(
        functools.partial(_inproj_kernel, tail=tail),
        grid=(bsz, nb),
        in_specs=[
            tok(d), *_mod_specs(mods, layer, (MOD_SHIFT1, MOD_SCALE1), t),
            _layer_spec(gains, layer), _const_spec(w_in.shape),
            base, base, *[_const_spec((t, LANES))] * 4, *[_const_spec((t, RET_W))] * 2,
        ],
        out_specs=[
            tok(ATT_Q_W), tok(4 * LANES), tok(ATT_KV_W), tok(RET_W), tok(RET_W), tok(RET_W), tok(RET_W),
            pl.BlockSpec((1, tail, 2 * ATT_KV_W), lambda b, i: (b, 0, 0)),
        ],
        out_shape=[
            out_bf(ATT_Q_W), out_bf(4 * LANES), out_bf(ATT_KV_W), out_bf(RET_W), out_bf(RET_W),
            out_bf(RET_W), out_bf(RET_W),
            jax.ShapeDtypeStruct((bsz, tail, 2 * ATT_KV_W), F32),
        ],
        compiler_params=_params("arbitrary", "arbitrary"),
        name="inproj",
    )(x, mods, mods, gains, w_in, base_cos, base_sin, *off, *decay)


def _lane_split(x):
    lo = lax.broadcasted_iota(jnp.int32, x.shape, 1) < HALF_LANES
    sw = pltpu.roll(x, HALF_LANES, 1)
    zero = jnp.zeros_like(x)
    return (jnp.where(lo, x, zero), jnp.where(lo, zero, sw), jnp.where(lo, sw, zero), jnp.where(lo, zero, x))


def _sink_softmax2(s, sink):
    m = jnp.maximum(jnp.max(s, axis=-1, keepdims=True), sink)
    e = jnp.exp2(s - m)
    den = jnp.sum(e, axis=-1, keepdims=True) + jnp.exp2(sink - m)
    return e / den


def _mix_prompt_kernel(cdec_ref, qa_ref, ka_ref, va_ref, qd_ref, kg_ref, vr_ref, sg_ref, bias_ref, sink_ref,
                       *rest, t, n_cast):
    cast_src = rest[:n_cast]
    mixed_ref, state_ref = rest[n_cast:n_cast + 2]
    cast_dst = rest[n_cast + 2:2 * n_cast + 2]
    kx_ref, vt_ref, s_ref = rest[2 * n_cast + 2:]
    for src, dst in zip(cast_src, cast_dst):
        dst[...] = src[...].astype(BF16)

    i = pl.program_id(1)
    n_groups = t // GROUP
    hist = BAND - CHUNK

    hd = ATT_HEAD_DIM

    @pl.when(i == 0)
    def _():
        kx_ref[:, 0:hist, :] = jnp.zeros((4, hist, LANES), BF16)
        vt_ref[0, :, 0:hd, 0:GROUP] = jnp.zeros((ATT_KV_HEADS, hd, GROUP), BF16)
        vt_ref[:, :, hd:, :] = jnp.ones((n_groups, ATT_KV_HEADS, BF16_SUBLANES, 2 * GROUP), BF16)
        s_ref[...] = jnp.zeros(s_ref.shape, F32)

    @pl.when(i > 0)
    def _():
        kx_ref[:, 0:hist, :] = kx_ref[:, t:t + hist, :]
        vt_ref[0, :, 0:hd, 0:GROUP] = vt_ref[n_groups - 1, :, 0:hd, GROUP:]

    for n in range(4):
        kx_ref[n, hist:hist + t, :] = ka_ref[0, :, n * LANES:(n + 1) * LANES]
    for m in range(n_groups):
        tile = va_ref[0, m * GROUP:(m + 1) * GROUP, :].astype(F32).T.astype(BF16)
        for k in range(ATT_KV_HEADS):
            vt_ref[m, k, 0:hd, GROUP:] = tile[k * hd:(k + 1) * hd]
            if m + 1 < n_groups:
                vt_ref[m + 1, k, 0:hd, 0:GROUP] = tile[k * hd:(k + 1) * hd]

    no_keys = jnp.zeros((CHUNK, LANES), BF16)
    causal = (lax.broadcasted_iota(jnp.int32, (GROUP, GROUP), 0)
              >= lax.broadcasted_iota(jnp.int32, (GROUP, GROUP), 1))

    def group(g, carry):
        g0 = pl.multiple_of(g * GROUP, GROUP)
        grows = pl.ds(g0, GROUP)
        heads = [slice(h * LANES, (h + 1) * LANES) for h in range(RET_HEADS)]
        att = [(j, k, e) for j in range(GROUP // CHUNK) for k in range(ATT_KV_HEADS) for e in range(2)]

        def chunk_rows(j):
            return pl.ds(pl.multiple_of(g0 + j * CHUNK, CHUNK), CHUNK)

        raw = {}
        for j in range(GROUP // CHUNK):
            band = pl.ds(pl.multiple_of(g0 + j * CHUNK, CHUNK), BAND)
            for k in range(ATT_KV_HEADS):
                c0 = 2 * LANES * k
                q2 = jnp.concatenate([qa_ref[0, chunk_rows(j), c0:c0 + LANES],
                                      qa_ref[0, chunk_rows(j), c0 + LANES:c0 + 2 * LANES]], axis=0)
                keys = jnp.concatenate([kx_ref[2 * k, band, :], kx_ref[2 * k + 1, band, :]], axis=0)
                s2 = _dot_nt(keys, q2)
                raw[j, k, 0], raw[j, k, 1] = s2[:BAND], s2[BAND:]
        states = [s_ref[h] for h in range(RET_HEADS)]
        ret_q = [qd_ref[0, grows, c] for c in heads]
        ret_k = [kg_ref[0, grows, c] for c in heads]
        ret_v = [vr_ref[0, grows, c] for c in heads]
        ret_scores = [_dot_nt(ret_q[h], ret_k[h]) for h in range(RET_HEADS)]
        for h in range(RET_HEADS):
            s_ref[h] = cdec_ref[h] * (states[h] + _dot_tn(ret_k[h], ret_v[h]))

        probs, sink_terms = {}, {}
        for j, k, e in att:
            n = 2 * k + e
            variant = jnp.minimum(i * (t // CHUNK) + 2 * g + j, BAND_CHUNKS)
            s = raw[j, k, e] + bias_ref[variant, n]
            sink = sink_ref[n]
            m = jnp.maximum(jnp.max(s, axis=0, keepdims=True), sink)
            probs[j, k, e] = jnp.exp2(s - m).astype(BF16)
            sink_terms[j, k, e] = jnp.exp2(sink - m)
        ret_p = [jnp.where(causal, ret_scores[h], 0.0).astype(BF16) for h in range(RET_HEADS)]

        out_t = {}
        for j, k, e in att:
            frame = [probs[j, k, e], no_keys] if j == 0 else [no_keys, probs[j, k, e]]
            out_t[j, k, e] = _dot(vt_ref[g, k], jnp.concatenate(frame, axis=0))
        ret_out = [_dot(jnp.concatenate([ret_p[h], ret_q[h]], axis=1),
                        jnp.concatenate([ret_v[h], states[h].astype(BF16)], axis=0)) for h in range(RET_HEADS)]

        for j in range(GROUP // CHUNK):
            for k in range(ATT_KV_HEADS):
                c0 = 2 * LANES * k
                halves = []
                for e in range(2):
                    o_t = out_t[j, k, e]
                    den = o_t[ATT_HEAD_DIM:ATT_HEAD_DIM + 1] + sink_terms[j, k, e]
                    halves.append(o_t[:ATT_HEAD_DIM] * (1.0 / den))
                o = jnp.concatenate(halves, axis=0).T
                mixed_ref[0, chunk_rows(j), c0:c0 + LANES] = o[:CHUNK].astype(BF16)
                mixed_ref[0, chunk_rows(j), c0 + LANES:c0 + 2 * LANES] = o[CHUNK:].astype(BF16)
        for h in range(RET_HEADS):
            out = _rms(ret_out[h]) * sg_ref[0, grows, heads[h]].astype(F32)
            mixed_ref[0, grows, ATT_Q_W + h * LANES:ATT_Q_W + (h + 1) * LANES] = out.astype(BF16)
        return carry

    lax.fori_loop(0, n_groups, group, 0, unroll=True)

    @pl.when(i == pl.num_programs(1) - 1)
    def _():
        state_ref[0] = s_ref[...]


def _mix_prompt_call(qa, kx, va, qd, kg, vr, sg, bias4, sink4, cdec, weights):
    bsz, seq, _ = qa.shape
    t = min(MIX_TOKEN_BLOCK, seq)
    nb = seq // t
    hist = BAND - CHUNK
    assert hist == GROUP and t % GROUP == 0 and seq % t == 0
    tok = lambda w: pl.BlockSpec((1, t, w), lambda b, i: (b, i, 0))
    steps = bsz * nb
    slabs, slab_in, slab_out, slab_shape = [], [], [], []
    for stack, layer in weights:
        depth, rows, cols = stack.shape
        assert rows % (steps * BF16_SUBLANES) == 0
        slabs.append(stack.reshape(depth, steps, rows // steps, cols))
        slab_in.append(pl.BlockSpec((None, None, rows // steps, cols), lambda b, i, layer=layer: (layer, b * nb + i, 0, 0)))
        slab_out.append(pl.BlockSpec((None, rows // steps, cols), lambda b, i: (b * nb + i, 0, 0)))
        slab_shape.append(jax.ShapeDtypeStruct((steps, rows // steps, cols), BF16))
    outs = pl.pallas_call(
        functools.partial(_mix_prompt_kernel, t=t, n_cast=len(weights)),
        grid=(bsz, nb),
        in_specs=[
            pl.BlockSpec(memory_space=pltpu.SMEM),
            tok(ATT_Q_W), tok(4 * LANES), tok(ATT_KV_W), tok(RET_W), tok(RET_W), tok(RET_W), tok(RET_W),
            _const_spec(bias4.shape), _const_spec(sink4.shape),
        ] + slab_in,
        out_specs=[
            tok(D_MODEL),
            pl.BlockSpec((1, RET_HEADS, RET_KEY_DIM, RET_VAL_DIM), lambda b, i: (b, 0, 0, 0)),
        ] + slab_out,
        out_shape=[
            jax.ShapeDtypeStruct((bsz, seq, D_MODEL), BF16),
            jax.ShapeDtypeStruct((bsz, RET_HEADS, RET_KEY_DIM, RET_VAL_DIM), F32),
        ] + slab_shape,
        scratch_shapes=[
            pltpu.VMEM((4, hist + t, LANES), BF16),
            pltpu.VMEM((t // GROUP, ATT_KV_HEADS, ATT_HEAD_DIM + BF16_SUBLANES, 2 * GROUP), BF16),
            pltpu.VMEM((RET_HEADS, RET_KEY_DIM, RET_VAL_DIM), F32),
        ],
        compiler_params=_params("arbitrary", "arbitrary"),
        name="mix_prompt",
    )(cdec, qa, kx, va, qd, kg, vr, sg, bias4, sink4, *slabs)
    rounded = [o.reshape(stack.shape[1:]) for o, (stack, _) in zip(outs[2:], weights)]
    return outs[0], outs[1], rounded


def _mix_sample_kernel(cdec_ref, qa_ref, kvn_ref, ck_ref, cv_ref, qd_ref, kg_ref, vr_ref, sg_ref, s0_ref,
                       bias_ref, sink_ref, *rest, n_new, n_cache):
    mixed_ref, *stack_refs = rest[-4:]
    done = 0
    if len(rest) > 4:
        done = rest[0].shape[0]
        for prev_ref, out_ref in zip(rest[:3], stack_refs):
            out_ref[0:done] = prev_ref[...]
    state_ref, wk_ref, wv_ref = (r.at[done] for r in stack_refs)
    pad = jnp.zeros((SAMPLE_KEYS - n_cache - n_new, LANES), F32)
    causal = (lax.broadcasted_iota(jnp.int32, (n_new, n_new), 0)
              >= lax.broadcasted_iota(jnp.int32, (n_new, n_new), 1))
    seqs = range(SAMPLE_GROUP)
    heads = [slice(h * LANES, (h + 1) * LANES) for h in range(RET_HEADS)]
    att = [(g, k, e) for g in seqs for k in range(ATT_KV_HEADS) for e in range(2)]
    ret = [(g, h) for g in seqs for h in range(RET_HEADS)]

    v4, raw = {}, {}
    for g in seqs:
        kvn = kvn_ref[g]
        k_all = jnp.concatenate([ck_ref[g], kvn[:, :ATT_KV_W], pad], axis=0)
        v_all = jnp.concatenate([cv_ref[g], kvn[:, ATT_KV_W:], pad], axis=0)
        wk_ref[g] = k_all[n_new:n_new + n_cache]
        wv_ref[g] = v_all[n_new:n_new + n_cache]
        k4 = [p.astype(BF16) for p in _lane_split(k_all)]
        v4[g] = [p.astype(BF16) for p in _lane_split(v_all)]
        for k in range(ATT_KV_HEADS):
            c0 = 2 * LANES * k
            q2 = jnp.concatenate([qa_ref[g, :, c0:c0 + LANES], qa_ref[g, :, c0 + LANES:c0 + 2 * LANES]], axis=0)
            for e in range(2):
                raw[g, k, e] = _dot_nt(q2, k4[2 * k + e])
    qd = {(g, h): qd_ref[g, :, heads[h]] for g, h in ret}
    kg = {(g, h): kg_ref[g, :, heads[h]] for g, h in ret}
    vr = {(g, h): vr_ref[g, :, heads[h]] for g, h in ret}
    state = {(g, h): s0_ref[g, h] for g, h in ret}
    ret_scores = {gh: _dot_nt(qd[gh], kg[gh]) for gh in ret}
    ret_inter = {gh: _dot(qd[gh], state[gh].astype(BF16)) for gh in ret}
    for g, h in ret:
        state_ref[g, h] = cdec_ref[h] * (state[g, h] + _dot_tn(kg[g, h], vr[g, h]))

    probs = {(g, k, e): _sink_softmax2(raw[g, k, e] + bias_ref[2 * k + e], sink_ref[2 * k + e]).astype(BF16)
             for g, k, e in att}
    ret_p = {gh: jnp.where(causal, ret_scores[gh], 0.0).astype(BF16) for gh in ret}

    out = {(g, k, e): _dot(probs[g, k, e], v4[g][2 * k + e]) for g, k, e in att}
    ret_intra = {gh: _dot(ret_p[gh], vr[gh]) for gh in ret}

    for g in seqs:
        for k in range(ATT_KV_HEADS):
            c0 = 2 * LANES * k
            o = out[g, k, 0] + out[g, k, 1]
            mixed_ref[g, :, c0:c0 + LANES] = o[:n_new].astype(BF16)
            mixed_ref[g, :, c0 + LANES:c0 + 2 * LANES] = o[n_new:].astype(BF16)
        for h in range(RET_HEADS):
            ro = _rms(ret_intra[g, h] + ret_inter[g, h]) * sg_ref[g, :, heads[h]].astype(F32)
            mixed_ref[g, :, ATT_Q_W + h * LANES:ATT_Q_W + (h + 1) * LANES] = ro.astype(BF16)


def _mix_sample_call(qa, kvn, cache_k, cache_v, qd, kg, vr, sg, state, layer, stacks, bias4, sink4, cdec):
    dbsz, n_new, _ = qa.shape
    n_cache = cache_k.shape[2]
    grp = lambda *tail: pl.BlockSpec((SAMPLE_GROUP,) + tail, lambda b: (b,) + (0,) * len(tail))
    lay = lambda *tail: pl.BlockSpec((None, SAMPLE_GROUP) + tail, lambda b: (layer, b) + (0,) * len(tail))
    stk = lambda n, *tail: pl.BlockSpec((n, SAMPLE_GROUP) + tail, lambda b: (0, b) + (0,) * len(tail))
    stack_tails = [(RET_HEADS, RET_KEY_DIM, RET_VAL_DIM), (n_cache, ATT_KV_W), (n_cache, ATT_KV_W)]
    operands = [cdec, qa, kvn, cache_k, cache_v, qd, kg, vr, sg, state, bias4, sink4]
    prev_specs = []
    if stacks is not None:
        assert all(s.shape[0] == layer for s in stacks)
        operands += list(stacks)
        prev_specs = [stk(layer, *tail) for tail in stack_tails]
    outs = pl.pallas_call(
        functools.partial(_mix_sample_kernel, n_new=n_new, n_cache=n_cache),
        grid=(dbsz // SAMPLE_GROUP,),
        in_specs=[
            pl.BlockSpec(memory_space=pltpu.SMEM),
            grp(n_new, ATT_Q_W), grp(n_new, 2 * ATT_KV_W), lay(n_cache, ATT_KV_W), lay(n_cache, ATT_KV_W),
            grp(n_new, RET_W), grp(n_new, RET_W), grp(n_new, RET_W), grp(n_new, RET_W),
            lay(RET_HEADS, RET_KEY_DIM, RET_VAL_DIM),
            _const_spec(bias4.shape), _const_spec(sink4.shape),
        ] + prev_specs,
        out_specs=[grp(n_new, D_MODEL)] + [stk(layer + 1, *tail) for tail in stack_tails],
        out_shape=[jax.ShapeDtypeStruct((dbsz, n_new, D_MODEL), BF16)]
        + [jax.ShapeDtypeStruct((layer + 1, dbsz) + tail, F32) for tail in stack_tails],
        compiler_params=_params("arbitrary"),
        name="mix_sample",
    )(*operands)
    return outs[0], outs[1:]


def _dense_kernel(x_ref, mixed_ref, ga1_ref, sh2_ref, sc2_ref, ga2_ref, g_ref, gf_ref,
                  wo_ref, wu_ref, wd_ref, o_ref, *, final):
    x1 = x_ref[0] + ga1_ref[...] * _dot(mixed_ref[0], wo_ref[...])
    h = ((_rms(x1) * g_ref[...]) * (1.0 + sc2_ref[...]) + sh2_ref[...]).astype(BF16)
    acc = jnp.zeros(x1.shape, F32)
    for j in range(D_FF // FF_BLOCK):
        u = _dot(h, wu_ref[:, j * FF_BLOCK:(j + 1) * FF_BLOCK])
        u = jnp.square(jnp.maximum(u, 0.0)).astype(BF16)
        acc = acc + _dot(u, wd_ref[j * FF_BLOCK:(j + 1) * FF_BLOCK, :])
    x2 = x1 + ga2_ref[...] * acc
    o_ref[0] = _rms(x2) * gf_ref[...] if final else x2


def _dense_call(x, mixed, mods, gains, g_final, w_out, w_up, w_down, layer, final):
    bsz, seq, d = x.shape
    t = min(TOKEN_BLOCK, seq)
    tok = lambda w: pl.BlockSpec((1, t, w), lambda b, i: (b, i, 0))
    return pl.pallas_call(
        functools.partial(_dense_kernel, final=final),
        grid=(bsz, seq // t),
        in_specs=[
            tok(d), tok(d), *_mod_specs(mods, layer, (MOD_GATE1, MOD_SHIFT2, MOD_SCALE2, MOD_GATE2), t),
            _layer_spec(gains, layer), _const_spec((1, d)),
            _const_spec(w_out.shape), _const_spec(w_up.shape), _const_spec(w_down.shape),
        ],
        out_specs=tok(d),
        out_shape=jax.ShapeDtypeStruct((bsz, seq, d), F32),
        compiler_params=_params("arbitrary", "arbitrary"),
        name="dense",
    )(x, mixed, mods, mods, mods, mods, gains, g_final.reshape(1, d), w_out, w_up, w_down)


def _rotary_tables(base_pos, off_pos):
    half = RET_KEY_DIM // 2
    inv = 1.0 / (ROPE_BASE ** (jnp.arange(half, dtype=F32) / half))
    sign = jnp.concatenate([-jnp.ones((half,), F32), jnp.ones((half,), F32)])

    def both(pos):
        ang = pos.astype(F32)[:, None] * inv[None, :]
        cos, sin = jnp.cos(ang), jnp.sin(ang)
        return jnp.concatenate([cos, cos], axis=1), jnp.concatenate([sin, sin], axis=1)

    base_cos, base_sin = both(base_pos)
    off_cos, off_sin = both(off_pos)
    return base_cos[:, None, :], base_sin[:, None, :], (off_cos, off_sin, sign * off_cos, sign * off_sin)


def _retention_tables(n, rows):
    log_g = jnp.log(1.0 - 2.0 ** (-5.0 - jnp.arange(RET_HEADS, dtype=F32)))
    steps = (jnp.arange(rows) % n).astype(F32) + 1.0
    expo = steps[:, None] * jnp.repeat(log_g, RET_KEY_DIM)[None, :]
    return (jnp.exp(expo), jnp.exp(-expo) * (RET_KEY_DIM ** -0.5)), jnp.exp(n * log_g)


def kernel(x_prompt, x_sample, c_prompt, c_sample, cache_win_k, cache_win_v, state_ret, g_mix, g_mlp, w_ada,
           b_ada, w_in, w_out, att_sinks, rel_bias, w_up, w_down, g_final):
    depth = w_in.shape[0]
    bsz, seq, d = x_prompt.shape
    dbsz, dec_seq, _ = x_sample.shape
    n_cache = cache_win_k.shape[2]
    n_rows = dbsz * dec_seq
    keep_p = min(WINDOW, seq)

    w_in_b = w_in[0].astype(BF16)
    g_mix3 = g_mix.reshape(depth, 1, d)
    g_mlp3 = g_mlp.reshape(depth, 1, d)
    cache_k = cache_win_k.reshape(depth, dbsz, n_cache, ATT_KV_W)
    cache_v = cache_win_v.reshape(depth, dbsz, n_cache, ATT_KV_W)

    c_all = jnp.concatenate([c_prompt, c_sample], axis=0)
    c_all = jnp.pad(c_all, ((0, -c_all.shape[0] % 8), (0, 0)))
    mods = _ada_call(c_all, w_ada, b_ada)
    mods_p = mods[:, :bsz, None, :]
    mods_s = jnp.repeat(mods[:, bsz:bsz + dbsz], dec_seq, axis=1)

    t_p = min(TOKEN_BLOCK, seq)
    rot_p = _rotary_tables(jnp.arange(seq // t_p) * t_p, jnp.arange(t_p))
    rot_s = _rotary_tables(jnp.full((1,), PAST_LEN), jnp.tile(jnp.arange(dec_seq), dbsz))
    decay_p, cdec_p = _retention_tables(GROUP, t_p)
    decay_s, cdec_s = _retention_tables(dec_seq, n_rows)

    rel_p = (jnp.arange(BAND) - BAND_CHUNKS * CHUNK)[None, :] - jnp.arange(CHUNK)[:, None]
    bias_p = _banded_bias(_bias_call(rel_bias, _rel_bucket(rel_p)) * LOG2_E)
    kpos = PAST_LEN - n_cache + jnp.arange(n_cache + dec_seq)
    qpos = PAST_LEN + jnp.arange(dec_seq)
    bias_s = _bias_call(rel_bias, _rel_bucket(kpos[None, :] - qpos[:, None])) * LOG2_E
    bias_s = jnp.pad(bias_s, ((0, 0), (0, 0), (0, SAMPLE_KEYS - n_cache - dec_seq)), constant_values=-jnp.inf)
    bias_s = _pair_rows(bias_s)

    xp = x_prompt
    xs = x_sample.reshape(1, n_rows, d)
    pk, pv, pr = [], [], []
    stacks = None
    for l in range(depth):
        sinks = att_sinks[l].astype(F32)[:, None, None] * LOG2_E
        sink_p = _pair_lanes(jnp.broadcast_to(sinks, (ATT_HEADS, 1, CHUNK)))
        sink_s = _pair_rows(jnp.broadcast_to(sinks, (ATT_HEADS, dec_seq, 1)))
        final = l == depth - 1

        qa, kx, va, qd, kg, vr, sg, kvt = _inproj_call(xp, mods_p, g_mix3, w_in_b, l, rot_p, decay_p, keep_p)
        to_round = [(w_out, l), (w_up, l), (w_down, l)] + ([] if final else [(w_in, l + 1)])
        mixed, st, rounded = _mix_prompt_call(qa, kx, va, qd, kg, vr, sg, bias_p, sink_p, cdec_p, to_round)
        w_out_b, w_up_b, w_down_b = rounded[:3]
        w_in_next = None if final else rounded[3]
        xp = _dense_call(xp, mixed, mods_p, g_mlp3, g_final, w_out_b, w_up_b, w_down_b, l, final)
        pk.append(kvt[:, :, :ATT_KV_W].reshape(bsz, keep_p, ATT_KV_HEADS, ATT_HEAD_DIM))
        pv.append(kvt[:, :, ATT_KV_W:].reshape(bsz, keep_p, ATT_KV_HEADS, ATT_HEAD_DIM))
        pr.append(st)

        qa, _, _, qd, kg, vr, sg, kvn = _inproj_call(xs, mods_s, g_mix3, w_in_b, l, rot_s, decay_s, n_rows)
        per_seq = lambda a: a.reshape(dbsz, dec_seq, a.shape[-1])
        mixed, stacks = _mix_sample_call(per_seq(qa), per_seq(kvn), cache_k, cache_v, per_seq(qd), per_seq(kg),
                                         per_seq(vr), per_seq(sg), state_ret, l, stacks, bias_s, sink_s, cdec_s)
        xs = _dense_call(xs, mixed.reshape(1, n_rows, d), mods_s, g_mlp3, g_final, w_out_b, w_up_b, w_down_b, l,
                         final)
        w_in_b = w_in_next

    sr, sk, sv = stacks
    win_shape = (depth, dbsz, n_cache, ATT_KV_HEADS, ATT_HEAD_DIM)
    return (xp, xs.reshape(dbsz, dec_seq, d), jnp.stack(pk), jnp.stack(pv), jnp.stack(pr),
            sk.reshape(win_shape), sv.reshape(win_shape), sr)
```

```python
import functools
import math

import jax
import jax.numpy as jnp
from jax import lax
from jax.experimental import pallas as pl
from jax.experimental.pallas import tpu as pltpu

F32 = jnp.float32
BF16 = jnp.bfloat16

D_MODEL = 1024
CHUNK = 64
ATT_HEADS = 8
ATT_KV_HEADS = 2
ATT_HEAD_DIM = 64
ATT_Q_W = ATT_HEADS * ATT_HEAD_DIM
ATT_KV_W = ATT_KV_HEADS * ATT_HEAD_DIM
ATT_SCALE = ATT_HEAD_DIM ** -0.5
WINDOW = 128
BAND_CHUNKS = -(-WINDOW // CHUNK)
BAND = (BAND_CHUNKS + 1) * CHUNK
REL_BUCKETS = 32
REL_MAX_DIST = 128
RET_HEADS = 4
RET_KEY_DIM = 128
RET_VAL_DIM = 128
RET_W = RET_HEADS * RET_KEY_DIM
ROPE_BASE = 10000.0
IN_W = ATT_Q_W + 2 * ATT_KV_W + 4 * RET_W
D_FF = 4 * D_MODEL
N_MOD = 6
EPS = 1e-6
PAST_LEN = 4096
LOG2_E = math.log2(math.e)

LANES = 128
HALF_LANES = LANES // 2
BF16_SUBLANES = 16
ONES_ROWS = BF16_SUBLANES
VMEM_LIMIT_BYTES = 56 * 1024 * 1024

OFF_QA = 0
OFF_KV = OFF_QA + ATT_Q_W
OFF_QR = OFF_KV + 2 * ATT_KV_W
OFF_KR = OFF_QR + RET_W
OFF_VR = OFF_KR + RET_W
OFF_GR = OFF_VR + RET_W

TOKEN_BLOCK = 1024
INPROJ_SUB_BLOCK = 256
MIX_TOKEN_BLOCK = 2048
GROUP = 128
FF_BLOCK = 1024
ADA_COL_BLOCK = 1024
SAMPLE_GROUP = 4
SAMPLE_KEYS = 2 * LANES


def _params(*sem):
    return pltpu.CompilerParams(dimension_semantics=sem, vmem_limit_bytes=VMEM_LIMIT_BYTES)


def _const_spec(shape):
    zeros = (0,) * len(shape)
    return pl.BlockSpec(shape, lambda *_: zeros, pipeline_mode=pl.Buffered(1))


def _rms(x):
    return x * lax.rsqrt(jnp.mean(x * x, axis=-1, keepdims=True) + EPS)


def _dot(a, b):
    return jnp.dot(a, b, preferred_element_type=F32)


def _dot_nt(a, b):
    return lax.dot_general(a, b, (((1,), (1,)), ((), ())), preferred_element_type=F32)


def _dot_tn(a, b):
    return lax.dot_general(a, b, (((0,), (0,)), ((), ())), preferred_element_type=F32)


def _ada_kernel(c_ref, w_ref, b_ref, o_ref):
    a = jax.nn.silu(c_ref[...]).astype(BF16)
    o_ref[0] = _dot(a, w_ref[0].astype(BF16)) + b_ref[0]


def _ada_call(c_all, w_ada, b_ada):
    depth, d, n = w_ada.shape
    rows = c_all.shape[0]
    return pl.pallas_call(
        _ada_kernel,
        grid=(depth, n // ADA_COL_BLOCK),
        in_specs=[
            pl.BlockSpec((rows, d), lambda l, j: (0, 0)),
            pl.BlockSpec((1, d, ADA_COL_BLOCK), lambda l, j: (l, 0, j)),
            pl.BlockSpec((1, 1, ADA_COL_BLOCK), lambda l, j: (l, 0, j)),
        ],
        out_specs=pl.BlockSpec((1, rows, ADA_COL_BLOCK), lambda l, j: (l, 0, j)),
        out_shape=jax.ShapeDtypeStruct((depth, rows, n), F32),
        compiler_params=_params("arbitrary", "arbitrary"),
        name="ada",
    )(c_all, w_ada, b_ada.reshape(depth, 1, n))


def _bias_kernel(tab_ref, idx_ref, o_ref):
    h = pl.program_id(0)
    idx = idx_ref[...]
    acc = jnp.zeros(idx.shape, F32)
    for b in range(REL_BUCKETS):
        acc = jnp.where(idx == b, tab_ref[b, h], acc)
    o_ref[0] = acc


def _bias_call(rel_bias, bucket):
    q, j = bucket.shape
    return pl.pallas_call(
        _bias_kernel,
        grid=(ATT_HEADS,),
        in_specs=[
            pl.BlockSpec(memory_space=pltpu.SMEM),
            pl.BlockSpec((q, j), lambda h: (0, 0)),
        ],
        out_specs=pl.BlockSpec((1, q, j), lambda h: (h, 0, 0)),
        out_shape=jax.ShapeDtypeStruct((ATT_HEADS, q, j), F32),
        compiler_params=_params("arbitrary"),
        name="rel_bias",
    )(rel_bias, bucket)


def _rel_bucket(rel):
    nb = REL_BUCKETS // 2
    n = -rel
    ret = jnp.where(n < 0, nb, 0)
    n = jnp.abs(n)
    max_exact = nb // 2
    nf = jnp.maximum(n, 1).astype(F32)
    large = max_exact + (jnp.log(nf / max_exact) / math.log(REL_MAX_DIST / max_exact)
                         * (nb - max_exact)).astype(jnp.int32)
    large = jnp.minimum(large, nb - 1)
    return (ret + jnp.where(n < max_exact, n, large)).astype(jnp.int32)


def _pair_rows(per_head):
    out = []
    for k in range(ATT_KV_HEADS):
        for e in range(2):
            out.append(jnp.concatenate([per_head[4 * k + e], per_head[4 * k + 2 + e]], axis=0))
    return jnp.stack(out)


def _pair_lanes(per_head):
    out = []
    for k in range(ATT_KV_HEADS):
        for e in range(2):
            out.append(jnp.concatenate([per_head[4 * k + e], per_head[4 * k + 2 + e]], axis=-1))
    return jnp.stack(out)


def _banded_bias(per_head):
    base = _pair_lanes(jnp.swapaxes(per_head, 1, 2))
    key = jnp.arange(BAND)[None, :, None]
    return jnp.stack([jnp.where(key < (BAND_CHUNKS - v) * CHUNK, -jnp.inf, base)
                      for v in range(BAND_CHUNKS + 1)])


def _rotary(u, cos2, sin2):
    parts = []
    for h in range(RET_HEADS):
        x = u[:, h * RET_KEY_DIM:(h + 1) * RET_KEY_DIM]
        parts.append(x * cos2 + pltpu.roll(x, HALF_LANES, 1) * sin2)
    return jnp.concatenate(parts, axis=1)


def _inproj_kernel(x_ref, sh_ref, sc_ref, g_ref, w_ref, ac_ref, as_ref, bc_ref, bs_ref, bcn_ref, bsn_ref,
                   qdec_ref, kdec_ref,
                   qa_ref, kx_ref, va_ref, qd_ref, kg_ref, vr_ref, sg_ref, kvt_ref, *, tail):
    i = pl.program_id(1)
    t = x_ref.shape[1]
    sub = min(INPROJ_SUB_BLOCK, t)
    for r0 in range(0, t, sub):
        rows = slice(r0, r0 + sub)
        per_row = lambda ref: ref[rows] if ref.shape[0] == t else ref[...]
        x = x_ref[0, rows]
        h = (_rms(x) * g_ref[...]) * (1.0 + per_row(sc_ref)) + per_row(sh_ref)
        hb = h.astype(BF16)
        cos2 = ac_ref[...] * bc_ref[rows] - as_ref[...] * bs_ref[rows]
        sin2 = as_ref[...] * bcn_ref[rows] + ac_ref[...] * bsn_ref[rows]

        qa_ref[0, rows] = (_dot(hb, w_ref[:, OFF_QA:OFF_KV]) * (ATT_SCALE * LOG2_E)).astype(BF16)
        kv = _dot(hb, w_ref[:, OFF_KV:OFF_QR])
        kx_ref[0, rows] = jnp.concatenate(_lane_split(kv[:, :ATT_KV_W]), axis=1).astype(BF16)
        va_ref[0, rows] = kv[:, ATT_KV_W:].astype(BF16)
        qd_ref[0, rows] = (_rotary(_dot(hb, w_ref[:, OFF_QR:OFF_KR]), cos2, sin2) * qdec_ref[rows]).astype(BF16)
        kg_ref[0, rows] = (_rotary(_dot(hb, w_ref[:, OFF_KR:OFF_VR]), cos2, sin2) * kdec_ref[rows]).astype(BF16)
        vr_ref[0, rows] = _dot(hb, w_ref[:, OFF_VR:OFF_GR]).astype(BF16)
        sg_ref[0, rows] = jax.nn.silu(_dot(hb, w_ref[:, OFF_GR:IN_W])).astype(BF16)

        lo = max(r0, t - tail)
        if lo < r0 + sub:
            @pl.when(i == pl.num_programs(1) - 1)
            def _(kv=kv, lo=lo, r0=r0):
                kvt_ref[0, lo - (t - tail):r0 + sub - (t - tail)] = kv[lo - r0:]


MOD_SHIFT1, MOD_SCALE1, MOD_GATE1, MOD_SHIFT2, MOD_SCALE2, MOD_GATE2 = range(N_MOD)


def _mod_specs(mods, layer, which, t):
    if mods.ndim == 4:
        return [pl.BlockSpec((None, None, 1, D_MODEL), lambda b, i, m=m: (layer, b, 0, m)) for m in which]
    return [pl.BlockSpec((None, t, D_MODEL), lambda b, i, m=m: (layer, i, m)) for m in which]


def _layer_spec(arr, layer):
    tail_zeros = (0,) * (arr.ndim - 1)
    return pl.BlockSpec((None,) + arr.shape[1:], lambda *_: (layer,) + tail_zeros, pipeline_mode=pl.Buffered(1))


def _inproj_call(x, mods, gains, w_in, layer, rot, decay, tail):
    bsz, seq, d = x.shape
    t = min(TOKEN_BLOCK, seq)
    nb = seq // t
    base_cos, base_sin, off = rot
    assert base_cos.shape == (nb, 1, LANES) and off[0].shape == (t, LANES)
    assert all(a.shape == (t, RET_W) for a in decay)
    tok = lambda w: pl.BlockSpec((1, t, w), lambda b, i: (b, i, 0))
    out_bf = lambda w: jax.ShapeDtypeStruct((bsz, seq, w), BF16)
    base = pl.BlockSpec((None, 1, LANES), lambda b, i: (i, 0, 0))
    return pl.pallas_call(
        functools.partial(_inproj_kernel, tail=tail),
        grid=(bsz, nb),
        in_specs=[
            tok(d), *_mod_specs(mods, layer, (MOD_SHIFT1, MOD_SCALE1), t),
            _layer_spec(gains, layer), _const_spec(w_in.shape),
            base, base, *[_const_spec((t, LANES))] * 4, *[_const_spec((t, RET_W))] * 2,
        ],
        out_specs=[
            tok(ATT_Q_W), tok(4 * LANES), tok(ATT_KV_W), tok(RET_W), tok(RET_W), tok(RET_W), tok(RET_W),
            pl.BlockSpec((1, tail, 2 * ATT_KV_W), lambda b, i: (b, 0, 0)),
        ],
        out_shape=[
            out_bf(ATT_Q_W), out_bf(4 * LANES), out_bf(ATT_KV_W), out_bf(RET_W), out_bf(RET_W),
            out_bf(RET_W), out_bf(RET_W),
            jax.ShapeDtypeStruct((bsz, tail, 2 * ATT_KV_W), F32),
        ],
        compiler_params=_params("arbitrary", "arbitrary"),
        name="inproj",
    )(x, mods, mods, gains, w_in, base_cos, base_sin, *off, *decay)


def _lane_split(x):
    lo = lax.broadcasted_iota(jnp.int32, x.shape, 1) < HALF_LANES
    sw = pltpu.roll(x, HALF_LANES, 1)
    zero = jnp.zeros_like(x)
    return (jnp.where(lo, x, zero), jnp.where(lo, zero, sw), jnp.where(lo, sw, zero), jnp.where(lo, zero, x))


def _sink_softmax2(s, sink):
    m = jnp.maximum(jnp.max(s, axis=-1, keepdims=True), sink)
    e = jnp.exp2(s - m)
    den = jnp.sum(e, axis=-1, keepdims=True) + jnp.exp2(sink - m)
    return e / den


def _mix_prompt_kernel(cdec_ref, qa_ref, ka_ref, va_ref, qd_ref, kg_ref, vr_ref, sg_ref, bias_ref, sink_ref,
                       *rest, t, n_cast):
    cast_src = rest[:n_cast]
    mixed_ref, state_ref = rest[n_cast:n_cast + 2]
    cast_dst = rest[n_cast + 2:2 * n_cast + 2]
    kx_ref, vt_ref, s_ref = rest[2 * n_cast + 2:]
    for src, dst in zip(cast_src, cast_dst):
        dst[...] = src[...].astype(BF16)

    i = pl.program_id(1)
    n_groups = t // GROUP
    hist = BAND - CHUNK

    hd = ATT_HEAD_DIM

    @pl.when(i == 0)
    def _():
        kx_ref[:, 0:hist, :] = jnp.zeros((4, hist, LANES), BF16)
        vt_ref[0, :, 0:hd, 0:GROUP] = jnp.zeros((ATT_KV_HEADS, hd, GROUP), BF16)
        vt_ref[:, :, hd:, :] = jnp.ones((n_groups, ATT_KV_HEADS, ONES_ROWS, 2 * GROUP), BF16)
        s_ref[...] = jnp.zeros(s_ref.shape, F32)

    @pl.when(i > 0)
    def _():
        kx_ref[:, 0:hist, :] = kx_ref[:, t:t + hist, :]
        vt_ref[0, :, 0:hd, 0:GROUP] = vt_ref[n_groups - 1, :, 0:hd, GROUP:]

    for n in range(4):
        kx_ref[n, hist:hist + t, :] = ka_ref[0, :, n * LANES:(n + 1) * LANES]
    for m in range(n_groups):
        tile = va_ref[0, m * GROUP:(m + 1) * GROUP, :].astype(F32).T.astype(BF16)
        for k in range(ATT_KV_HEADS):
            vt_ref[m, k, 0:hd, GROUP:] = tile[k * hd:(k + 1) * hd]
            if m + 1 < n_groups:
                vt_ref[m + 1, k, 0:hd, 0:GROUP] = tile[k * hd:(k + 1) * hd]

    no_keys = jnp.zeros((CHUNK, LANES), BF16)
    causal = (lax.broadcasted_iota(jnp.int32, (GROUP, GROUP), 0)
              >= lax.broadcasted_iota(jnp.int32, (GROUP, GROUP), 1))
    heads = [slice(h * LANES, (h + 1) * LANES) for h in range(RET_HEADS)]
    att = [(j, k, e) for j in range(GROUP // CHUNK) for k in range(ATT_KV_HEADS) for e in range(2)]

    def lead(g):
        grows = slice(g * GROUP, (g + 1) * GROUP)
        raw = {}
        for j in range(GROUP // CHUNK):
            r0 = g * GROUP + j * CHUNK
            for k in range(ATT_KV_HEADS):
                c0 = 2 * LANES * k
                q2 = jnp.concatenate([qa_ref[0, r0:r0 + CHUNK, c0:c0 + LANES],
                                      qa_ref[0, r0:r0 + CHUNK, c0 + LANES:c0 + 2 * LANES]], axis=0)
                keys = jnp.concatenate([kx_ref[2 * k, r0:r0 + BAND, :], kx_ref[2 * k + 1, r0:r0 + BAND, :]], axis=0)
                s2 = _dot_nt(keys, q2)
                raw[j, k, 0], raw[j, k, 1] = s2[:BAND], s2[BAND:]
        states = [s_ref[h] for h in range(RET_HEADS)]
        ret_q = [qd_ref[0, grows, c] for c in heads]
        ret_k = [kg_ref[0, grows, c] for c in heads]
        ret_v = [vr_ref[0, grows, c] for c in heads]
        ret_scores = [_dot_nt(ret_q[h], ret_k[h]) for h in range(RET_HEADS)]
        for h in range(RET_HEADS):
            s_ref[h] = cdec_ref[h] * (states[h] + _dot_tn(ret_k[h], ret_v[h]))
        return raw, states, ret_q, ret_v, ret_scores

    def finish(g, lead_out):
        raw, states, ret_q, ret_v, ret_scores = lead_out
        grows = slice(g * GROUP, (g + 1) * GROUP)
        chunk_rows = lambda j: slice(g * GROUP + j * CHUNK, g * GROUP + (j + 1) * CHUNK)
        probs, sink_terms = {}, {}
        for j, k, e in att:
            n = 2 * k + e
            variant = jnp.minimum(i * (t // CHUNK) + 2 * g + j, BAND_CHUNKS)
            s = raw[j, k, e] + bias_ref[variant, n]
            sink = sink_ref[n]
            m = jnp.maximum(jnp.max(s, axis=0, keepdims=True), sink)
            probs[j, k, e] = jnp.exp2(s - m).astype(BF16)
            sink_terms[j, k, e] = jnp.exp2(sink - m)
        ret_p = [jnp.where(causal, ret_scores[h], 0.0).astype(BF16) for h in range(RET_HEADS)]

        out_t = {}
        for j, k, e in att:
            frame = [probs[j, k, e], no_keys] if j == 0 else [no_keys, probs[j, k, e]]
            out_t[j, k, e] = _dot(vt_ref[g, k], jnp.concatenate(frame, axis=0))
        ret_out = [_dot(jnp.concatenate([ret_p[h], ret_q[h]], axis=1),
                        jnp.concatenate([ret_v[h], states[h].astype(BF16)], axis=0)) for h in range(RET_HEADS)]

        for j in range(GROUP // CHUNK):
            for k in range(ATT_KV_HEADS):
                c0 = 2 * LANES * k
                halves = []
                for e in range(2):
                    o_t = out_t[j, k, e]
                    den = o_t[ATT_HEAD_DIM:ATT_HEAD_DIM + 1] + sink_terms[j, k, e]
                    halves.append(o_t[:ATT_HEAD_DIM] * (1.0 / den))
                o = jnp.concatenate(halves, axis=0).T
                mixed_ref[0, chunk_rows(j), c0:c0 + LANES] = o[:CHUNK].astype(BF16)
                mixed_ref[0, chunk_rows(j), c0 + LANES:c0 + 2 * LANES] = o[CHUNK:].astype(BF16)
        for h in range(RET_HEADS):
            out = _rms(ret_out[h]) * sg_ref[0, grows, heads[h]].astype(F32)
            mixed_ref[0, grows, ATT_Q_W + h * LANES:ATT_Q_W + (h + 1) * LANES] = out.astype(BF16)

    ahead = lead(0)
    for g in range(n_groups):
        current, ahead = ahead, (lead(g + 1) if g + 1 < n_groups else None)
        finish(g, current)

    @pl.when(i == pl.num_programs(1) - 1)
    def _():
        state_ref[0] = s_ref[...]


def _mix_prompt_call(qa, kx, va, qd, kg, vr, sg, bias4, sink4, cdec, weights):
    bsz, seq, _ = qa.shape
    t = min(MIX_TOKEN_BLOCK, seq)
    nb = seq // t
    hist = BAND - CHUNK
    assert hist == GROUP and t % GROUP == 0 and seq % t == 0
    tok = lambda w: pl.BlockSpec((1, t, w), lambda b, i: (b, i, 0))
    steps = bsz * nb
    slabs, slab_in, slab_out, slab_shape = [], [], [], []
    for stack, layer in weights:
        depth, rows, cols = stack.shape
        assert rows % (steps * BF16_SUBLANES) == 0
        slabs.append(stack.reshape(depth, steps, rows // steps, cols))
        slab_in.append(pl.BlockSpec((None, None, rows // steps, cols), lambda b, i, layer=layer: (layer, b * nb + i, 0, 0)))
        slab_out.append(pl.BlockSpec((None, rows // steps, cols), lambda b, i: (b * nb + i, 0, 0)))
        slab_shape.append(jax.ShapeDtypeStruct((steps, rows // steps, cols), BF16))
    outs = pl.pallas_call(
        functools.partial(_mix_prompt_kernel, t=t, n_cast=len(weights)),
        grid=(bsz, nb),
        in_specs=[
            pl.BlockSpec(memory_space=pltpu.SMEM),
            tok(ATT_Q_W), tok(4 * LANES), tok(ATT_KV_W), tok(RET_W), tok(RET_W), tok(RET_W), tok(RET_W),
            _const_spec(bias4.shape), _const_spec(sink4.shape),
        ] + slab_in,
        out_specs=[
            tok(D_MODEL),
            pl.BlockSpec((1, RET_HEADS, RET_KEY_DIM, RET_VAL_DIM), lambda b, i: (b, 0, 0, 0)),
        ] + slab_out,
        out_shape=[
            jax.ShapeDtypeStruct((bsz, seq, D_MODEL), BF16),
            jax.ShapeDtypeStruct((bsz, RET_HEADS, RET_KEY_DIM, RET_VAL_DIM), F32),
        ] + slab_shape,
        scratch_shapes=[
            pltpu.VMEM((4, hist + t, LANES), BF16),
            pltpu.VMEM((t // GROUP, ATT_KV_HEADS, ATT_HEAD_DIM + ONES_ROWS, 2 * GROUP), BF16),
            pltpu.VMEM((RET_HEADS, RET_KEY_DIM, RET_VAL_DIM), F32),
        ],
        compiler_params=_params("arbitrary", "arbitrary"),
        name="mix_prompt",
    )(cdec, qa, kx, va, qd, kg, vr, sg, bias4, sink4, *slabs)
    rounded = [o.reshape(stack.shape[1:]) for o, (stack, _) in zip(outs[2:], weights)]
    return outs[0], outs[1], rounded


def _mix_sample_kernel(cdec_ref, qa_ref, kvn_ref, ck_ref, cv_ref, qd_ref, kg_ref, vr_ref, sg_ref, s0_ref,
                       bias_ref, sink_ref, *rest, n_new, n_cache):
    mixed_ref, *stack_refs = rest[-4:]
    done = 0
    if len(rest) > 4:
        done = rest[0].shape[0]
        for prev_ref, out_ref in zip(rest[:3], stack_refs):
            out_ref[0:done] = prev_ref[...]
    state_ref, wk_ref, wv_ref = (r.at[done] for r in stack_refs)
    pad = jnp.zeros((SAMPLE_KEYS - n_cache - n_new, LANES), F32)
    causal = (lax.broadcasted_iota(jnp.int32, (n_new, n_new), 0)
              >= lax.broadcasted_iota(jnp.int32, (n_new, n_new), 1))
    seqs = range(SAMPLE_GROUP)
    heads = [slice(h * LANES, (h + 1) * LANES) for h in range(RET_HEADS)]
    att = [(g, k, e) for g in seqs for k in range(ATT_KV_HEADS) for e in range(2)]
    ret = [(g, h) for g in seqs for h in range(RET_HEADS)]

    v4, raw = {}, {}
    for g in seqs:
        kvn = kvn_ref[g]
        k_all = jnp.concatenate([ck_ref[g], kvn[:, :ATT_KV_W], pad], axis=0)
        v_all = jnp.concatenate([cv_ref[g], kvn[:, ATT_KV_W:], pad], axis=0)
        wk_ref[g] = k_all[n_new:n_new + n_cache]
        wv_ref[g] = v_all[n_new:n_new + n_cache]
        k4 = [p.astype(BF16) for p in _lane_split(k_all)]
        v4[g] = [p.astype(BF16) for p in _lane_split(v_all)]
        for k in range(ATT_KV_HEADS):
            c0 = 2 * LANES * k
            q2 = jnp.concatenate([qa_ref[g, :, c0:c0 + LANES], qa_ref[g, :, c0 + LANES:c0 + 2 * LANES]], axis=0)
            for e in range(2):
                raw[g, k, e] = _dot_nt(q2, k4[2 * k + e])
    qd = {(g, h): qd_ref[g, :, heads[h]] for g, h in ret}
    kg = {(g, h): kg_ref[g, :, heads[h]] for g, h in ret}
    vr = {(g, h): vr_ref[g, :, heads[h]] for g, h in ret}
    state = {(g, h): s0_ref[g, h] for g, h in ret}
    ret_scores = {gh: _dot_nt(qd[gh], kg[gh]) for gh in ret}
    ret_inter = {gh: _dot(qd[gh], state[gh].astype(BF16)) for gh in ret}
    for g, h in ret:
        state_ref[g, h] = cdec_ref[h] * (state[g, h] + _dot_tn(kg[g, h], vr[g, h]))

    probs = {(g, k, e): _sink_softmax2(raw[g, k, e] + bias_ref[2 * k + e], sink_ref[2 * k + e]).astype(BF16)
             for g, k, e in att}
    ret_p = {gh: jnp.where(causal, ret_scores[gh], 0.0).astype(BF16) for gh in ret}

    out = {(g, k, e): _dot(probs[g, k, e], v4[g][2 * k + e]) for g, k, e in att}
    ret_intra = {gh: _dot(ret_p[gh], vr[gh]) for gh in ret}

    for g in seqs:
        for k in range(ATT_KV_HEADS):
            c0 = 2 * LANES * k
            o = out[g, k, 0] + out[g, k, 1]
            mixed_ref[g, :, c0:c0 + LANES] = o[:n_new].astype(BF16)
            mixed_ref[g, :, c0 + LANES:c0 + 2 * LANES] = o[n_new:].astype(BF16)
        for h in range(RET_HEADS):
            ro = _rms(ret_intra[g, h] + ret_inter[g, h]) * sg_ref[g, :, heads[h]].astype(F32)
            mixed_ref[g, :, ATT_Q_W + h * LANES:ATT_Q_W + (h + 1) * LANES] = ro.astype(BF16)


def _mix_sample_call(qa, kvn, cache_k, cache_v, qd, kg, vr, sg, state, layer, stacks, bias4, sink4, cdec):
    dbsz, n_new, _ = qa.shape
    n_cache = cache_k.shape[2]
    grp = lambda *tail: pl.BlockSpec((SAMPLE_GROUP,) + tail, lambda b: (b,) + (0,) * len(tail))
    lay = lambda *tail: pl.BlockSpec((None, SAMPLE_GROUP) + tail, lambda b: (layer, b) + (0,) * len(tail))
    stk = lambda n, *tail: pl.BlockSpec((n, SAMPLE_GROUP) + tail, lambda b: (0, b) + (0,) * len(tail))
    stack_tails = [(RET_HEADS, RET_KEY_DIM, RET_VAL_DIM), (n_cache, ATT_KV_W), (n_cache, ATT_KV_W)]
    operands = [cdec, qa, kvn, cache_k, cache_v, qd, kg, vr, sg, state, bias4, sink4]
    prev_specs = []
    if stacks is not None:
        assert all(s.shape[0] == layer for s in stacks)
        operands += list(stacks)
        prev_specs = [stk(layer, *tail) for tail in stack_tails]
    outs = pl.pallas_call(
        functools.partial(_mix_sample_kernel, n_new=n_new, n_cache=n_cache),
        grid=(dbsz // SAMPLE_GROUP,),
        in_specs=[
            pl.BlockSpec(memory_space=pltpu.SMEM),
            grp(n_new, ATT_Q_W), grp(n_new, 2 * ATT_KV_W), lay(n_cache, ATT_KV_W), lay(n_cache, ATT_KV_W),
            grp(n_new, RET_W), grp(n_new, RET_W), grp(n_new, RET_W), grp(n_new, RET_W),
            lay(RET_HEADS, RET_KEY_DIM, RET_VAL_DIM),
            _const_spec(bias4.shape), _const_spec(sink4.shape),
        ] + prev_specs,
        out_specs=[grp(n_new, D_MODEL)] + [stk(layer + 1, *tail) for tail in stack_tails],
        out_shape=[jax.ShapeDtypeStruct((dbsz, n_new, D_MODEL), BF16)]
        + [jax.ShapeDtypeStruct((layer + 1, dbsz) + tail, F32) for tail in stack_tails],
        compiler_params=_params("arbitrary"),
        name="mix_sample",
    )(*operands)
    return outs[0], outs[1:]


def _dense_kernel(x_ref, mixed_ref, ga1_ref, sh2_ref, sc2_ref, ga2_ref, g_ref, gf_ref,
                  wo_ref, wu_ref, wd_ref, o_ref, *, final):
    x1 = x_ref[0] + ga1_ref[...] * _dot(mixed_ref[0], wo_ref[...])
    h = ((_rms(x1) * g_ref[...]) * (1.0 + sc2_ref[...]) + sh2_ref[...]).astype(BF16)
    acc = jnp.zeros(x1.shape, F32)
    for j in range(D_FF // FF_BLOCK):
        u = _dot(h, wu_ref[:, j * FF_BLOCK:(j + 1) * FF_BLOCK])
        u = jnp.square(jnp.maximum(u, 0.0)).astype(BF16)
        acc = acc + _dot(u, wd_ref[j * FF_BLOCK:(j + 1) * FF_BLOCK, :])
    x2 = x1 + ga2_ref[...] * acc
    o_ref[0] = _rms(x2) * gf_ref[...] if final else x2


def _dense_call(x, mixed, mods, gains, g_final, w_out, w_up, w_down, layer, final):
    bsz, seq, d = x.shape
    t = min(TOKEN_BLOCK, seq)
    tok = lambda w: pl.BlockSpec((1, t, w), lambda b, i: (b, i, 0))
    return pl.pallas_call(
        functools.partial(_dense_kernel, final=final),
        grid=(bsz, seq // t),
        in_specs=[
            tok(d), tok(d), *_mod_specs(mods, layer, (MOD_GATE1, MOD_SHIFT2, MOD_SCALE2, MOD_GATE2), t),
            _layer_spec(gains, layer), _const_spec((1, d)),
            _const_spec(w_out.shape), _const_spec(w_up.shape), _const_spec(w_down.shape),
        ],
        out_specs=tok(d),
        out_shape=jax.ShapeDtypeStruct((bsz, seq, d), F32),
        compiler_params=_params("arbitrary", "arbitrary"),
        name="dense",
    )(x, mixed, mods, mods, mods, mods, gains, g_final.reshape(1, d), w_out, w_up, w_down)


def _rotary_tables(base_pos, off_pos):
    half = RET_KEY_DIM // 2
    inv = 1.0 / (ROPE_BASE ** (jnp.arange(half, dtype=F32) / half))
    sign = jnp.concatenate([-jnp.ones((half,), F32), jnp.ones((half,), F32)])

    def both(pos):
        ang = pos.astype(F32)[:, None] * inv[None, :]
        cos, sin = jnp.cos(ang), jnp.sin(ang)
        return jnp.concatenate([cos, cos], axis=1), jnp.concatenate([sin, sin], axis=1)

    base_cos, base_sin = both(base_pos)
    off_cos, off_sin = both(off_pos)
    return base_cos[:, None, :], base_sin[:, None, :], (off_cos, off_sin, sign * off_cos, sign * off_sin)


def _retention_tables(n, rows):
    log_g = jnp.log(1.0 - 2.0 ** (-5.0 - jnp.arange(RET_HEADS, dtype=F32)))
    steps = (jnp.arange(rows) % n).astype(F32) + 1.0
    expo = steps[:, None] * jnp.repeat(log_g, RET_KEY_DIM)[None, :]
    return (jnp.exp(expo), jnp.exp(-expo) * (RET_KEY_DIM ** -0.5)), jnp.exp(n * log_g)


def kernel(x_prompt, x_sample, c_prompt, c_sample, cache_win_k, cache_win_v, state_ret, g_mix, g_mlp, w_ada,
           b_ada, w_in, w_out, att_sinks, rel_bias, w_up, w_down, g_final):
    depth = w_in.shape[0]
    bsz, seq, d = x_prompt.shape
    dbsz, dec_seq, _ = x_sample.shape
    n_cache = cache_win_k.shape[2]
    n_rows = dbsz * dec_seq
    keep_p = min(WINDOW, seq)

    w_in_b = w_in[0].astype(BF16)
    g_mix3 = g_mix.reshape(depth, 1, d)
    g_mlp3 = g_mlp.reshape(depth, 1, d)
    cache_k = cache_win_k.reshape(depth, dbsz, n_cache, ATT_KV_W)
    cache_v = cache_win_v.reshape(depth, dbsz, n_cache, ATT_KV_W)

    c_all = jnp.concatenate([c_prompt, c_sample], axis=0)
    c_all = jnp.pad(c_all, ((0, -c_all.shape[0] % 8), (0, 0)))
    mods = _ada_call(c_all, w_ada, b_ada)
    mods_p = mods[:, :bsz, None, :]
    mods_s = jnp.repeat(mods[:, bsz:bsz + dbsz], dec_seq, axis=1)

    t_p = min(TOKEN_BLOCK, seq)
    rot_p = _rotary_tables(jnp.arange(seq // t_p) * t_p, jnp.arange(t_p))
    rot_s = _rotary_tables(jnp.full((1,), PAST_LEN), jnp.tile(jnp.arange(dec_seq), dbsz))
    decay_p, cdec_p = _retention_tables(GROUP, t_p)
    decay_s, cdec_s = _retention_tables(dec_seq, n_rows)

    rel_p = (jnp.arange(BAND) - BAND_CHUNKS * CHUNK)[None, :] - jnp.arange(CHUNK)[:, None]
    bias_p = _banded_bias(_bias_call(rel_bias, _rel_bucket(rel_p)) * LOG2_E)
    kpos = PAST_LEN - n_cache + jnp.arange(n_cache + dec_seq)
    qpos = PAST_LEN + jnp.arange(dec_seq)
    bias_s = _bias_call(rel_bias, _rel_bucket(kpos[None, :] - qpos[:, None])) * LOG2_E
    bias_s = jnp.pad(bias_s, ((0, 0), (0, 0), (0, SAMPLE_KEYS - n_cache - dec_seq)), constant_values=-jnp.inf)
    bias_s = _pair_rows(bias_s)

    xp = x_prompt
    xs = x_sample.reshape(1, n_rows, d)
    pk, pv, pr = [], [], []
    stacks = None
    for l in range(depth):
        sinks = att_sinks[l].astype(F32)[:, None, None] * LOG2_E
        sink_p = _pair_lanes(jnp.broadcast_to(sinks, (ATT_HEADS, 1, CHUNK)))
        sink_s = _pair_rows(jnp.broadcast_to(sinks, (ATT_HEADS, dec_seq, 1)))
        final = l == depth - 1

        qa, kx, va, qd, kg, vr, sg, kvt = _inproj_call(xp, mods_p, g_mix3, w_in_b, l, rot_p, decay_p, keep_p)
        to_round = [(w_out, l), (w_up, l), (w_down, l)] + ([] if final else [(w_in, l + 1)])
        mixed, st, rounded = _mix_prompt_call(qa, kx, va, qd, kg, vr, sg, bias_p, sink_p, cdec_p, to_round)
        w_out_b, w_up_b, w_down_b = rounded[:3]
        w_in_next = None if final else rounded[3]
        xp = _dense_call(xp, mixed, mods_p, g_mlp3, g_final, w_out_b, w_up_b, w_down_b, l, final)
        pk.append(kvt[:, :, :ATT_KV_W].reshape(bsz, keep_p, ATT_KV_HEADS, ATT_HEAD_DIM))
        pv.append(kvt[:, :, ATT_KV_W:].reshape(bsz, keep_p, ATT_KV_HEADS, ATT_HEAD_DIM))
        pr.append(st)

        qa, _, _, qd, kg, vr, sg, kvn = _inproj_call(xs, mods_s, g_mix3, w_in_b, l, rot_s, decay_s, n_rows)
        per_seq = lambda a: a.reshape(dbsz, dec_seq, a.shape[-1])
        mixed, stacks = _mix_sample_call(per_seq(qa), per_seq(kvn), cache_k, cache_v, per_seq(qd), per_seq(kg),
                                         per_seq(vr), per_seq(sg), state_ret, l, stacks, bias_s, sink_s, cdec_s)
        xs = _dense_call(xs, mixed.reshape(1, n_rows, d), mods_s, g_mlp3, g_final, w_out_b, w_up_b, w_down_b, l,
                         final)
        w_in_b = w_in_next

    sr, sk, sv = stacks
    win_shape = (depth, dbsz, n_cache, ATT_KV_HEADS, ATT_HEAD_DIM)
    return (xp, xs.reshape(dbsz, dec_seq, d), jnp.stack(pk), jnp.stack(pv), jnp.stack(pr),
            sk.reshape(win_shape), sv.reshape(win_shape), sr)
```

```python
import functools
import math

import jax
import jax.numpy as jnp
from jax import lax
from jax.experimental import pallas as pl
from jax.experimental.pallas import tpu as pltpu

F32 = jnp.float32
BF16 = jnp.bfloat16

D_MODEL = 1024
CHUNK = 64
ATT_HEADS = 8
ATT_KV_HEADS = 2
ATT_HEAD_DIM = 64
ATT_Q_W = ATT_HEADS * ATT_HEAD_DIM
ATT_KV_W = ATT_KV_HEADS * ATT_HEAD_DIM
ATT_SCALE = ATT_HEAD_DIM ** -0.5
WINDOW = 128
BAND_CHUNKS = -(-WINDOW // CHUNK)
BAND = (BAND_CHUNKS + 1) * CHUNK
REL_BUCKETS = 32
REL_MAX_DIST = 128
RET_HEADS = 4
RET_KEY_DIM = 128
RET_VAL_DIM = 128
RET_W = RET_HEADS * RET_KEY_DIM
ROPE_BASE = 10000.0
IN_W = ATT_Q_W + 2 * ATT_KV_W + 4 * RET_W
D_FF = 4 * D_MODEL
N_MOD = 6
EPS = 1e-6
PAST_LEN = 4096
LOG2_E = math.log2(math.e)

LANES = 128
HALF_LANES = LANES // 2
BF16_SUBLANES = 16
ONES_ROWS = BF16_SUBLANES
VMEM_LIMIT_BYTES = 56 * 1024 * 1024

OFF_QA = 0
OFF_KV = OFF_QA + ATT_Q_W
OFF_QR = OFF_KV + 2 * ATT_KV_W
OFF_KR = OFF_QR + RET_W
OFF_VR = OFF_KR + RET_W
OFF_GR = OFF_VR + RET_W

TOKEN_BLOCK = 1024
INPROJ_SUB_BLOCK = 256
MIX_TOKEN_BLOCK = 1024
GROUP = 128
FF_BLOCK = 1024
ADA_COL_BLOCK = 1024
SAMPLE_GROUP = 4
SAMPLE_KEYS = 2 * LANES


def _params(*sem):
    return pltpu.CompilerParams(dimension_semantics=sem, vmem_limit_bytes=VMEM_LIMIT_BYTES)


def _const_spec(shape):
    zeros = (0,) * len(shape)
    return pl.BlockSpec(shape, lambda *_: zeros, pipeline_mode=pl.Buffered(1))


def _rms(x):
    return x * lax.rsqrt(jnp.mean(x * x, axis=-1, keepdims=True) + EPS)


def _dot(a, b):
    return jnp.dot(a, b, preferred_element_type=F32)


def _dot_nt(a, b):
    return lax.dot_general(a, b, (((1,), (1,)), ((), ())), preferred_element_type=F32)


def _dot_tn(a, b):
    return lax.dot_general(a, b, (((0,), (0,)), ((), ())), preferred_element_type=F32)


def _ada_kernel(c_ref, w_ref, b_ref, o_ref):
    a = jax.nn.silu(c_ref[...]).astype(BF16)
    o_ref[0] = _dot(a, w_ref[0].astype(BF16)) + b_ref[0]


def _ada_call(c_all, w_ada, b_ada):
    depth, d, n = w_ada.shape
    rows = c_all.shape[0]
    return pl.pallas_call(
        _ada_kernel,
        grid=(depth, n // ADA_COL_BLOCK),
        in_specs=[
            pl.BlockSpec((rows, d), lambda l, j: (0, 0)),
            pl.BlockSpec((1, d, ADA_COL_BLOCK), lambda l, j: (l, 0, j)),
            pl.BlockSpec((1, 1, ADA_COL_BLOCK), lambda l, j: (l, 0, j)),
        ],
        out_specs=pl.BlockSpec((1, rows, ADA_COL_BLOCK), lambda l, j: (l, 0, j)),
        out_shape=jax.ShapeDtypeStruct((depth, rows, n), F32),
        compiler_params=_params("arbitrary", "arbitrary"),
        name="ada",
    )(c_all, w_ada, b_ada.reshape(depth, 1, n))


def _bias_kernel(tab_ref, idx_ref, o_ref):
    h = pl.program_id(0)
    idx = idx_ref[...]
    acc = jnp.zeros(idx.shape, F32)
    for b in range(REL_BUCKETS):
        acc = jnp.where(idx == b, tab_ref[b, h], acc)
    o_ref[0] = acc


def _bias_call(rel_bias, bucket):
    q, j = bucket.shape
    return pl.pallas_call(
        _bias_kernel,
        grid=(ATT_HEADS,),
        in_specs=[
            pl.BlockSpec(memory_space=pltpu.SMEM),
            pl.BlockSpec((q, j), lambda h: (0, 0)),
        ],
        out_specs=pl.BlockSpec((1, q, j), lambda h: (h, 0, 0)),
        out_shape=jax.ShapeDtypeStruct((ATT_HEADS, q, j), F32),
        compiler_params=_params("arbitrary"),
        name="rel_bias",
    )(rel_bias, bucket)


def _rel_bucket(rel):
    nb = REL_BUCKETS // 2
    n = -rel
    ret = jnp.where(n < 0, nb, 0)
    n = jnp.abs(n)
    max_exact = nb // 2
    nf = jnp.maximum(n, 1).astype(F32)
    large = max_exact + (jnp.log(nf / max_exact) / math.log(REL_MAX_DIST / max_exact)
                         * (nb - max_exact)).astype(jnp.int32)
    large = jnp.minimum(large, nb - 1)
    return (ret + jnp.where(n < max_exact, n, large)).astype(jnp.int32)


def _pair_rows(per_head):
    out = []
    for k in range(ATT_KV_HEADS):
        for e in range(2):
            out.append(jnp.concatenate([per_head[4 * k + e], per_head[4 * k + 2 + e]], axis=0))
    return jnp.stack(out)


def _pair_lanes(per_head):
    out = []
    for k in range(ATT_KV_HEADS):
        for e in range(2):
            out.append(jnp.concatenate([per_head[4 * k + e], per_head[4 * k + 2 + e]], axis=-1))
    return jnp.stack(out)


def _banded_bias(per_head):
    base = _pair_lanes(jnp.swapaxes(per_head, 1, 2))
    key = jnp.arange(BAND)[None, :, None]
    return jnp.stack([jnp.where(key < (BAND_CHUNKS - v) * CHUNK, -jnp.inf, base)
                      for v in range(BAND_CHUNKS + 1)])


def _rotary(u, cos2, sin2):
    parts = []
    for h in range(RET_HEADS):
        x = u[:, h * RET_KEY_DIM:(h + 1) * RET_KEY_DIM]
        parts.append(x * cos2 + pltpu.roll(x, HALF_LANES, 1) * sin2)
    return jnp.concatenate(parts, axis=1)


def _inproj_kernel(x_ref, sh_ref, sc_ref, g_ref, w_ref, ac_ref, as_ref, bc_ref, bs_ref, bcn_ref, bsn_ref,
                   qdec_ref, kdec_ref,
                   qa_ref, kx_ref, va_ref, qd_ref, kg_ref, vr_ref, sg_ref, kvt_ref, *, tail):
    i = pl.program_id(1)
    t = x_ref.shape[1]
    sub = min(INPROJ_SUB_BLOCK, t)
    for r0 in range(0, t, sub):
        rows = slice(r0, r0 + sub)
        per_row = lambda ref: ref[rows] if ref.shape[0] == t else ref[...]
        x = x_ref[0, rows]
        h = (_rms(x) * g_ref[...]) * (1.0 + per_row(sc_ref)) + per_row(sh_ref)
        hb = h.astype(BF16)
        cos2 = ac_ref[...] * bc_ref[rows] - as_ref[...] * bs_ref[rows]
        sin2 = as_ref[...] * bcn_ref[rows] + ac_ref[...] * bsn_ref[rows]

        qa_ref[0, rows] = (_dot(hb, w_ref[:, OFF_QA:OFF_KV]) * (ATT_SCALE * LOG2_E)).astype(BF16)
        kv = _dot(hb, w_ref[:, OFF_KV:OFF_QR])
        kx_ref[0, rows] = jnp.concatenate(_lane_split(kv[:, :ATT_KV_W]), axis=1).astype(BF16)
        va_ref[0, rows] = kv[:, ATT_KV_W:].astype(BF16)
        qd_ref[0, rows] = (_rotary(_dot(hb, w_ref[:, OFF_QR:OFF_KR]), cos2, sin2) * qdec_ref[rows]).astype(BF16)
        kg_ref[0, rows] = (_rotary(_dot(hb, w_ref[:, OFF_KR:OFF_VR]), cos2, sin2) * kdec_ref[rows]).astype(BF16)
        vr_ref[0, rows] = _dot(hb, w_ref[:, OFF_VR:OFF_GR]).astype(BF16)
        sg_ref[0, rows] = jax.nn.silu(_dot(hb, w_ref[:, OFF_GR:IN_W])).astype(BF16)

        lo = max(r0, t - tail)
        if lo < r0 + sub:
            @pl.when(i == pl.num_programs(1) - 1)
            def _(kv=kv, lo=lo, r0=r0):
                kvt_ref[0, lo - (t - tail):r0 + sub - (t - tail)] = kv[lo - r0:]


MOD_SHIFT1, MOD_SCALE1, MOD_GATE1, MOD_SHIFT2, MOD_SCALE2, MOD_GATE2 = range(N_MOD)


def _mod_specs(mods, layer, which, t):
    if mods.ndim == 4:
        return [pl.BlockSpec((None, None, 1, D_MODEL), lambda b, i, m=m: (layer, b, 0, m)) for m in which]
    return [pl.BlockSpec((None, t, D_MODEL), lambda b, i, m=m: (layer, i, m)) for m in which]


def _layer_spec(arr, layer):
    tail_zeros = (0,) * (arr.ndim - 1)
    return pl.BlockSpec((None,) + arr.shape[1:], lambda *_: (layer,) + tail_zeros, pipeline_mode=pl.Buffered(1))


def _inproj_call(x, mods, gains, w_in, layer, rot, decay, tail):
    bsz, seq, d = x.shape
    t = min(TOKEN_BLOCK, seq)
    nb = seq // t
    base_cos, base_sin, off = rot
    assert base_cos.shape == (nb, 1, LANES) and off[0].shape == (t, LANES)
    assert all(a.shape == (t, RET_W) for a in decay)
    tok = lambda w: pl.BlockSpec((1, t, w), lambda b, i: (b, i, 0))
    out_bf = lambda w: jax.ShapeDtypeStruct((bsz, seq, w), BF16)
    base = pl.BlockSpec((None, 1, LANES), lambda b, i: (i, 0, 0))
    return pl.pallas_call(
        functools.partial(_inproj_kernel, tail=tail),
        grid=(bsz, nb),
        in_specs=[
            tok(d), *_mod_specs(mods, layer, (MOD_SHIFT1, MOD_SCALE1), t),
            _layer_spec(gains, layer), _const_spec(w_in.shape),
            base, base, *[_const_spec((t, LANES))] * 4, *[_const_spec((t, RET_W))] * 2,
        ],
        out_specs=[
            tok(ATT_Q_W), tok(4 * LANES), tok(ATT_KV_W), tok(RET_W), tok(RET_W), tok(RET_W), tok(RET_W),
            pl.BlockSpec((1, tail, 2 * ATT_KV_W), lambda b, i: (b, 0, 0)),
        ],
        out_shape=[
            out_bf(ATT_Q_W), out_bf(4 * LANES), out_bf(ATT_KV_W), out_bf(RET_W), out_bf(RET_W),
            out_bf(RET_W), out_bf(RET_W),
            jax.ShapeDtypeStruct((bsz, tail, 2 * ATT_KV_W), F32),
        ],
        compiler_params=_params("arbitrary", "arbitrary"),
        name="inproj",
    )(x, mods, mods, gains, w_in, base_cos, base_sin, *off, *decay)


def _lane_split(x):
    lo = lax.broadcasted_iota(jnp.int32, x.shape, 1) < HALF_LANES
    sw = pltpu.roll(x, HALF_LANES, 1)
    zero = jnp.zeros_like(x)
    return (jnp.where(lo, x, zero), jnp.where(lo, zero, sw), jnp.where(lo, sw, zero), jnp.where(lo, zero, x))


def _sink_softmax2(s, sink):
    m = jnp.maximum(jnp.max(s, axis=-1, keepdims=True), sink)
    e = jnp.exp2(s - m)
    den = jnp.sum(e, axis=-1, keepdims=True) + jnp.exp2(sink - m)
    return e / den


def _mix_prompt_kernel(cdec_ref, qa_ref, ka_ref, va_ref, qd_ref, kg_ref, vr_ref, sg_ref, bias_ref, sink_ref,
                       *rest, t, n_cast):
    cast_src = rest[:n_cast]
    mixed_ref, state_ref = rest[n_cast:n_cast + 2]
    cast_dst = rest[n_cast + 2:2 * n_cast + 2]
    kx_ref, vt_ref, s_ref = rest[2 * n_cast + 2:]
    for src, dst in zip(cast_src, cast_dst):
        dst[...] = src[...].astype(BF16)

    i = pl.program_id(1)
    n_groups = t // GROUP
    hist = BAND - CHUNK

    hd = ATT_HEAD_DIM

    @pl.when(i == 0)
    def _():
        kx_ref[:, 0:hist, :] = jnp.zeros((4, hist, LANES), BF16)
        vt_ref[0, :, 0:hd, 0:GROUP] = jnp.zeros((ATT_KV_HEADS, hd, GROUP), BF16)
        vt_ref[:, :, hd:, :] = jnp.ones((n_groups, ATT_KV_HEADS, ONES_ROWS, 2 * GROUP), BF16)
        s_ref[...] = jnp.zeros(s_ref.shape, F32)

    @pl.when(i > 0)
    def _():
        kx_ref[:, 0:hist, :] = kx_ref[:, t:t + hist, :]
        vt_ref[0, :, 0:hd, 0:GROUP] = vt_ref[n_groups - 1, :, 0:hd, GROUP:]

    for n in range(4):
        kx_ref[n, hist:hist + t, :] = ka_ref[0, :, n * LANES:(n + 1) * LANES]
    for m in range(n_groups):
        tile = va_ref[0, m * GROUP:(m + 1) * GROUP, :].astype(F32).T.astype(BF16)
        for k in range(ATT_KV_HEADS):
            vt_ref[m, k, 0:hd, GROUP:] = tile[k * hd:(k + 1) * hd]
            if m + 1 < n_groups:
                vt_ref[m + 1, k, 0:hd, 0:GROUP] = tile[k * hd:(k + 1) * hd]

    no_keys = jnp.zeros((CHUNK, LANES), BF16)
    causal = (lax.broadcasted_iota(jnp.int32, (GROUP, GROUP), 0)
              >= lax.broadcasted_iota(jnp.int32, (GROUP, GROUP), 1))
    heads = [slice(h * LANES, (h + 1) * LANES) for h in range(RET_HEADS)]
    att = [(j, k, e) for j in range(GROUP // CHUNK) for k in range(ATT_KV_HEADS) for e in range(2)]

    def lead(g):
        grows = slice(g * GROUP, (g + 1) * GROUP)
        raw = {}
        for j in range(GROUP // CHUNK):
            r0 = g * GROUP + j * CHUNK
            for k in range(ATT_KV_HEADS):
                c0 = 2 * LANES * k
                q2 = jnp.concatenate([qa_ref[0, r0:r0 + CHUNK, c0:c0 + LANES],
                                      qa_ref[0, r0:r0 + CHUNK, c0 + LANES:c0 + 2 * LANES]], axis=0)
                keys = jnp.concatenate([kx_ref[2 * k, r0:r0 + BAND, :], kx_ref[2 * k + 1, r0:r0 + BAND, :]], axis=0)
                s2 = _dot_nt(keys, q2)
                raw[j, k, 0], raw[j, k, 1] = s2[:BAND], s2[BAND:]
        states = [s_ref[h] for h in range(RET_HEADS)]
        ret_q = [qd_ref[0, grows, c] for c in heads]
        ret_k = [kg_ref[0, grows, c] for c in heads]
        ret_v = [vr_ref[0, grows, c] for c in heads]
        ret_scores = [_dot_nt(ret_q[h], ret_k[h]) for h in range(RET_HEADS)]
        for h in range(RET_HEADS):
            s_ref[h] = cdec_ref[h] * (states[h] + _dot_tn(ret_k[h], ret_v[h]))
        return raw, states, ret_q, ret_v, ret_scores

    def finish(g, lead_out):
        raw, states, ret_q, ret_v, ret_scores = lead_out
        grows = slice(g * GROUP, (g + 1) * GROUP)
        chunk_rows = lambda j: slice(g * GROUP + j * CHUNK, g * GROUP + (j + 1) * CHUNK)
        probs, sink_terms = {}, {}
        for j, k, e in att:
            n = 2 * k + e
            variant = jnp.minimum(i * (t // CHUNK) + 2 * g + j, BAND_CHUNKS)
            s = raw[j, k, e] + bias_ref[variant, n]
            sink = sink_ref[n]
            m = jnp.maximum(jnp.max(s, axis=0, keepdims=True), sink)
            probs[j, k, e] = jnp.exp2(s - m).astype(BF16)
            sink_terms[j, k, e] = jnp.exp2(sink - m)
        ret_p = [jnp.where(causal, ret_scores[h], 0.0).astype(BF16) for h in range(RET_HEADS)]

        out_t = {}
        for j, k, e in att:
            frame = [probs[j, k, e], no_keys] if j == 0 else [no_keys, probs[j, k, e]]
            out_t[j, k, e] = _dot(vt_ref[g, k], jnp.concatenate(frame, axis=0))
        ret_out = [_dot(jnp.concatenate([ret_p[h], ret_q[h]], axis=1),
                        jnp.concatenate([ret_v[h], states[h].astype(BF16)], axis=0)) for h in range(RET_HEADS)]

        for j in range(GROUP // CHUNK):
            for k in range(ATT_KV_HEADS):
                c0 = 2 * LANES * k
                halves = []
                for e in range(2):
                    o_t = out_t[j, k, e]
                    den = o_t[ATT_HEAD_DIM:ATT_HEAD_DIM + 1] + sink_terms[j, k, e]
                    halves.append(o_t[:ATT_HEAD_DIM] * (1.0 / den))
                o = jnp.concatenate(halves, axis=0).T
                mixed_ref[0, chunk_rows(j), c0:c0 + LANES] = o[:CHUNK].astype(BF16)
                mixed_ref[0, chunk_rows(j), c0 + LANES:c0 + 2 * LANES] = o[CHUNK:].astype(BF16)
        for h in range(RET_HEADS):
            out = _rms(ret_out[h]) * sg_ref[0, grows, heads[h]].astype(F32)
            mixed_ref[0, grows, ATT_Q_W + h * LANES:ATT_Q_W + (h + 1) * LANES] = out.astype(BF16)

    ahead = lead(0)
    for g in range(n_groups):
        current, ahead = ahead, (lead(g + 1) if g + 1 < n_groups else None)
        finish(g, current)

    @pl.when(i == pl.num_programs(1) - 1)
    def _():
        state_ref[0] = s_ref[...]


def _mix_prompt_call(qa, kx, va, qd, kg, vr, sg, bias4, sink4, cdec, weights):
    bsz, seq, _ = qa.shape
    t = min(MIX_TOKEN_BLOCK, seq)
    nb = seq // t
    hist = BAND - CHUNK
    assert hist == GROUP and t % GROUP == 0 and seq % t == 0
    tok = lambda w: pl.BlockSpec((1, t, w), lambda b, i: (b, i, 0))
    steps = bsz * nb
    slabs, slab_in, slab_out, slab_shape = [], [], [], []
    for stack, layer in weights:
        depth, rows, cols = stack.shape
        assert rows % (steps * BF16_SUBLANES) == 0
        slabs.append(stack.reshape(depth, steps, rows // steps, cols))
        slab_in.append(pl.BlockSpec((None, None, rows // steps, cols), lambda b, i, layer=layer: (layer, b * nb + i, 0, 0)))
        slab_out.append(pl.BlockSpec((None, rows // steps, cols), lambda b, i: (b * nb + i, 0, 0)))
        slab_shape.append(jax.ShapeDtypeStruct((steps, rows // steps, cols), BF16))
    outs = pl.pallas_call(
        functools.partial(_mix_prompt_kernel, t=t, n_cast=len(weights)),
        grid=(bsz, nb),
        in_specs=[
            pl.BlockSpec(memory_space=pltpu.SMEM),
            tok(ATT_Q_W), tok(4 * LANES), tok(ATT_KV_W), tok(RET_W), tok(RET_W), tok(RET_W), tok(RET_W),
            _const_spec(bias4.shape), _const_spec(sink4.shape),
        ] + slab_in,
        out_specs=[
            tok(D_MODEL),
            pl.BlockSpec((1, RET_HEADS, RET_KEY_DIM, RET_VAL_DIM), lambda b, i: (b, 0, 0, 0)),
        ] + slab_out,
        out_shape=[
            jax.ShapeDtypeStruct((bsz, seq, D_MODEL), BF16),
            jax.ShapeDtypeStruct((bsz, RET_HEADS, RET_KEY_DIM, RET_VAL_DIM), F32),
        ] + slab_shape,
        scratch_shapes=[
            pltpu.VMEM((4, hist + t, LANES), BF16),
            pltpu.VMEM((t // GROUP, ATT_KV_HEADS, ATT_HEAD_DIM + ONES_ROWS, 2 * GROUP), BF16),
            pltpu.VMEM((RET_HEADS, RET_KEY_DIM, RET_VAL_DIM), F32),
        ],
        compiler_params=_params("arbitrary", "arbitrary"),
        name="mix_prompt",
    )(cdec, qa, kx, va, qd, kg, vr, sg, bias4, sink4, *slabs)
    rounded = [o.reshape(stack.shape[1:]) for o, (stack, _) in zip(outs[2:], weights)]
    return outs[0], outs[1], rounded


def _mix_sample_kernel(cdec_ref, qa_ref, kvn_ref, ck_ref, cv_ref, qd_ref, kg_ref, vr_ref, sg_ref, s0_ref,
                       bias_ref, sink_ref, *rest, n_new, n_cache):
    mixed_ref, *stack_refs = rest[-4:]
    done = 0
    if len(rest) > 4:
        done = rest[0].shape[0]
        for prev_ref, out_ref in zip(rest[:3], stack_refs):
            out_ref[0:done] = prev_ref[...]
    state_ref, wk_ref, wv_ref = (r.at[done] for r in stack_refs)
    pad = jnp.zeros((SAMPLE_KEYS - n_cache - n_new, LANES), F32)
    causal = (lax.broadcasted_iota(jnp.int32, (n_new, n_new), 0)
              >= lax.broadcasted_iota(jnp.int32, (n_new, n_new), 1))
    seqs = range(SAMPLE_GROUP)
    heads = [slice(h * LANES, (h + 1) * LANES) for h in range(RET_HEADS)]
    att = [(g, k, e) for g in seqs for k in range(ATT_KV_HEADS) for e in range(2)]
    ret = [(g, h) for g in seqs for h in range(RET_HEADS)]

    v4, raw = {}, {}
    for g in seqs:
        kvn = kvn_ref[g]
        k_all = jnp.concatenate([ck_ref[g], kvn[:, :ATT_KV_W], pad], axis=0)
        v_all = jnp.concatenate([cv_ref[g], kvn[:, ATT_KV_W:], pad], axis=0)
        wk_ref[g] = k_all[n_new:n_new + n_cache]
        wv_ref[g] = v_all[n_new:n_new + n_cache]
        k4 = [p.astype(BF16) for p in _lane_split(k_all)]
        v4[g] = [p.astype(BF16) for p in _lane_split(v_all)]
        for k in range(ATT_KV_HEADS):
            c0 = 2 * LANES * k
            q2 = jnp.concatenate([qa_ref[g, :, c0:c0 + LANES], qa_ref[g, :, c0 + LANES:c0 + 2 * LANES]], axis=0)
            for e in range(2):
                raw[g, k, e] = _dot_nt(q2, k4[2 * k + e])
    qd = {(g, h): qd_ref[g, :, heads[h]] for g, h in ret}
    kg = {(g, h): kg_ref[g, :, heads[h]] for g, h in ret}
    vr = {(g, h): vr_ref[g, :, heads[h]] for g, h in ret}
    state = {(g, h): s0_ref[g, h] for g, h in ret}
    ret_scores = {gh: _dot_nt(qd[gh], kg[gh]) for gh in ret}
    ret_inter = {gh: _dot(qd[gh], state[gh].astype(BF16)) for gh in ret}
    for g, h in ret:
        state_ref[g, h] = cdec_ref[h] * (state[g, h] + _dot_tn(kg[g, h], vr[g, h]))

    probs = {(g, k, e): _sink_softmax2(raw[g, k, e] + bias_ref[2 * k + e], sink_ref[2 * k + e]).astype(BF16)
             for g, k, e in att}
    ret_p = {gh: jnp.where(causal, ret_scores[gh], 0.0).astype(BF16) for gh in ret}

    out = {(g, k, e): _dot(probs[g, k, e], v4[g][2 * k + e]) for g, k, e in att}
    ret_intra = {gh: _dot(ret_p[gh], vr[gh]) for gh in ret}

    for g in seqs:
        for k in range(ATT_KV_HEADS):
            c0 = 2 * LANES * k
            o = out[g, k, 0] + out[g, k, 1]
            mixed_ref[g, :, c0:c0 + LANES] = o[:n_new].astype(BF16)
            mixed_ref[g, :, c0 + LANES:c0 + 2 * LANES] = o[n_new:].astype(BF16)
        for h in range(RET_HEADS):
            ro = _rms(ret_intra[g, h] + ret_inter[g, h]) * sg_ref[g, :, heads[h]].astype(F32)
            mixed_ref[g, :, ATT_Q_W + h * LANES:ATT_Q_W + (h + 1) * LANES] = ro.astype(BF16)


def _mix_sample_call(qa, kvn, cache_k, cache_v, qd, kg, vr, sg, state, layer, stacks, bias4, sink4, cdec):
    dbsz, n_new, _ = qa.shape
    n_cache = cache_k.shape[2]
    grp = lambda *tail: pl.BlockSpec((SAMPLE_GROUP,) + tail, lambda b: (b,) + (0,) * len(tail))
    lay = lambda *tail: pl.BlockSpec((None, SAMPLE_GROUP) + tail, lambda b: (layer, b) + (0,) * len(tail))
    stk = lambda n, *tail: pl.BlockSpec((n, SAMPLE_GROUP) + tail, lambda b: (0, b) + (0,) * len(tail))
    stack_tails = [(RET_HEADS, RET_KEY_DIM, RET_VAL_DIM), (n_cache, ATT_KV_W), (n_cache, ATT_KV_W)]
    operands = [cdec, qa, kvn, cache_k, cache_v, qd, kg, vr, sg, state, bias4, sink4]
    prev_specs = []
    if stacks is not None:
        assert all(s.shape[0] == layer for s in stacks)
        operands += list(stacks)
        prev_specs = [stk(layer, *tail) for tail in stack_tails]
    outs = pl.pallas_call(
        functools.partial(_mix_sample_kernel, n_new=n_new, n_cache=n_cache),
        grid=(dbsz // SAMPLE_GROUP,),
        in_specs=[
            pl.BlockSpec(memory_space=pltpu.SMEM),
            grp(n_new, ATT_Q_W), grp(n_new, 2 * ATT_KV_W), lay(n_cache, ATT_KV_W), lay(n_cache, ATT_KV_W),
            grp(n_new, RET_W), grp(n_new, RET_W), grp(n_new, RET_W), grp(n_new, RET_W),
            lay(RET_HEADS, RET_KEY_DIM, RET_VAL_DIM),
            _const_spec(bias4.shape), _const_spec(sink4.shape),
        ] + prev_specs,
        out_specs=[grp(n_new, D_MODEL)] + [stk(layer + 1, *tail) for tail in stack_tails],
        out_shape=[jax.ShapeDtypeStruct((dbsz, n_new, D_MODEL), BF16)]
        + [jax.ShapeDtypeStruct((layer + 1, dbsz) + tail, F32) for tail in stack_tails],
        compiler_params=_params("arbitrary"),
        name="mix_sample",
    )(*operands)
    return outs[0], outs[1:]


def _dense_kernel(x_ref, mixed_ref, ga1_ref, sh2_ref, sc2_ref, ga2_ref, g_ref, gf_ref,
                  wo_ref, wu_ref, wd_ref, o_ref, *, final):
    x1 = x_ref[0] + ga1_ref[...] * _dot(mixed_ref[0], wo_ref[...])
    h = ((_rms(x1) * g_ref[...]) * (1.0 + sc2_ref[...]) + sh2_ref[...]).astype(BF16)
    acc = jnp.zeros(x1.shape, F32)
    for j in range(D_FF // FF_BLOCK):
        u = _dot(h, wu_ref[:, j * FF_BLOCK:(j + 1) * FF_BLOCK])
        u = jnp.square(jnp.maximum(u, 0.0)).astype(BF16)
        acc = acc + _dot(u, wd_ref[j * FF_BLOCK:(j + 1) * FF_BLOCK, :])
    x2 = x1 + ga2_ref[...] * acc
    o_ref[0] = _rms(x2) * gf_ref[...] if final else x2


def _dense_call(x, mixed, mods, gains, g_final, w_out, w_up, w_down, layer, final):
    bsz, seq, d = x.shape
    t = min(TOKEN_BLOCK, seq)
    tok = lambda w: pl.BlockSpec((1, t, w), lambda b, i: (b, i, 0))
    return pl.pallas_call(
        functools.partial(_dense_kernel, final=final),
        grid=(bsz, seq // t),
        in_specs=[
            tok(d), tok(d), *_mod_specs(mods, layer, (MOD_GATE1, MOD_SHIFT2, MOD_SCALE2, MOD_GATE2), t),
            _layer_spec(gains, layer), _const_spec((1, d)),
            _const_spec(w_out.shape), _const_spec(w_up.shape), _const_spec(w_down.shape),
        ],
        out_specs=tok(d),
        out_shape=jax.ShapeDtypeStruct((bsz, seq, d), F32),
        compiler_params=_params("arbitrary", "arbitrary"),
        name="dense",
    )(x, mixed, mods, mods, mods, mods, gains, g_final.reshape(1, d), w_out, w_up, w_down)


def _rotary_tables(base_pos, off_pos):
    half = RET_KEY_DIM // 2
    inv = 1.0 / (ROPE_BASE ** (jnp.arange(half, dtype=F32) / half))
    sign = jnp.concatenate([-jnp.ones((half,), F32), jnp.ones((half,), F32)])

    def both(pos):
        ang = pos.astype(F32)[:, None] * inv[None, :]
        cos, sin = jnp.cos(ang), jnp.sin(ang)
        return jnp.concatenate([cos, cos], axis=1), jnp.concatenate([sin, sin], axis=1)

    base_cos, base_sin = both(base_pos)
    off_cos, off_sin = both(off_pos)
    return base_cos[:, None, :], base_sin[:, None, :], (off_cos, off_sin, sign * off_cos, sign * off_sin)


def _retention_tables(n, rows):
    log_g = jnp.log(1.0 - 2.0 ** (-5.0 - jnp.arange(RET_HEADS, dtype=F32)))
    steps = (jnp.arange(rows) % n).astype(F32) + 1.0
    expo = steps[:, None] * jnp.repeat(log_g, RET_KEY_DIM)[None, :]
    return (jnp.exp(expo), jnp.exp(-expo) * (RET_KEY_DIM ** -0.5)), jnp.exp(n * log_g)


def kernel(x_prompt, x_sample, c_prompt, c_sample, cache_win_k, cache_win_v, state_ret, g_mix, g_mlp, w_ada,
           b_ada, w_in, w_out, att_sinks, rel_bias, w_up, w_down, g_final):
    depth = w_in.shape[0]
    bsz, seq, d = x_prompt.shape
    dbsz, dec_seq, _ = x_sample.shape
    n_cache = cache_win_k.shape[2]
    n_rows = dbsz * dec_seq
    keep_p = min(WINDOW, seq)

    w_in_b = w_in[0].astype(BF16)
    g_mix3 = g_mix.reshape(depth, 1, d)
    g_mlp3 = g_mlp.reshape(depth, 1, d)
    cache_k = cache_win_k.reshape(depth, dbsz, n_cache, ATT_KV_W)
    cache_v = cache_win_v.reshape(depth, dbsz, n_cache, ATT_KV_W)

    c_all = jnp.concatenate([c_prompt, c_sample], axis=0)
    c_all = jnp.pad(c_all, ((0, -c_all.shape[0] % 8), (0, 0)))
    mods = _ada_call(c_all, w_ada, b_ada)
    mods_p = mods[:, :bsz, None, :]
    mods_s = jnp.repeat(mods[:, bsz:bsz + dbsz], dec_seq, axis=1)

    t_p = min(TOKEN_BLOCK, seq)
    rot_p = _rotary_tables(jnp.arange(seq // t_p) * t_p, jnp.arange(t_p))
    rot_s = _rotary_tables(jnp.full((1,), PAST_LEN), jnp.tile(jnp.arange(dec_seq), dbsz))
    decay_p, cdec_p = _retention_tables(GROUP, t_p)
    decay_s, cdec_s = _retention_tables(dec_seq, n_rows)

    rel_p = (jnp.arange(BAND) - BAND_CHUNKS * CHUNK)[None, :] - jnp.arange(CHUNK)[:, None]
    bias_p = _banded_bias(_bias_call(rel_bias, _rel_bucket(rel_p)) * LOG2_E)
    kpos = PAST_LEN - n_cache + jnp.arange(n_cache + dec_seq)
    qpos = PAST_LEN + jnp.arange(dec_seq)
    bias_s = _bias_call(rel_bias, _rel_bucket(kpos[None, :] - qpos[:, None])) * LOG2_E
    bias_s = jnp.pad(bias_s, ((0, 0), (0, 0), (0, SAMPLE_KEYS - n_cache - dec_seq)), constant_values=-jnp.inf)
    bias_s = _pair_rows(bias_s)

    xp = x_prompt
    xs = x_sample.reshape(1, n_rows, d)
    pk, pv, pr = [], [], []
    stacks = None
    for l in range(depth):
        sinks = att_sinks[l].astype(F32)[:, None, None] * LOG2_E
        sink_p = _pair_lanes(jnp.broadcast_to(sinks, (ATT_HEADS, 1, CHUNK)))
        sink_s = _pair_rows(jnp.broadcast_to(sinks, (ATT_HEADS, dec_seq, 1)))
        final = l == depth - 1

        qa, kx, va, qd, kg, vr, sg, kvt = _inproj_call(xp, mods_p, g_mix3, w_in_b, l, rot_p, decay_p, keep_p)
        to_round = [(w_out, l), (w_up, l), (w_down, l)] + ([] if final else [(w_in, l + 1)])
        mixed, st, rounded = _mix_prompt_call(qa, kx, va, qd, kg, vr, sg, bias_p, sink_p, cdec_p, to_round)
        w_out_b, w_up_b, w_down_b = rounded[:3]
        w_in_next = None if final else rounded[3]
        xp = _dense_call(xp, mixed, mods_p, g_mlp3, g_final, w_out_b, w_up_b, w_down_b, l, final)
        pk.append(kvt[:, :, :ATT_KV_W].reshape(bsz, keep_p, ATT_KV_HEADS, ATT_HEAD_DIM))
        pv.append(kvt[:, :, ATT_KV_W:].reshape(bsz, keep_p, ATT_KV_HEADS, ATT_HEAD_DIM))
        pr.append(st)

        qa, _, _, qd, kg, vr, sg, kvn = _inproj_call(xs, mods_s, g_mix3, w_in_b, l, rot_s, decay_s, n_rows)
        per_seq = lambda a: a.reshape(dbsz, dec_seq, a.shape[-1])
        mixed, stacks = _mix_sample_call(per_seq(qa), per_seq(kvn), cache_k, cache_v, per_seq(qd), per_seq(kg),
                                         per_seq(vr), per_seq(sg), state_ret, l, stacks, bias_s, sink_s, cdec_s)
        xs = _dense_call(xs, mixed.reshape(1, n_rows, d), mods_s, g_mlp3, g_final, w_out_b, w_up_b, w_down_b, l,
                         final)
        w_in_b = w_in_next

    sr, sk, sv = stacks
    win_shape = (depth, dbsz, n_cache, ATT_KV_HEADS, ATT_HEAD_DIM)
    return (xp, xs.reshape(dbsz, dec_seq, d), jnp.stack(pk), jnp.stack(pv), jnp.stack(pr),
            sk.reshape(win_shape), sv.reshape(win_shape), sr)
```

```python
import functools
import math

import jax
import jax.numpy as jnp
import numpy as np
from jax import lax
from jax.experimental import pallas as pl
from jax.experimental.pallas import tpu as pltpu

F32 = jnp.float32
BF16 = jnp.bfloat16

D_MODEL = 1024
CHUNK = 64
ATT_HEADS = 8
ATT_KV_HEADS = 2
ATT_HEAD_DIM = 64
ATT_Q_W = ATT_HEADS * ATT_HEAD_DIM
ATT_KV_W = ATT_KV_HEADS * ATT_HEAD_DIM
ATT_SCALE = ATT_HEAD_DIM ** -0.5
WINDOW = 128
BAND_CHUNKS = -(-WINDOW // CHUNK)
BAND = (BAND_CHUNKS + 1) * CHUNK
REL_BUCKETS = 32
REL_MAX_DIST = 128
RET_HEADS = 4
RET_KEY_DIM = 128
RET_VAL_DIM = 128
RET_W = RET_HEADS * RET_KEY_DIM
ROPE_BASE = 10000.0
IN_W = ATT_Q_W + 2 * ATT_KV_W + 4 * RET_W
D_FF = 4 * D_MODEL
N_MOD = 6
EPS = 1e-6
PAST_LEN = 4096
LOG2_E = math.log2(math.e)

LANES = 128
HALF_LANES = LANES // 2
BF16_SUBLANES = 16
ONES_ROWS = BF16_SUBLANES
VMEM_LIMIT_BYTES = 56 * 1024 * 1024

OFF_QA = 0
OFF_KV = OFF_QA + ATT_Q_W
OFF_QR = OFF_KV + 2 * ATT_KV_W
OFF_KR = OFF_QR + RET_W
OFF_VR = OFF_KR + RET_W
OFF_GR = OFF_VR + RET_W

TOKEN_BLOCK = 1024
INPROJ_SUB_BLOCK = 256
MIX_TOKEN_BLOCK = 1024
GROUP = 128
FF_BLOCK = 1024
ADA_COL_BLOCK = 1024
SAMPLE_GROUP = 4
SAMPLE_KEYS = 2 * LANES


def _params(*sem):
    return pltpu.CompilerParams(dimension_semantics=sem, vmem_limit_bytes=VMEM_LIMIT_BYTES)


def _const_spec(shape):
    zeros = (0,) * len(shape)
    return pl.BlockSpec(shape, lambda *_: zeros, pipeline_mode=pl.Buffered(1))


def _rms(x):
    return x * lax.rsqrt(jnp.mean(x * x, axis=-1, keepdims=True) + EPS)


def _dot(a, b):
    return jnp.dot(a, b, preferred_element_type=F32)


def _dot_nt(a, b):
    return lax.dot_general(a, b, (((1,), (1,)), ((), ())), preferred_element_type=F32)


def _dot_tn(a, b):
    return lax.dot_general(a, b, (((0,), (0,)), ((), ())), preferred_element_type=F32)


def _ada_kernel(c_ref, w_ref, b_ref, o_ref):
    a = jax.nn.silu(c_ref[...]).astype(BF16)
    o_ref[0] = _dot(a, w_ref[0].astype(BF16)) + b_ref[0]


def _ada_call(c_all, w_ada, b_ada):
    depth, d, n = w_ada.shape
    rows = c_all.shape[0]
    return pl.pallas_call(
        _ada_kernel,
        grid=(depth, n // ADA_COL_BLOCK),
        in_specs=[
            pl.BlockSpec((rows, d), lambda l, j: (0, 0)),
            pl.BlockSpec((1, d, ADA_COL_BLOCK), lambda l, j: (l, 0, j)),
            pl.BlockSpec((1, 1, ADA_COL_BLOCK), lambda l, j: (l, 0, j)),
        ],
        out_specs=pl.BlockSpec((1, rows, ADA_COL_BLOCK), lambda l, j: (l, 0, j)),
        out_shape=jax.ShapeDtypeStruct((depth, rows, n), F32),
        compiler_params=_params("arbitrary", "arbitrary"),
        name="ada",
    )(c_all, w_ada, b_ada.reshape(depth, 1, n))


def _bias_kernel(tab_ref, idx_ref, o_ref):
    h = pl.program_id(0)
    idx = idx_ref[...]
    acc = jnp.zeros(idx.shape, F32)
    for b in range(REL_BUCKETS):
        acc = jnp.where(idx == b, tab_ref[b, h], acc)
    o_ref[0] = acc


def _bias_call(rel_bias, bucket):
    q, j = bucket.shape
    return pl.pallas_call(
        _bias_kernel,
        grid=(ATT_HEADS,),
        in_specs=[
            pl.BlockSpec(memory_space=pltpu.SMEM),
            pl.BlockSpec((q, j), lambda h: (0, 0)),
        ],
        out_specs=pl.BlockSpec((1, q, j), lambda h: (h, 0, 0)),
        out_shape=jax.ShapeDtypeStruct((ATT_HEADS, q, j), F32),
        compiler_params=_params("arbitrary"),
        name="rel_bias",
    )(rel_bias, bucket)


def _rel_bucket(rel):
    nb = REL_BUCKETS // 2
    n = -np.asarray(rel, np.int64)
    ret = np.where(n < 0, nb, 0)
    n = np.abs(n)
    max_exact = nb // 2
    nf = np.maximum(n, 1).astype(np.float64)
    large = max_exact + (np.log(nf / max_exact) / math.log(REL_MAX_DIST / max_exact)
                         * (nb - max_exact)).astype(np.int64)
    large = np.minimum(large, nb - 1)
    return (ret + np.where(n < max_exact, n, large)).astype(np.int32)


def _pair_rows(per_head):
    out = []
    for k in range(ATT_KV_HEADS):
        for e in range(2):
            out.append(jnp.concatenate([per_head[4 * k + e], per_head[4 * k + 2 + e]], axis=0))
    return jnp.stack(out)


def _pair_lanes(per_head):
    out = []
    for k in range(ATT_KV_HEADS):
        for e in range(2):
            out.append(jnp.concatenate([per_head[4 * k + e], per_head[4 * k + 2 + e]], axis=-1))
    return jnp.stack(out)


def _banded_bias(per_head):
    base = _pair_lanes(jnp.swapaxes(per_head, 1, 2))
    key = jnp.arange(BAND)[None, :, None]
    return jnp.stack([jnp.where(key < (BAND_CHUNKS - v) * CHUNK, -jnp.inf, base)
                      for v in range(BAND_CHUNKS + 1)])


def _rotary(u, cos2, sin2):
    parts = []
    for h in range(RET_HEADS):
        x = u[:, h * RET_KEY_DIM:(h + 1) * RET_KEY_DIM]
        parts.append(x * cos2 + pltpu.roll(x, HALF_LANES, 1) * sin2)
    return jnp.concatenate(parts, axis=1)


def _inproj_kernel(x_ref, sh_ref, sc_ref, g_ref, w_ref, ac_ref, as_ref, bc_ref, bs_ref, bcn_ref, bsn_ref,
                   qdec_ref, kdec_ref,
                   qa_ref, kx_ref, va_ref, qd_ref, kg_ref, vr_ref, sg_ref, kvt_ref, *, tail):
    i = pl.program_id(1)
    t = x_ref.shape[1]
    sub = min(INPROJ_SUB_BLOCK, t)
    for r0 in range(0, t, sub):
        rows = slice(r0, r0 + sub)
        per_row = lambda ref: ref[rows] if ref.shape[0] == t else ref[...]
        x = x_ref[0, rows]
        h = (_rms(x) * g_ref[...]) * (1.0 + per_row(sc_ref)) + per_row(sh_ref)
        hb = h.astype(BF16)
        cos2 = ac_ref[...] * bc_ref[rows] - as_ref[...] * bs_ref[rows]
        sin2 = as_ref[...] * bcn_ref[rows] + ac_ref[...] * bsn_ref[rows]

        qa_ref[0, rows] = (_dot(hb, w_ref[:, OFF_QA:OFF_KV]) * (ATT_SCALE * LOG2_E)).astype(BF16)
        kv = _dot(hb, w_ref[:, OFF_KV:OFF_QR])
        kx_ref[0, rows] = jnp.concatenate(_lane_split(kv[:, :ATT_KV_W]), axis=1).astype(BF16)
        va_ref[0, rows] = kv[:, ATT_KV_W:].astype(BF16)
        qd_ref[0, rows] = (_rotary(_dot(hb, w_ref[:, OFF_QR:OFF_KR]), cos2, sin2) * qdec_ref[rows]).astype(BF16)
        kg_ref[0, rows] = (_rotary(_dot(hb, w_ref[:, OFF_KR:OFF_VR]), cos2, sin2) * kdec_ref[rows]).astype(BF16)
        vr_ref[0, rows] = _dot(hb, w_ref[:, OFF_VR:OFF_GR]).astype(BF16)
        sg_ref[0, rows] = jax.nn.silu(_dot(hb, w_ref[:, OFF_GR:IN_W])).astype(BF16)

        lo = max(r0, t - tail)
        if lo < r0 + sub:
            @pl.when(i == pl.num_programs(1) - 1)
            def _(kv=kv, lo=lo, r0=r0):
                kvt_ref[0, lo - (t - tail):r0 + sub - (t - tail)] = kv[lo - r0:]


MOD_SHIFT1, MOD_SCALE1, MOD_GATE1, MOD_SHIFT2, MOD_SCALE2, MOD_GATE2 = range(N_MOD)


def _mod_specs(mods, layer, which, t):
    if mods.ndim == 4:
        return [pl.BlockSpec((None, None, 1, D_MODEL), lambda b, i, m=m: (layer, b, 0, m)) for m in which]
    return [pl.BlockSpec((None, t, D_MODEL), lambda b, i, m=m: (layer, i, m)) for m in which]


def _layer_spec(arr, layer):
    tail_zeros = (0,) * (arr.ndim - 1)
    return pl.BlockSpec((None,) + arr.shape[1:], lambda *_: (layer,) + tail_zeros, pipeline_mode=pl.Buffered(1))


def _inproj_call(x, mods, gains, w_in, layer, rot, decay, tail):
    bsz, seq, d = x.shape
    t = min(TOKEN_BLOCK, seq)
    nb = seq // t
    base_cos, base_sin, off = rot
    assert base_cos.shape == (nb, 1, LANES) and off[0].shape == (t, LANES)
    assert all(a.shape == (t, RET_W) for a in decay)
    tok = lambda w: pl.BlockSpec((1, t, w), lambda b, i: (b, i, 0))
    out_bf = lambda w: jax.ShapeDtypeStruct((bsz, seq, w), BF16)
    base = pl.BlockSpec((None, 1, LANES), lambda b, i: (i, 0, 0))
    return pl.pallas_call(
        functools.partial(_inproj_kernel, tail=tail),
        grid=(bsz, nb),
        in_specs=[
            tok(d), *_mod_specs(mods, layer, (MOD_SHIFT1, MOD_SCALE1), t),
            _layer_spec(gains, layer), _const_spec(w_in.shape),
            base, base, *[_const_spec((t, LANES))] * 4, *[_const_spec((t, RET_W))] * 2,
        ],
        out_specs=[
            tok(ATT_Q_W), tok(4 * LANES), tok(ATT_KV_W), tok(RET_W), tok(RET_W), tok(RET_W), tok(RET_W),
            pl.BlockSpec((1, tail, 2 * ATT_KV_W), lambda b, i: (b, 0, 0)),
        ],
        out_shape=[
            out_bf(ATT_Q_W), out_bf(4 * LANES), out_bf(ATT_KV_W), out_bf(RET_W), out_bf(RET_W),
            out_bf(RET_W), out_bf(RET_W),
            jax.ShapeDtypeStruct((bsz, tail, 2 * ATT_KV_W), F32),
        ],
        compiler_params=_params("arbitrary", "arbitrary"),
        name="inproj",
    )(x, mods, mods, gains, w_in, base_cos, base_sin, *off, *decay)


def _lane_split(x):
    lo = lax.broadcasted_iota(jnp.int32, x.shape, 1) < HALF_LANES
    sw = pltpu.roll(x, HALF_LANES, 1)
    zero = jnp.zeros_like(x)
    return (jnp.where(lo, x, zero), jnp.where(lo, zero, sw), jnp.where(lo, sw, zero), jnp.where(lo, zero, x))


def _sink_softmax2(s, sink):
    m = jnp.maximum(jnp.max(s, axis=-1, keepdims=True), sink)
    e = jnp.exp2(s - m)
    den = jnp.sum(e, axis=-1, keepdims=True) + jnp.exp2(sink - m)
    return e / den


def _mix_prompt_kernel(cdec_ref, qa_ref, ka_ref, va_ref, qd_ref, kg_ref, vr_ref, sg_ref, bias_ref, sink_ref,
                       *rest, t, n_cast):
    cast_src = rest[:n_cast]
    mixed_ref, state_ref = rest[n_cast:n_cast + 2]
    cast_dst = rest[n_cast + 2:2 * n_cast + 2]
    kx_ref, vt_ref, s_ref = rest[2 * n_cast + 2:]
    for src, dst in zip(cast_src, cast_dst):
        dst[...] = src[...].astype(BF16)

    i = pl.program_id(1)
    n_groups = t // GROUP
    hist = BAND - CHUNK

    hd = ATT_HEAD_DIM

    @pl.when(i == 0)
    def _():
        kx_ref[:, 0:hist, :] = jnp.zeros((4, hist, LANES), BF16)
        vt_ref[0, :, 0:hd, 0:GROUP] = jnp.zeros((ATT_KV_HEADS, hd, GROUP), BF16)
        vt_ref[:, :, hd:, :] = jnp.ones((n_groups, ATT_KV_HEADS, ONES_ROWS, 2 * GROUP), BF16)
        s_ref[...] = jnp.zeros(s_ref.shape, F32)

    @pl.when(i > 0)
    def _():
        kx_ref[:, 0:hist, :] = kx_ref[:, t:t + hist, :]
        vt_ref[0, :, 0:hd, 0:GROUP] = vt_ref[n_groups - 1, :, 0:hd, GROUP:]

    for n in range(4):
        kx_ref[n, hist:hist + t, :] = ka_ref[0, :, n * LANES:(n + 1) * LANES]
    for m in range(n_groups):
        tile = va_ref[0, m * GROUP:(m + 1) * GROUP, :].astype(F32).T.astype(BF16)
        for k in range(ATT_KV_HEADS):
            vt_ref[m, k, 0:hd, GROUP:] = tile[k * hd:(k + 1) * hd]
            if m + 1 < n_groups:
                vt_ref[m + 1, k, 0:hd, 0:GROUP] = tile[k * hd:(k + 1) * hd]

    no_keys = jnp.zeros((CHUNK, LANES), BF16)
    causal = (lax.broadcasted_iota(jnp.int32, (GROUP, GROUP), 0)
              >= lax.broadcasted_iota(jnp.int32, (GROUP, GROUP), 1))
    heads = [slice(h * LANES, (h + 1) * LANES) for h in range(RET_HEADS)]
    att = [(j, k, e) for j in range(GROUP // CHUNK) for k in range(ATT_KV_HEADS) for e in range(2)]

    def lead(g):
        grows = slice(g * GROUP, (g + 1) * GROUP)
        raw = {}
        for j in range(GROUP // CHUNK):
            r0 = g * GROUP + j * CHUNK
            for k in range(ATT_KV_HEADS):
                c0 = 2 * LANES * k
                q2 = jnp.concatenate([qa_ref[0, r0:r0 + CHUNK, c0:c0 + LANES],
                                      qa_ref[0, r0:r0 + CHUNK, c0 + LANES:c0 + 2 * LANES]], axis=0)
                keys = jnp.concatenate([kx_ref[2 * k, r0:r0 + BAND, :], kx_ref[2 * k + 1, r0:r0 + BAND, :]], axis=0)
                s2 = _dot_nt(keys, q2)
                raw[j, k, 0], raw[j, k, 1] = s2[:BAND], s2[BAND:]
        states = [s_ref[h] for h in range(RET_HEADS)]
        ret_q = [qd_ref[0, grows, c] for c in heads]
        ret_k = [kg_ref[0, grows, c] for c in heads]
        ret_v = [vr_ref[0, grows, c] for c in heads]
        ret_scores = [_dot_nt(ret_q[h], ret_k[h]) for h in range(RET_HEADS)]
        for h in range(RET_HEADS):
            s_ref[h] = cdec_ref[h] * (states[h] + _dot_tn(ret_k[h], ret_v[h]))
        return raw, states, ret_q, ret_v, ret_scores

    def finish(g, lead_out):
        raw, states, ret_q, ret_v, ret_scores = lead_out
        grows = slice(g * GROUP, (g + 1) * GROUP)
        chunk_rows = lambda j: slice(g * GROUP + j * CHUNK, g * GROUP + (j + 1) * CHUNK)
        probs, sink_terms = {}, {}
        for j, k, e in att:
            n = 2 * k + e
            variant = jnp.minimum(i * (t // CHUNK) + 2 * g + j, BAND_CHUNKS)
            s = raw[j, k, e] + bias_ref[variant, n]
            sink = sink_ref[n]
            m = jnp.maximum(jnp.max(s, axis=0, keepdims=True), sink)
            probs[j, k, e] = jnp.exp2(s - m).astype(BF16)
            sink_terms[j, k, e] = jnp.exp2(sink - m)
        ret_p = [jnp.where(causal, ret_scores[h], 0.0).astype(BF16) for h in range(RET_HEADS)]

        out_t = {}
        for j, k, e in att:
            frame = [probs[j, k, e], no_keys] if j == 0 else [no_keys, probs[j, k, e]]
            out_t[j, k, e] = _dot(vt_ref[g, k], jnp.concatenate(frame, axis=0))
        ret_out = [_dot(jnp.concatenate([ret_p[h], ret_q[h]], axis=1),
                        jnp.concatenate([ret_v[h], states[h].astype(BF16)], axis=0)) for h in range(RET_HEADS)]

        for j in range(GROUP // CHUNK):
            for k in range(ATT_KV_HEADS):
                c0 = 2 * LANES * k
                halves = []
                for e in range(2):
                    o_t = out_t[j, k, e]
                    den = o_t[ATT_HEAD_DIM:ATT_HEAD_DIM + 1] + sink_terms[j, k, e]
                    halves.append(o_t[:ATT_HEAD_DIM] * (1.0 / den))
                o = jnp.concatenate(halves, axis=0).T
                mixed_ref[0, chunk_rows(j), c0:c0 + LANES] = o[:CHUNK].astype(BF16)
                mixed_ref[0, chunk_rows(j), c0 + LANES:c0 + 2 * LANES] = o[CHUNK:].astype(BF16)
        for h in range(RET_HEADS):
            out = _rms(ret_out[h]) * sg_ref[0, grows, heads[h]].astype(F32)
            mixed_ref[0, grows, ATT_Q_W + h * LANES:ATT_Q_W + (h + 1) * LANES] = out.astype(BF16)

    for g in range(n_groups):
        finish(g, lead(g))

    @pl.when(i == pl.num_programs(1) - 1)
    def _():
        state_ref[0] = s_ref[...]


def _mix_prompt_call(qa, kx, va, qd, kg, vr, sg, bias4, sink4, cdec, weights):
    bsz, seq, _ = qa.shape
    t = min(MIX_TOKEN_BLOCK, seq)
    nb = seq // t
    hist = BAND - CHUNK
    assert hist == GROUP and t % GROUP == 0 and seq % t == 0
    tok = lambda w: pl.BlockSpec((1, t, w), lambda b, i: (b, i, 0))
    steps = bsz * nb
    slabs, slab_in, slab_out, slab_shape = [], [], [], []
    for stack, layer in weights:
        depth, rows, cols = stack.shape
        assert rows % (steps * BF16_SUBLANES) == 0
        slabs.append(stack.reshape(depth, steps, rows // steps, cols))
        slab_in.append(pl.BlockSpec((None, None, rows // steps, cols), lambda b, i, layer=layer: (layer, b * nb + i, 0, 0)))
        slab_out.append(pl.BlockSpec((None, rows // steps, cols), lambda b, i: (b * nb + i, 0, 0)))
        slab_shape.append(jax.ShapeDtypeStruct((steps, rows // steps, cols), BF16))
    outs = pl.pallas_call(
        functools.partial(_mix_prompt_kernel, t=t, n_cast=len(weights)),
        grid=(bsz, nb),
        in_specs=[
            pl.BlockSpec(memory_space=pltpu.SMEM),
            tok(ATT_Q_W), tok(4 * LANES), tok(ATT_KV_W), tok(RET_W), tok(RET_W), tok(RET_W), tok(RET_W),
            _const_spec(bias4.shape), _const_spec(sink4.shape),
        ] + slab_in,
        out_specs=[
            tok(D_MODEL),
            pl.BlockSpec((1, RET_HEADS, RET_KEY_DIM, RET_VAL_DIM), lambda b, i: (b, 0, 0, 0)),
        ] + slab_out,
        out_shape=[
            jax.ShapeDtypeStruct((bsz, seq, D_MODEL), BF16),
            jax.ShapeDtypeStruct((bsz, RET_HEADS, RET_KEY_DIM, RET_VAL_DIM), F32),
        ] + slab_shape,
        scratch_shapes=[
            pltpu.VMEM((4, hist + t, LANES), BF16),
            pltpu.VMEM((t // GROUP, ATT_KV_HEADS, ATT_HEAD_DIM + ONES_ROWS, 2 * GROUP), BF16),
            pltpu.VMEM((RET_HEADS, RET_KEY_DIM, RET_VAL_DIM), F32),
        ],
        compiler_params=_params("arbitrary", "arbitrary"),
        name="mix_prompt",
    )(cdec, qa, kx, va, qd, kg, vr, sg, bias4, sink4, *slabs)
    rounded = [o.reshape(stack.shape[1:]) for o, (stack, _) in zip(outs[2:], weights)]
    return outs[0], outs[1], rounded


def _mix_sample_kernel(cdec_ref, qa_ref, kvn_ref, ck_ref, cv_ref, qd_ref, kg_ref, vr_ref, sg_ref, s0_ref,
                       bias_ref, sink_ref, *rest, n_new, n_cache):
    mixed_ref, *stack_refs = rest[-4:]
    done = 0
    if len(rest) > 4:
        done = rest[0].shape[0]
        for prev_ref, out_ref in zip(rest[:3], stack_refs):
            out_ref[0:done] = prev_ref[...]
    state_ref, wk_ref, wv_ref = (r.at[done] for r in stack_refs)
    pad = jnp.zeros((SAMPLE_KEYS - n_cache - n_new, LANES), F32)
    causal = (lax.broadcasted_iota(jnp.int32, (n_new, n_new), 0)
              >= lax.broadcasted_iota(jnp.int32, (n_new, n_new), 1))
    seqs = range(SAMPLE_GROUP)
    heads = [slice(h * LANES, (h + 1) * LANES) for h in range(RET_HEADS)]
    att = [(g, k, e) for g in seqs for k in range(ATT_KV_HEADS) for e in range(2)]
    ret = [(g, h) for g in seqs for h in range(RET_HEADS)]

    v4, raw = {}, {}
    for g in seqs:
        kvn = kvn_ref[g]
        k_all = jnp.concatenate([ck_ref[g], kvn[:, :ATT_KV_W], pad], axis=0)
        v_all = jnp.concatenate([cv_ref[g], kvn[:, ATT_KV_W:], pad], axis=0)
        wk_ref[g] = k_all[n_new:n_new + n_cache]
        wv_ref[g] = v_all[n_new:n_new + n_cache]
        k4 = [p.astype(BF16) for p in _lane_split(k_all)]
        v4[g] = [p.astype(BF16) for p in _lane_split(v_all)]
        for k in range(ATT_KV_HEADS):
            c0 = 2 * LANES * k
            q2 = jnp.concatenate([qa_ref[g, :, c0:c0 + LANES], qa_ref[g, :, c0 + LANES:c0 + 2 * LANES]], axis=0)
            for e in range(2):
                raw[g, k, e] = _dot_nt(q2, k4[2 * k + e])
    qd = {(g, h): qd_ref[g, :, heads[h]] for g, h in ret}
    kg = {(g, h): kg_ref[g, :, heads[h]] for g, h in ret}
    vr = {(g, h): vr_ref[g, :, heads[h]] for g, h in ret}
    state = {(g, h): s0_ref[g, h] for g, h in ret}
    ret_scores = {gh: _dot_nt(qd[gh], kg[gh]) for gh in ret}
    ret_inter = {gh: _dot(qd[gh], state[gh].astype(BF16)) for gh in ret}
    for g, h in ret:
        state_ref[g, h] = cdec_ref[h] * (state[g, h] + _dot_tn(kg[g, h], vr[g, h]))

    probs = {(g, k, e): _sink_softmax2(raw[g, k, e] + bias_ref[2 * k + e], sink_ref[2 * k + e]).astype(BF16)
             for g, k, e in att}
    ret_p = {gh: jnp.where(causal, ret_scores[gh], 0.0).astype(BF16) for gh in ret}

    out = {(g, k, e): _dot(probs[g, k, e], v4[g][2 * k + e]) for g, k, e in att}
    ret_intra = {gh: _dot(ret_p[gh], vr[gh]) for gh in ret}

    for g in seqs:
        for k in range(ATT_KV_HEADS):
            c0 = 2 * LANES * k
            o = out[g, k, 0] + out[g, k, 1]
            mixed_ref[g, :, c0:c0 + LANES] = o[:n_new].astype(BF16)
            mixed_ref[g, :, c0 + LANES:c0 + 2 * LANES] = o[n_new:].astype(BF16)
        for h in range(RET_HEADS):
            ro = _rms(ret_intra[g, h] + ret_inter[g, h]) * sg_ref[g, :, heads[h]].astype(F32)
            mixed_ref[g, :, ATT_Q_W + h * LANES:ATT_Q_W + (h + 1) * LANES] = ro.astype(BF16)


def _mix_sample_call(qa, kvn, cache_k, cache_v, qd, kg, vr, sg, state, layer, stacks, bias4, sink4, cdec):
    dbsz, n_new, _ = qa.shape
    n_cache = cache_k.shape[2]
    grp = lambda *tail: pl.BlockSpec((SAMPLE_GROUP,) + tail, lambda b: (b,) + (0,) * len(tail))
    lay = lambda *tail: pl.BlockSpec((None, SAMPLE_GROUP) + tail, lambda b: (layer, b) + (0,) * len(tail))
    stk = lambda n, *tail: pl.BlockSpec((n, SAMPLE_GROUP) + tail, lambda b: (0, b) + (0,) * len(tail))
    stack_tails = [(RET_HEADS, RET_KEY_DIM, RET_VAL_DIM), (n_cache, ATT_KV_W), (n_cache, ATT_KV_W)]
    operands = [cdec, qa, kvn, cache_k, cache_v, qd, kg, vr, sg, state, bias4, sink4]
    prev_specs = []
    if stacks is not None:
        assert all(s.shape[0] == layer for s in stacks)
        operands += list(stacks)
        prev_specs = [stk(layer, *tail) for tail in stack_tails]
    outs = pl.pallas_call(
        functools.partial(_mix_sample_kernel, n_new=n_new, n_cache=n_cache),
        grid=(dbsz // SAMPLE_GROUP,),
        in_specs=[
            pl.BlockSpec(memory_space=pltpu.SMEM),
            grp(n_new, ATT_Q_W), grp(n_new, 2 * ATT_KV_W), lay(n_cache, ATT_KV_W), lay(n_cache, ATT_KV_W),
            grp(n_new, RET_W), grp(n_new, RET_W), grp(n_new, RET_W), grp(n_new, RET_W),
            lay(RET_HEADS, RET_KEY_DIM, RET_VAL_DIM),
            _const_spec(bias4.shape), _const_spec(sink4.shape),
        ] + prev_specs,
        out_specs=[grp(n_new, D_MODEL)] + [stk(layer + 1, *tail) for tail in stack_tails],
        out_shape=[jax.ShapeDtypeStruct((dbsz, n_new, D_MODEL), BF16)]
        + [jax.ShapeDtypeStruct((layer + 1, dbsz) + tail, F32) for tail in stack_tails],
        compiler_params=_params("arbitrary"),
        name="mix_sample",
    )(*operands)
    return outs[0], outs[1:]


def _dense_kernel(x_ref, mixed_ref, ga1_ref, sh2_ref, sc2_ref, ga2_ref, g_ref, gf_ref,
                  wo_ref, wu_ref, wd_ref, o_ref, *, final):
    x1 = x_ref[0] + ga1_ref[...] * _dot(mixed_ref[0], wo_ref[...])
    h = ((_rms(x1) * g_ref[...]) * (1.0 + sc2_ref[...]) + sh2_ref[...]).astype(BF16)
    acc = jnp.zeros(x1.shape, F32)
    for j in range(D_FF // FF_BLOCK):
        u = _dot(h, wu_ref[:, j * FF_BLOCK:(j + 1) * FF_BLOCK])
        u = jnp.square(jnp.maximum(u, 0.0)).astype(BF16)
        acc = acc + _dot(u, wd_ref[j * FF_BLOCK:(j + 1) * FF_BLOCK, :])
    x2 = x1 + ga2_ref[...] * acc
    o_ref[0] = _rms(x2) * gf_ref[...] if final else x2


def _dense_call(x, mixed, mods, gains, g_final, w_out, w_up, w_down, layer, final):
    bsz, seq, d = x.shape
    t = min(TOKEN_BLOCK, seq)
    tok = lambda w: pl.BlockSpec((1, t, w), lambda b, i: (b, i, 0))
    return pl.pallas_call(
        functools.partial(_dense_kernel, final=final),
        grid=(bsz, seq // t),
        in_specs=[
            tok(d), tok(d), *_mod_specs(mods, layer, (MOD_GATE1, MOD_SHIFT2, MOD_SCALE2, MOD_GATE2), t),
            _layer_spec(gains, layer), _const_spec((1, d)),
            _const_spec(w_out.shape), _const_spec(w_up.shape), _const_spec(w_down.shape),
        ],
        out_specs=tok(d),
        out_shape=jax.ShapeDtypeStruct((bsz, seq, d), F32),
        compiler_params=_params("arbitrary", "arbitrary"),
        name="dense",
    )(x, mixed, mods, mods, mods, mods, gains, g_final.reshape(1, d), w_out, w_up, w_down)


def _rotary_tables(base_pos, off_pos):
    half = RET_KEY_DIM // 2
    inv = 1.0 / (ROPE_BASE ** (jnp.arange(half, dtype=F32) / half))
    sign = jnp.concatenate([-jnp.ones((half,), F32), jnp.ones((half,), F32)])

    def both(pos):
        ang = pos.astype(F32)[:, None] * inv[None, :]
        cos, sin = jnp.cos(ang), jnp.sin(ang)
        return jnp.concatenate([cos, cos], axis=1), jnp.concatenate([sin, sin], axis=1)

    base_cos, base_sin = both(base_pos)
    off_cos, off_sin = both(off_pos)
    return base_cos[:, None, :], base_sin[:, None, :], (off_cos, off_sin, sign * off_cos, sign * off_sin)


def _retention_tables(n, rows):
    log_g = jnp.log(1.0 - 2.0 ** (-5.0 - jnp.arange(RET_HEADS, dtype=F32)))
    steps = (jnp.arange(rows) % n).astype(F32) + 1.0
    expo = steps[:, None] * jnp.repeat(log_g, RET_KEY_DIM)[None, :]
    return (jnp.exp(expo), jnp.exp(-expo) * (RET_KEY_DIM ** -0.5)), jnp.exp(n * log_g)


def kernel(x_prompt, x_sample, c_prompt, c_sample, cache_win_k, cache_win_v, state_ret, g_mix, g_mlp, w_ada,
           b_ada, w_in, w_out, att_sinks, rel_bias, w_up, w_down, g_final):
    depth = w_in.shape[0]
    bsz, seq, d = x_prompt.shape
    dbsz, dec_seq, _ = x_sample.shape
    n_cache = cache_win_k.shape[2]
    n_rows = dbsz * dec_seq
    keep_p = min(WINDOW, seq)

    w_in_b = w_in[0].astype(BF16)
    g_mix3 = g_mix.reshape(depth, 1, d)
    g_mlp3 = g_mlp.reshape(depth, 1, d)
    cache_k = cache_win_k.reshape(depth, dbsz, n_cache, ATT_KV_W)
    cache_v = cache_win_v.reshape(depth, dbsz, n_cache, ATT_KV_W)

    c_all = jnp.concatenate([c_prompt, c_sample], axis=0)
    c_all = jnp.pad(c_all, ((0, -c_all.shape[0] % 8), (0, 0)))
    mods = _ada_call(c_all, w_ada, b_ada)
    mods_p = mods[:, :bsz, None, :]
    mods_s = jnp.repeat(mods[:, bsz:bsz + dbsz], dec_seq, axis=1)

    t_p = min(TOKEN_BLOCK, seq)
    rot_p = _rotary_tables(jnp.arange(seq // t_p) * t_p, jnp.arange(t_p))
    rot_s = _rotary_tables(jnp.full((1,), PAST_LEN), jnp.tile(jnp.arange(dec_seq), dbsz))
    decay_p, cdec_p = _retention_tables(GROUP, t_p)
    decay_s, cdec_s = _retention_tables(dec_seq, n_rows)

    rel_p = (np.arange(BAND) - BAND_CHUNKS * CHUNK)[None, :] - np.arange(CHUNK)[:, None]
    bias_p = _banded_bias(_bias_call(rel_bias, jnp.asarray(_rel_bucket(rel_p))) * LOG2_E)
    kpos = PAST_LEN - n_cache + np.arange(n_cache + dec_seq)
    qpos = PAST_LEN + np.arange(dec_seq)
    bias_s = _bias_call(rel_bias, jnp.asarray(_rel_bucket(kpos[None, :] - qpos[:, None]))) * LOG2_E
    bias_s = jnp.pad(bias_s, ((0, 0), (0, 0), (0, SAMPLE_KEYS - n_cache - dec_seq)), constant_values=-jnp.inf)
    bias_s = _pair_rows(bias_s)

    xp = x_prompt
    xs = x_sample.reshape(1, n_rows, d)
    pk, pv, pr = [], [], []
    stacks = None
    for l in range(depth):
        sinks = att_sinks[l].astype(F32)[:, None, None] * LOG2_E
        sink_p = _pair_lanes(jnp.broadcast_to(sinks, (ATT_HEADS, 1, CHUNK)))
        sink_s = _pair_rows(jnp.broadcast_to(sinks, (ATT_HEADS, dec_seq, 1)))
        final = l == depth - 1

        qa, kx, va, qd, kg, vr, sg, kvt = _inproj_call(xp, mods_p, g_mix3, w_in_b, l, rot_p, decay_p, keep_p)
        to_round = [(w_out, l), (w_up, l), (w_down, l)] + ([] if final else [(w_in, l + 1)])
        mixed, st, rounded = _mix_prompt_call(qa, kx, va, qd, kg, vr, sg, bias_p, sink_p, cdec_p, to_round)
        w_out_b, w_up_b, w_down_b = rounded[:3]
        w_in_next = None if final else rounded[3]
        xp = _dense_call(xp, mixed, mods_p, g_mlp3, g_final, w_out_b, w_up_b, w_down_b, l, final)
        pk.append(kvt[:, :, :ATT_KV_W].reshape(bsz, keep_p, ATT_KV_HEADS, ATT_HEAD_DIM))
        pv.append(kvt[:, :, ATT_KV_W:].reshape(bsz, keep_p, ATT_KV_HEADS, ATT_HEAD_DIM))
        pr.append(st)

        qa, _, _, qd, kg, vr, sg, kvn = _inproj_call(xs, mods_s, g_mix3, w_in_b, l, rot_s, decay_s, n_rows)
        per_seq = lambda a: a.reshape(dbsz, dec_seq, a.shape[-1])
        mixed, stacks = _mix_sample_call(per_seq(qa), per_seq(kvn), cache_k, cache_v, per_seq(qd), per_seq(kg),
                                         per_seq(vr), per_seq(sg), state_ret, l, stacks, bias_s, sink_s, cdec_s)
        xs = _dense_call(xs, mixed.reshape(1, n_rows, d), mods_s, g_mlp3, g_final, w_out_b, w_up_b, w_down_b, l,
                         final)
        w_in_b = w_in_next

    sr, sk, sv = stacks
    win_shape = (depth, dbsz, n_cache, ATT_KV_HEADS, ATT_HEAD_DIM)
    return (xp, xs.reshape(dbsz, dec_seq, d), jnp.stack(pk), jnp.stack(pv), jnp.stack(pr),
            sk.reshape(win_shape), sv.reshape(win_shape), sr)
```

```python
import functools
import math

import jax
import jax.numpy as jnp
import numpy as np
from jax import lax
from jax.experimental import pallas as pl
from jax.experimental.pallas import tpu as pltpu

F32 = jnp.float32
BF16 = jnp.bfloat16

D_MODEL = 1024
CHUNK = 64
ATT_HEADS = 8
ATT_KV_HEADS = 2
ATT_HEAD_DIM = 64
ATT_Q_W = ATT_HEADS * ATT_HEAD_DIM
ATT_KV_W = ATT_KV_HEADS * ATT_HEAD_DIM
ATT_SCALE = ATT_HEAD_DIM ** -0.5
WINDOW = 128
BAND_CHUNKS = -(-WINDOW // CHUNK)
BAND = (BAND_CHUNKS + 1) * CHUNK
REL_BUCKETS = 32
REL_MAX_DIST = 128
RET_HEADS = 4
RET_KEY_DIM = 128
RET_VAL_DIM = 128
RET_W = RET_HEADS * RET_KEY_DIM
ROPE_BASE = 10000.0
IN_W = ATT_Q_W + 2 * ATT_KV_W + 4 * RET_W
D_FF = 4 * D_MODEL
N_MOD = 6
EPS = 1e-6
PAST_LEN = 4096
LOG2_E = math.log2(math.e)

LANES = 128
HALF_LANES = LANES // 2
BF16_SUBLANES = 16
ONES_ROWS = BF16_SUBLANES
VMEM_LIMIT_BYTES = 56 * 1024 * 1024

OFF_QA = 0
OFF_KV = OFF_QA + ATT_Q_W
OFF_QR = OFF_KV + 2 * ATT_KV_W
OFF_KR = OFF_QR + RET_W
OFF_VR = OFF_KR + RET_W
OFF_GR = OFF_VR + RET_W

TOKEN_BLOCK = 1024
INPROJ_SUB_BLOCK = 256
MIX_TOKEN_BLOCK = 1024
GROUP = 128
FF_BLOCK = 1024
ADA_COL_BLOCK = 1024
SAMPLE_GROUP = 4
SAMPLE_KEYS = 2 * LANES


def _params(*sem):
    return pltpu.CompilerParams(dimension_semantics=sem, vmem_limit_bytes=VMEM_LIMIT_BYTES)


def _const_spec(shape):
    zeros = (0,) * len(shape)
    return pl.BlockSpec(shape, lambda *_: zeros, pipeline_mode=pl.Buffered(1))


def _rms(x):
    return x * lax.rsqrt(jnp.mean(x * x, axis=-1, keepdims=True) + EPS)


def _dot(a, b):
    return jnp.dot(a, b, preferred_element_type=F32)


def _dot_nt(a, b):
    return lax.dot_general(a, b, (((1,), (1,)), ((), ())), preferred_element_type=F32)


def _dot_tn(a, b):
    return lax.dot_general(a, b, (((0,), (0,)), ((), ())), preferred_element_type=F32)


def _ada_kernel(c_ref, w_ref, b_ref, o_ref):
    a = jax.nn.silu(c_ref[...]).astype(BF16)
    o_ref[0] = _dot(a, w_ref[0].astype(BF16)) + b_ref[0]


def _ada_call(c_all, w_ada, b_ada):
    depth, d, n = w_ada.shape
    rows = c_all.shape[0]
    return pl.pallas_call(
        _ada_kernel,
        grid=(depth, n // ADA_COL_BLOCK),
        in_specs=[
            pl.BlockSpec((rows, d), lambda l, j: (0, 0)),
            pl.BlockSpec((1, d, ADA_COL_BLOCK), lambda l, j: (l, 0, j)),
            pl.BlockSpec((1, 1, ADA_COL_BLOCK), lambda l, j: (l, 0, j)),
        ],
        out_specs=pl.BlockSpec((1, rows, ADA_COL_BLOCK), lambda l, j: (l, 0, j)),
        out_shape=jax.ShapeDtypeStruct((depth, rows, n), F32),
        compiler_params=_params("arbitrary", "arbitrary"),
        name="ada",
    )(c_all, w_ada, b_ada.reshape(depth, 1, n))


def _bias_kernel(tab_ref, idx_ref, o_ref):
    h = pl.program_id(0)
    idx = idx_ref[...]
    acc = jnp.zeros(idx.shape, F32)
    for b in range(REL_BUCKETS):
        acc = jnp.where(idx == b, tab_ref[b, h], acc)
    o_ref[0] = acc


def _bias_call(rel_bias, bucket):
    q, j = bucket.shape
    return pl.pallas_call(
        _bias_kernel,
        grid=(ATT_HEADS,),
        in_specs=[
            pl.BlockSpec(memory_space=pltpu.SMEM),
            pl.BlockSpec((q, j), lambda h: (0, 0)),
        ],
        out_specs=pl.BlockSpec((1, q, j), lambda h: (h, 0, 0)),
        out_shape=jax.ShapeDtypeStruct((ATT_HEADS, q, j), F32),
        compiler_params=_params("arbitrary"),
        name="rel_bias",
    )(rel_bias, bucket)


def _rel_bucket(rel):
    nb = REL_BUCKETS // 2
    n = -np.asarray(rel, np.int64)
    ret = np.where(n < 0, nb, 0)
    n = np.abs(n)
    max_exact = nb // 2
    nf = np.maximum(n, 1).astype(np.float64)
    large = max_exact + (np.log(nf / max_exact) / math.log(REL_MAX_DIST / max_exact)
                         * (nb - max_exact)).astype(np.int64)
    large = np.minimum(large, nb - 1)
    return (ret + np.where(n < max_exact, n, large)).astype(np.int32)


def _pair_rows(per_head):
    out = []
    for k in range(ATT_KV_HEADS):
        for e in range(2):
            out.append(jnp.concatenate([per_head[4 * k + e], per_head[4 * k + 2 + e]], axis=0))
    return jnp.stack(out)


def _pair_lanes(per_head):
    out = []
    for k in range(ATT_KV_HEADS):
        for e in range(2):
            out.append(jnp.concatenate([per_head[4 * k + e], per_head[4 * k + 2 + e]], axis=-1))
    return jnp.stack(out)


def _banded_bias(per_head):
    base = _pair_lanes(jnp.swapaxes(per_head, 1, 2))
    key = jnp.arange(BAND)[None, :, None]
    return jnp.stack([jnp.where(key < (BAND_CHUNKS - v) * CHUNK, -jnp.inf, base)
                      for v in range(BAND_CHUNKS + 1)])


def _rotary(u, cos2, sin2):
    parts = []
    for h in range(RET_HEADS):
        x = u[:, h * RET_KEY_DIM:(h + 1) * RET_KEY_DIM]
        parts.append(x * cos2 + pltpu.roll(x, HALF_LANES, 1) * sin2)
    return jnp.concatenate(parts, axis=1)


def _inproj_kernel(x_ref, sh_ref, sc_ref, g_ref, w_ref, ac_ref, as_ref, bc_ref, bs_ref, bcn_ref, bsn_ref,
                   qdec_ref, kdec_ref,
                   qa_ref, kx_ref, va_ref, qd_ref, kg_ref, vr_ref, sg_ref, kvt_ref, *, tail):
    i = pl.program_id(1)
    t = x_ref.shape[1]
    sub = min(INPROJ_SUB_BLOCK, t)
    for r0 in range(0, t, sub):
        rows = slice(r0, r0 + sub)
        per_row = lambda ref: ref[rows] if ref.shape[0] == t else ref[...]
        x = x_ref[0, rows]
        h = (_rms(x) * g_ref[...]) * (1.0 + per_row(sc_ref)) + per_row(sh_ref)
        hb = h.astype(BF16)
        cos2 = ac_ref[...] * bc_ref[rows] - as_ref[...] * bs_ref[rows]
        sin2 = as_ref[...] * bcn_ref[rows] + ac_ref[...] * bsn_ref[rows]

        qa_ref[0, rows] = (_dot(hb, w_ref[:, OFF_QA:OFF_KV]) * (ATT_SCALE * LOG2_E)).astype(BF16)
        kv = _dot(hb, w_ref[:, OFF_KV:OFF_QR])
        kx_ref[0, rows] = jnp.concatenate(_lane_split(kv[:, :ATT_KV_W]), axis=1).astype(BF16)
        va_ref[0, rows] = kv[:, ATT_KV_W:].astype(BF16)
        qd_ref[0, rows] = (_rotary(_dot(hb, w_ref[:, OFF_QR:OFF_KR]), cos2, sin2) * qdec_ref[rows]).astype(BF16)
        kg_ref[0, rows] = (_rotary(_dot(hb, w_ref[:, OFF_KR:OFF_VR]), cos2, sin2) * kdec_ref[rows]).astype(BF16)
        vr_ref[0, rows] = _dot(hb, w_ref[:, OFF_VR:OFF_GR]).astype(BF16)
        sg_ref[0, rows] = jax.nn.silu(_dot(hb, w_ref[:, OFF_GR:IN_W])).astype(BF16)

        lo = max(r0, t - tail)
        if lo < r0 + sub:
            @pl.when(i == pl.num_programs(1) - 1)
            def _(kv=kv, lo=lo, r0=r0):
                kvt_ref[0, lo - (t - tail):r0 + sub - (t - tail)] = kv[lo - r0:]


MOD_SHIFT1, MOD_SCALE1, MOD_GATE1, MOD_SHIFT2, MOD_SCALE2, MOD_GATE2 = range(N_MOD)


def _mod_specs(mods, layer, which, t):
    if mods.ndim == 4:
        return [pl.BlockSpec((None, None, 1, D_MODEL), lambda b, i, m=m: (layer, b, 0, m)) for m in which]
    return [pl.BlockSpec((None, t, D_MODEL), lambda b, i, m=m: (layer, i, m)) for m in which]


def _layer_spec(arr, layer):
    tail_zeros = (0,) * (arr.ndim - 1)
    return pl.BlockSpec((None,) + arr.shape[1:], lambda *_: (layer,) + tail_zeros, pipeline_mode=pl.Buffered(1))


def _inproj_call(x, mods, gains, w_in, layer, rot, decay, tail):
    bsz, seq, d = x.shape
    t = min(TOKEN_BLOCK, seq)
    nb = seq // t
    base_cos, base_sin, off = rot
    assert base_cos.shape == (nb, 1, LANES) and off[0].shape == (t, LANES)
    assert all(a.shape == (t, RET_W) for a in decay)
    tok = lambda w: pl.BlockSpec((1, t, w), lambda b, i: (b, i, 0))
    out_bf = lambda w: jax.ShapeDtypeStruct((bsz, seq, w), BF16)
    base = pl.BlockSpec((None, 1, LANES), lambda b, i: (i, 0, 0))
    return pl.pallas_call(
        functools.partial(_inproj_kernel, tail=tail),
        grid=(bsz, nb),
        in_specs=[
            tok(d), *_mod_specs(mods, layer, (MOD_SHIFT1, MOD_SCALE1), t),
            _layer_spec(gains, layer), _const_spec(w_in.shape),
            base, base, *[_const_spec((t, LANES))] * 4, *[_const_spec((t, RET_W))] * 2,
        ],
        out_specs=[
            tok(ATT_Q_W), tok(4 * LANES), tok(ATT_KV_W), tok(RET_W), tok(RET_W), tok(RET_W), tok(RET_W),
            pl.BlockSpec((1, tail, 2 * ATT_KV_W), lambda b, i: (b, 0, 0)),
        ],
        out_shape=[
            out_bf(ATT_Q_W), out_bf(4 * LANES), out_bf(ATT_KV_W), out_bf(RET_W), out_bf(RET_W),
            out_bf(RET_W), out_bf(RET_W),
            jax.ShapeDtypeStruct((bsz, tail, 2 * ATT_KV_W), F32),
        ],
        compiler_params=_params("arbitrary", "arbitrary"),
        name="inproj",
    )(x, mods, mods, gains, w_in, base_cos, base_sin, *off, *decay)


def _lane_split(x):
    lo = lax.broadcasted_iota(jnp.int32, x.shape, 1) < HALF_LANES
    sw = pltpu.roll(x, HALF_LANES, 1)
    zero = jnp.zeros_like(x)
    return (jnp.where(lo, x, zero), jnp.where(lo, zero, sw), jnp.where(lo, sw, zero), jnp.where(lo, zero, x))


def _sink_softmax2(s, sink):
    m = jnp.maximum(jnp.max(s, axis=-1, keepdims=True), sink)
    e = jnp.exp2(s - m)
    den = jnp.sum(e, axis=-1, keepdims=True) + jnp.exp2(sink - m)
    return e / den


def _mix_prompt_kernel(cdec_ref, qa_ref, ka_ref, va_ref, qd_ref, kg_ref, vr_ref, sg_ref, bias_ref, sink_ref,
                       *rest, t, n_cast):
    cast_src = rest[:n_cast]
    mixed_ref, state_ref = rest[n_cast:n_cast + 2]
    cast_dst = rest[n_cast + 2:2 * n_cast + 2]
    kx_ref, vt_ref, s_ref = rest[2 * n_cast + 2:]
    for src, dst in zip(cast_src, cast_dst):
        dst[...] = src[...].astype(BF16)

    i = pl.program_id(1)
    n_groups = t // GROUP
    hist = BAND - CHUNK

    hd = ATT_HEAD_DIM

    @pl.when(i == 0)
    def _():
        kx_ref[:, 0:hist, :] = jnp.zeros((4, hist, LANES), BF16)
        vt_ref[0, :, 0:hd, 0:GROUP] = jnp.zeros((ATT_KV_HEADS, hd, GROUP), BF16)
        vt_ref[:, :, hd:, :] = jnp.ones((n_groups, ATT_KV_HEADS, ONES_ROWS, 2 * GROUP), BF16)
        s_ref[...] = jnp.zeros(s_ref.shape, F32)

    @pl.when(i > 0)
    def _():
        kx_ref[:, 0:hist, :] = kx_ref[:, t:t + hist, :]
        vt_ref[0, :, 0:hd, 0:GROUP] = vt_ref[n_groups - 1, :, 0:hd, GROUP:]

    for n in range(4):
        kx_ref[n, hist:hist + t, :] = ka_ref[0, :, n * LANES:(n + 1) * LANES]
    for m in range(n_groups):
        tile = va_ref[0, m * GROUP:(m + 1) * GROUP, :].astype(F32).T.astype(BF16)
        for k in range(ATT_KV_HEADS):
            vt_ref[m, k, 0:hd, GROUP:] = tile[k * hd:(k + 1) * hd]
            if m + 1 < n_groups:
                vt_ref[m + 1, k, 0:hd, 0:GROUP] = tile[k * hd:(k + 1) * hd]

    no_keys = jnp.zeros((CHUNK, LANES), BF16)
    causal = (lax.broadcasted_iota(jnp.int32, (GROUP, GROUP), 0)
              >= lax.broadcasted_iota(jnp.int32, (GROUP, GROUP), 1))
    heads = [slice(h * LANES, (h + 1) * LANES) for h in range(RET_HEADS)]
    att = [(j, k, e) for j in range(GROUP // CHUNK) for k in range(ATT_KV_HEADS) for e in range(2)]

    def lead(g):
        grows = slice(g * GROUP, (g + 1) * GROUP)
        raw = {}
        for j in range(GROUP // CHUNK):
            r0 = g * GROUP + j * CHUNK
            for k in range(ATT_KV_HEADS):
                c0 = 2 * LANES * k
                q2 = jnp.concatenate([qa_ref[0, r0:r0 + CHUNK, c0:c0 + LANES],
                                      qa_ref[0, r0:r0 + CHUNK, c0 + LANES:c0 + 2 * LANES]], axis=0)
                keys = jnp.concatenate([kx_ref[2 * k, r0:r0 + BAND, :], kx_ref[2 * k + 1, r0:r0 + BAND, :]], axis=0)
                s2 = _dot_nt(keys, q2)
                raw[j, k, 0], raw[j, k, 1] = s2[:BAND], s2[BAND:]
        states = [s_ref[h] for h in range(RET_HEADS)]
        ret_q = [qd_ref[0, grows, c] for c in heads]
        ret_k = [kg_ref[0, grows, c] for c in heads]
        ret_v = [vr_ref[0, grows, c] for c in heads]
        ret_scores = [_dot_nt(ret_q[h], ret_k[h]) for h in range(RET_HEADS)]
        for h in range(RET_HEADS):
            s_ref[h] = cdec_ref[h] * (states[h] + _dot_tn(ret_k[h], ret_v[h]))
        return raw, states, ret_q, ret_v, ret_scores

    def finish(g, lead_out):
        raw, states, ret_q, ret_v, ret_scores = lead_out
        grows = slice(g * GROUP, (g + 1) * GROUP)
        chunk_rows = lambda j: slice(g * GROUP + j * CHUNK, g * GROUP + (j + 1) * CHUNK)
        probs, sink_terms = {}, {}
        for j, k, e in att:
            n = 2 * k + e
            variant = jnp.minimum(i * (t // CHUNK) + 2 * g + j, BAND_CHUNKS)
            s = raw[j, k, e] + bias_ref[variant, n]
            sink = sink_ref[n]
            m = jnp.maximum(jnp.max(s, axis=0, keepdims=True), sink)
            probs[j, k, e] = jnp.exp2(s - m).astype(BF16)
            sink_terms[j, k, e] = jnp.exp2(sink - m)
        ret_p = [jnp.where(causal, ret_scores[h], 0.0).astype(BF16) for h in range(RET_HEADS)]

        out_t = {}
        for j, k, e in att:
            frame = [probs[j, k, e], no_keys] if j == 0 else [no_keys, probs[j, k, e]]
            out_t[j, k, e] = _dot(vt_ref[g, k], jnp.concatenate(frame, axis=0))
        ret_out = [_dot(jnp.concatenate([ret_p[h], ret_q[h]], axis=1),
                        jnp.concatenate([ret_v[h], states[h].astype(BF16)], axis=0)) for h in range(RET_HEADS)]

        for j in range(GROUP // CHUNK):
            for k in range(ATT_KV_HEADS):
                c0 = 2 * LANES * k
                halves = []
                for e in range(2):
                    o_t = out_t[j, k, e]
                    den = o_t[ATT_HEAD_DIM:ATT_HEAD_DIM + 1] + sink_terms[j, k, e]
                    halves.append(o_t[:ATT_HEAD_DIM] * (1.0 / den))
                o = jnp.concatenate(halves, axis=0).T
                mixed_ref[0, chunk_rows(j), c0:c0 + LANES] = o[:CHUNK].astype(BF16)
                mixed_ref[0, chunk_rows(j), c0 + LANES:c0 + 2 * LANES] = o[CHUNK:].astype(BF16)
        for h in range(RET_HEADS):
            out = _rms(ret_out[h]) * sg_ref[0, grows, heads[h]].astype(F32)
            mixed_ref[0, grows, ATT_Q_W + h * LANES:ATT_Q_W + (h + 1) * LANES] = out.astype(BF16)

    for g in range(n_groups):
        finish(g, lead(g))

    @pl.when(i == pl.num_programs(1) - 1)
    def _():
        state_ref[0] = s_ref[...]


def _mix_prompt_call(qa, kx, va, qd, kg, vr, sg, bias4, sink4, cdec, weights):
    bsz, seq, _ = qa.shape
    t = min(MIX_TOKEN_BLOCK, seq)
    nb = seq // t
    hist = BAND - CHUNK
    assert hist == GROUP and t % GROUP == 0 and seq % t == 0
    tok = lambda w: pl.BlockSpec((1, t, w), lambda b, i: (b, i, 0))
    steps = bsz * nb
    slabs, slab_in, slab_out, slab_shape = [], [], [], []
    for stack, layer in weights:
        depth, rows, cols = stack.shape
        assert rows % (steps * BF16_SUBLANES) == 0
        slabs.append(stack.reshape(depth, steps, rows // steps, cols))
        slab_in.append(pl.BlockSpec((None, None, rows // steps, cols), lambda b, i, layer=layer: (layer, b * nb + i, 0, 0)))
        slab_out.append(pl.BlockSpec((None, rows // steps, cols), lambda b, i: (b * nb + i, 0, 0)))
        slab_shape.append(jax.ShapeDtypeStruct((steps, rows // steps, cols), BF16))
    outs = pl.pallas_call(
        functools.partial(_mix_prompt_kernel, t=t, n_cast=len(weights)),
        grid=(bsz, nb),
        in_specs=[
            pl.BlockSpec(memory_space=pltpu.SMEM),
            tok(ATT_Q_W), tok(4 * LANES), tok(ATT_KV_W), tok(RET_W), tok(RET_W), tok(RET_W), tok(RET_W),
            _const_spec(bias4.shape), _const_spec(sink4.shape),
        ] + slab_in,
        out_specs=[
            tok(D_MODEL),
            pl.BlockSpec((1, RET_HEADS, RET_KEY_DIM, RET_VAL_DIM), lambda b, i: (b, 0, 0, 0)),
        ] + slab_out,
        out_shape=[
            jax.ShapeDtypeStruct((bsz, seq, D_MODEL), BF16),
            jax.ShapeDtypeStruct((bsz, RET_HEADS, RET_KEY_DIM, RET_VAL_DIM), F32),
        ] + slab_shape,
        scratch_shapes=[
            pltpu.VMEM((4, hist + t, LANES), BF16),
            pltpu.VMEM((t // GROUP, ATT_KV_HEADS, ATT_HEAD_DIM + ONES_ROWS, 2 * GROUP), BF16),
            pltpu.VMEM((RET_HEADS, RET_KEY_DIM, RET_VAL_DIM), F32),
        ],
        compiler_params=_params("arbitrary", "arbitrary"),
        name="mix_prompt",
    )(cdec, qa, kx, va, qd, kg, vr, sg, bias4, sink4, *slabs)
    rounded = [o.reshape(stack.shape[1:]) for o, (stack, _) in zip(outs[2:], weights)]
    return outs[0], outs[1], rounded


def _mix_sample_kernel(cdec_ref, qa_ref, kvn_ref, ck_ref, cv_ref, qd_ref, kg_ref, vr_ref, sg_ref, s0_ref,
                       bias_ref, sink_ref, *rest, n_new, n_cache):
    mixed_ref, *stack_refs = rest[-4:]
    done = 0
    if len(rest) > 4:
        done = rest[0].shape[0]
        for prev_ref, out_ref in zip(rest[:3], stack_refs):
            out_ref[0:done] = prev_ref[...]
    state_ref, wk_ref, wv_ref = (r.at[done] for r in stack_refs)
    pad = jnp.zeros((SAMPLE_KEYS - n_cache - n_new, LANES), F32)
    causal = (lax.broadcasted_iota(jnp.int32, (n_new, n_new), 0)
              >= lax.broadcasted_iota(jnp.int32, (n_new, n_new), 1))
    seqs = range(SAMPLE_GROUP)
    heads = [slice(h * LANES, (h + 1) * LANES) for h in range(RET_HEADS)]
    att = [(g, k, e) for g in seqs for k in range(ATT_KV_HEADS) for e in range(2)]
    ret = [(g, h) for g in seqs for h in range(RET_HEADS)]

    v4, raw = {}, {}
    for g in seqs:
        kvn = kvn_ref[g]
        k_all = jnp.concatenate([ck_ref[g], kvn[:, :ATT_KV_W], pad], axis=0)
        v_all = jnp.concatenate([cv_ref[g], kvn[:, ATT_KV_W:], pad], axis=0)
        wk_ref[g] = k_all[n_new:n_new + n_cache]
        wv_ref[g] = v_all[n_new:n_new + n_cache]
        k4 = [p.astype(BF16) for p in _lane_split(k_all)]
        v4[g] = [p.astype(BF16) for p in _lane_split(v_all)]
        for k in range(ATT_KV_HEADS):
            c0 = 2 * LANES * k
            q2 = jnp.concatenate([qa_ref[g, :, c0:c0 + LANES], qa_ref[g, :, c0 + LANES:c0 + 2 * LANES]], axis=0)
            for e in range(2):
                raw[g, k, e] = _dot_nt(q2, k4[2 * k + e])
    qd = {(g, h): qd_ref[g, :, heads[h]] for g, h in ret}
    kg = {(g, h): kg_ref[g, :, heads[h]] for g, h in ret}
    vr = {(g, h): vr_ref[g, :, heads[h]] for g, h in ret}
    state = {(g, h): s0_ref[g, h] for g, h in ret}
    ret_scores = {gh: _dot_nt(qd[gh], kg[gh]) for gh in ret}
    ret_inter = {gh: _dot(qd[gh], state[gh].astype(BF16)) for gh in ret}
    for g, h in ret:
        state_ref[g, h] = cdec_ref[h] * (state[g, h] + _dot_tn(kg[g, h], vr[g, h]))

    probs = {(g, k, e): _sink_softmax2(raw[g, k, e] + bias_ref[2 * k + e], sink_ref[2 * k + e]).astype(BF16)
             for g, k, e in att}
    ret_p = {gh: jnp.where(causal, ret_scores[gh], 0.0).astype(BF16) for gh in ret}

    out = {(g, k, e): _dot(probs[g, k, e], v4[g][2 * k + e]) for g, k, e in att}
    ret_intra = {gh: _dot(ret_p[gh], vr[gh]) for gh in ret}

    for g in seqs:
        for k in range(ATT_KV_HEADS):
            c0 = 2 * LANES * k
            o = out[g, k, 0] + out[g, k, 1]
            mixed_ref[g, :, c0:c0 + LANES] = o[:n_new].astype(BF16)
            mixed_ref[g, :, c0 + LANES:c0 + 2 * LANES] = o[n_new:].astype(BF16)
        for h in range(RET_HEADS):
            ro = _rms(ret_intra[g, h] + ret_inter[g, h]) * sg_ref[g, :, heads[h]].astype(F32)
            mixed_ref[g, :, ATT_Q_W + h * LANES:ATT_Q_W + (h + 1) * LANES] = ro.astype(BF16)


def _mix_sample_call(qa, kvn, cache_k, cache_v, qd, kg, vr, sg, state, layer, stacks, bias4, sink4, cdec):
    dbsz, n_new, _ = qa.shape
    n_cache = cache_k.shape[2]
    grp = lambda *tail: pl.BlockSpec((SAMPLE_GROUP,) + tail, lambda b: (b,) + (0,) * len(tail))
    lay = lambda *tail: pl.BlockSpec((None, SAMPLE_GROUP) + tail, lambda b: (layer, b) + (0,) * len(tail))
    stk = lambda n, *tail: pl.BlockSpec((n, SAMPLE_GROUP) + tail, lambda b: (0, b) + (0,) * len(tail))
    stack_tails = [(RET_HEADS, RET_KEY_DIM, RET_VAL_DIM), (n_cache, ATT_KV_W), (n_cache, ATT_KV_W)]
    operands = [cdec, qa, kvn, cache_k, cache_v, qd, kg, vr, sg, state, bias4, sink4]
    prev_specs = []
    if stacks is not None:
        assert all(s.shape[0] == layer for s in stacks)
        operands += list(stacks)
        prev_specs = [stk(layer, *tail) for tail in stack_tails]
    outs = pl.pallas_call(
        functools.partial(_mix_sample_kernel, n_new=n_new, n_cache=n_cache),
        grid=(dbsz // SAMPLE_GROUP,),
        in_specs=[
            pl.BlockSpec(memory_space=pltpu.SMEM),
            grp(n_new, ATT_Q_W), grp(n_new, 2 * ATT_KV_W), lay(n_cache, ATT_KV_W), lay(n_cache, ATT_KV_W),
            grp(n_new, RET_W), grp(n_new, RET_W), grp(n_new, RET_W), grp(n_new, RET_W),
            lay(RET_HEADS, RET_KEY_DIM, RET_VAL_DIM),
            _const_spec(bias4.shape), _const_spec(sink4.shape),
        ] + prev_specs,
        out_specs=[grp(n_new, D_MODEL)] + [stk(layer + 1, *tail) for tail in stack_tails],
        out_shape=[jax.ShapeDtypeStruct((dbsz, n_new, D_MODEL), BF16)]
        + [jax.ShapeDtypeStruct((layer + 1, dbsz) + tail, F32) for tail in stack_tails],
        compiler_params=_params("arbitrary"),
        name="mix_sample",
    )(*operands)
    return outs[0], outs[1:]


def _dense_kernel(x_ref, mixed_ref, ga1_ref, sh2_ref, sc2_ref, ga2_ref, g_ref, gf_ref,
                  wo_ref, wu_ref, wd_ref, o_ref, *, final):
    x1 = x_ref[0] + ga1_ref[...] * _dot(mixed_ref[0], wo_ref[...])
    h = ((_rms(x1) * g_ref[...]) * (1.0 + sc2_ref[...]) + sh2_ref[...]).astype(BF16)
    acc = jnp.zeros(x1.shape, F32)
    for j in range(D_FF // FF_BLOCK):
        u = _dot(h, wu_ref[:, j * FF_BLOCK:(j + 1) * FF_BLOCK])
        u = jnp.square(jnp.maximum(u, 0.0)).astype(BF16)
        acc = acc + _dot(u, wd_ref[j * FF_BLOCK:(j + 1) * FF_BLOCK, :])
    x2 = x1 + ga2_ref[...] * acc
    o_ref[0] = _rms(x2) * gf_ref[...] if final else x2


def _dense_call(x, mixed, mods, gains, g_final, w_out, w_up, w_down, layer, final):
    bsz, seq, d = x.shape
    t = min(TOKEN_BLOCK, seq)
    tok = lambda w: pl.BlockSpec((1, t, w), lambda b, i: (b, i, 0))
    return pl.pallas_call(
        functools.partial(_dense_kernel, final=final),
        grid=(bsz, seq // t),
        in_specs=[
            tok(d), tok(d), *_mod_specs(mods, layer, (MOD_GATE1, MOD_SHIFT2, MOD_SCALE2, MOD_GATE2), t),
            _layer_spec(gains, layer), _const_spec((1, d)),
            _const_spec(w_out.shape), _const_spec(w_up.shape), _const_spec(w_down.shape),
        ],
        out_specs=tok(d),
        out_shape=jax.ShapeDtypeStruct((bsz, seq, d), F32),
        compiler_params=_params("arbitrary", "arbitrary"),
        name="dense",
    )(x, mixed, mods, mods, mods, mods, gains, g_final.reshape(1, d), w_out, w_up, w_down)


def _rotary_tables(base_pos, off_pos):
    half = RET_KEY_DIM // 2
    inv = 1.0 / (ROPE_BASE ** (np.arange(half, dtype=np.float64) / half))
    sign = np.concatenate([-np.ones((half,)), np.ones((half,))])

    def both(pos):
        ang = np.asarray(pos, np.float64)[:, None] * inv[None, :]
        cos, sin = np.cos(ang), np.sin(ang)
        return np.concatenate([cos, cos], axis=1), np.concatenate([sin, sin], axis=1)

    base_cos, base_sin = both(base_pos)
    off_cos, off_sin = both(off_pos)
    f32 = lambda a: jnp.asarray(a, F32)
    return (f32(base_cos[:, None, :]), f32(base_sin[:, None, :]),
            tuple(f32(a) for a in (off_cos, off_sin, sign * off_cos, sign * off_sin)))


def _retention_tables(n, rows):
    log_g = np.log(1.0 - 2.0 ** (-5.0 - np.arange(RET_HEADS, dtype=np.float64)))
    steps = (np.arange(rows) % n).astype(np.float64) + 1.0
    expo = steps[:, None] * np.repeat(log_g, RET_KEY_DIM)[None, :]
    f32 = lambda a: jnp.asarray(a, F32)
    return (f32(np.exp(expo)), f32(np.exp(-expo) * (RET_KEY_DIM ** -0.5))), f32(np.exp(n * log_g))


def kernel(x_prompt, x_sample, c_prompt, c_sample, cache_win_k, cache_win_v, state_ret, g_mix, g_mlp, w_ada,
           b_ada, w_in, w_out, att_sinks, rel_bias, w_up, w_down, g_final):
    depth = w_in.shape[0]
    bsz, seq, d = x_prompt.shape
    dbsz, dec_seq, _ = x_sample.shape
    n_cache = cache_win_k.shape[2]
    n_rows = dbsz * dec_seq
    keep_p = min(WINDOW, seq)

    w_in_b = w_in[0].astype(BF16)
    g_mix3 = g_mix.reshape(depth, 1, d)
    g_mlp3 = g_mlp.reshape(depth, 1, d)
    cache_k = cache_win_k.reshape(depth, dbsz, n_cache, ATT_KV_W)
    cache_v = cache_win_v.reshape(depth, dbsz, n_cache, ATT_KV_W)

    c_all = jnp.concatenate([c_prompt, c_sample], axis=0)
    c_all = jnp.pad(c_all, ((0, -c_all.shape[0] % 8), (0, 0)))
    mods = _ada_call(c_all, w_ada, b_ada)
    mods_p = mods[:, :bsz, None, :]
    mods_s = jnp.repeat(mods[:, bsz:bsz + dbsz], dec_seq, axis=1)

    t_p = min(TOKEN_BLOCK, seq)
    rot_p = _rotary_tables(np.arange(seq // t_p) * t_p, np.arange(t_p))
    rot_s = _rotary_tables(np.full((1,), PAST_LEN), np.tile(np.arange(dec_seq), dbsz))
    decay_p, cdec_p = _retention_tables(GROUP, t_p)
    decay_s, cdec_s = _retention_tables(dec_seq, n_rows)

    rel_p = (np.arange(BAND) - BAND_CHUNKS * CHUNK)[None, :] - np.arange(CHUNK)[:, None]
    bias_p = _banded_bias(_bias_call(rel_bias, jnp.asarray(_rel_bucket(rel_p))) * LOG2_E)
    kpos = PAST_LEN - n_cache + np.arange(n_cache + dec_seq)
    qpos = PAST_LEN + np.arange(dec_seq)
    bias_s = _bias_call(rel_bias, jnp.asarray(_rel_bucket(kpos[None, :] - qpos[:, None]))) * LOG2_E
    bias_s = jnp.pad(bias_s, ((0, 0), (0, 0), (0, SAMPLE_KEYS - n_cache - dec_seq)), constant_values=-jnp.inf)
    bias_s = _pair_rows(bias_s)

    xp = x_prompt
    xs = x_sample.reshape(1, n_rows, d)
    pk, pv, pr = [], [], []
    stacks = None
    for l in range(depth):
        sinks = att_sinks[l].astype(F32)[:, None, None] * LOG2_E
        sink_p = _pair_lanes(jnp.broadcast_to(sinks, (ATT_HEADS, 1, CHUNK)))
        sink_s = _pair_rows(jnp.broadcast_to(sinks, (ATT_HEADS, dec_seq, 1)))
        final = l == depth - 1

        qa, kx, va, qd, kg, vr, sg, kvt = _inproj_call(xp, mods_p, g_mix3, w_in_b, l, rot_p, decay_p, keep_p)
        to_round = [(w_out, l), (w_up, l), (w_down, l)] + ([] if final else [(w_in, l + 1)])
        mixed, st, rounded = _mix_prompt_call(qa, kx, va, qd, kg, vr, sg, bias_p, sink_p, cdec_p, to_round)
        w_out_b, w_up_b, w_down_b = rounded[:3]
        w_in_next = None if final else rounded[3]
        xp = _dense_call(xp, mixed, mods_p, g_mlp3, g_final, w_out_b, w_up_b, w_down_b, l, final)
        pk.append(kvt[:, :, :ATT_KV_W].reshape(bsz, keep_p, ATT_KV_HEADS, ATT_HEAD_DIM))
        pv.append(kvt[:, :, ATT_KV_W:].reshape(bsz, keep_p, ATT_KV_HEADS, ATT_HEAD_DIM))
        pr.append(st)

        qa, _, _, qd, kg, vr, sg, kvn = _inproj_call(xs, mods_s, g_mix3, w_in_b, l, rot_s, decay_s, n_rows)
        per_seq = lambda a: a.reshape(dbsz, dec_seq, a.shape[-1])
        mixed, stacks = _mix_sample_call(per_seq(qa), per_seq(kvn), cache_k, cache_v, per_seq(qd), per_seq(kg),
                                         per_seq(vr), per_seq(sg), state_ret, l, stacks, bias_s, sink_s, cdec_s)
        xs = _dense_call(xs, mixed.reshape(1, n_rows, d), mods_s, g_mlp3, g_final, w_out_b, w_up_b, w_down_b, l,
                         final)
        w_in_b = w_in_next

    sr, sk, sv = stacks
    win_shape = (depth, dbsz, n_cache, ATT_KV_HEADS, ATT_HEAD_DIM)
    return (xp, xs.reshape(dbsz, dec_seq, d), jnp.stack(pk), jnp.stack(pv), jnp.stack(pr),
            sk.reshape(win_shape), sv.reshape(win_shape), sr)
```

```python
import functools
import math

import jax
import jax.numpy as jnp
import numpy as np
from jax import lax
from jax.experimental import pallas as pl
from jax.experimental.pallas import tpu as pltpu

F32 = jnp.float32
BF16 = jnp.bfloat16

D_MODEL = 1024
CHUNK = 64
ATT_HEADS = 8
ATT_KV_HEADS = 2
ATT_HEAD_DIM = 64
ATT_Q_W = ATT_HEADS * ATT_HEAD_DIM
ATT_KV_W = ATT_KV_HEADS * ATT_HEAD_DIM
ATT_SCALE = ATT_HEAD_DIM ** -0.5
WINDOW = 128
BAND_CHUNKS = -(-WINDOW // CHUNK)
BAND = (BAND_CHUNKS + 1) * CHUNK
REL_BUCKETS = 32
REL_MAX_DIST = 128
RET_HEADS = 4
RET_KEY_DIM = 128
RET_VAL_DIM = 128
RET_W = RET_HEADS * RET_KEY_DIM
ROPE_BASE = 10000.0
IN_W = ATT_Q_W + 2 * ATT_KV_W + 4 * RET_W
D_FF = 4 * D_MODEL
N_MOD = 6
EPS = 1e-6
PAST_LEN = 4096
LOG2_E = math.log2(math.e)

LANES = 128
HALF_LANES = LANES // 2
BF16_SUBLANES = 16
ONES_ROWS = BF16_SUBLANES
VMEM_LIMIT_BYTES = 56 * 1024 * 1024

OFF_QA = 0
OFF_KV = OFF_QA + ATT_Q_W
OFF_QR = OFF_KV + 2 * ATT_KV_W
OFF_KR = OFF_QR + RET_W
OFF_VR = OFF_KR + RET_W
OFF_GR = OFF_VR + RET_W

TOKEN_BLOCK = 1024
INPROJ_SUB_BLOCK = 256
MIX_TOKEN_BLOCK = 1024
GROUP = 128
FF_BLOCK = 1024
ADA_COL_BLOCK = 1024
SAMPLE_GROUP = 4
SAMPLE_KEYS = 2 * LANES


def _params(*sem):
    return pltpu.CompilerParams(dimension_semantics=sem, vmem_limit_bytes=VMEM_LIMIT_BYTES)


def _const_spec(shape):
    zeros = (0,) * len(shape)
    return pl.BlockSpec(shape, lambda *_: zeros, pipeline_mode=pl.Buffered(1))


def _rms(x):
    return x * lax.rsqrt(jnp.mean(x * x, axis=-1, keepdims=True) + EPS)


def _dot(a, b):
    return jnp.dot(a, b, preferred_element_type=F32)


def _dot_nt(a, b):
    return lax.dot_general(a, b, (((1,), (1,)), ((), ())), preferred_element_type=F32)


def _dot_tn(a, b):
    return lax.dot_general(a, b, (((0,), (0,)), ((), ())), preferred_element_type=F32)


def _ada_kernel(c_ref, w_ref, b_ref, o_ref):
    a = jax.nn.silu(c_ref[...]).astype(BF16)
    o_ref[0] = _dot(a, w_ref[0].astype(BF16)) + b_ref[0]


def _ada_call(c_all, w_ada, b_ada):
    depth, d, n = w_ada.shape
    rows = c_all.shape[0]
    return pl.pallas_call(
        _ada_kernel,
        grid=(depth, n // ADA_COL_BLOCK),
        in_specs=[
            pl.BlockSpec((rows, d), lambda l, j: (0, 0)),
            pl.BlockSpec((1, d, ADA_COL_BLOCK), lambda l, j: (l, 0, j)),
            pl.BlockSpec((1, 1, ADA_COL_BLOCK), lambda l, j: (l, 0, j)),
        ],
        out_specs=pl.BlockSpec((1, rows, ADA_COL_BLOCK), lambda l, j: (l, 0, j)),
        out_shape=jax.ShapeDtypeStruct((depth, rows, n), F32),
        compiler_params=_params("arbitrary", "arbitrary"),
        name="ada",
    )(c_all, w_ada, b_ada.reshape(depth, 1, n))


def _bias_kernel(tab_ref, idx_ref, o_ref):
    h = pl.program_id(0)
    idx = idx_ref[...]
    acc = jnp.zeros(idx.shape, F32)
    for b in range(REL_BUCKETS):
        acc = jnp.where(idx == b, tab_ref[b, h], acc)
    o_ref[0] = acc


def _bias_call(rel_bias, bucket):
    q, j = bucket.shape
    return pl.pallas_call(
        _bias_kernel,
        grid=(ATT_HEADS,),
        in_specs=[
            pl.BlockSpec(memory_space=pltpu.SMEM),
            pl.BlockSpec((q, j), lambda h: (0, 0)),
        ],
        out_specs=pl.BlockSpec((1, q, j), lambda h: (h, 0, 0)),
        out_shape=jax.ShapeDtypeStruct((ATT_HEADS, q, j), F32),
        compiler_params=_params("arbitrary"),
        name="rel_bias",
    )(rel_bias, bucket)


def _rel_bucket(rel):
    nb = REL_BUCKETS // 2
    n = -np.asarray(rel, np.int64)
    ret = np.where(n < 0, nb, 0)
    n = np.abs(n)
    max_exact = nb // 2
    nf = np.maximum(n, 1).astype(np.float64)
    large = max_exact + (np.log(nf / max_exact) / math.log(REL_MAX_DIST / max_exact)
                         * (nb - max_exact)).astype(np.int64)
    large = np.minimum(large, nb - 1)
    return (ret + np.where(n < max_exact, n, large)).astype(np.int32)


def _pair_rows(per_head):
    out = []
    for k in range(ATT_KV_HEADS):
        for e in range(2):
            out.append(jnp.concatenate([per_head[4 * k + e], per_head[4 * k + 2 + e]], axis=0))
    return jnp.stack(out)


def _pair_lanes(per_head):
    out = []
    for k in range(ATT_KV_HEADS):
        for e in range(2):
            out.append(jnp.concatenate([per_head[4 * k + e], per_head[4 * k + 2 + e]], axis=-1))
    return jnp.stack(out)


def _banded_bias(per_head):
    base = _pair_lanes(jnp.swapaxes(per_head, 1, 2))
    key = jnp.arange(BAND)[None, :, None]
    return jnp.stack([jnp.where(key < (BAND_CHUNKS - v) * CHUNK, -jnp.inf, base)
                      for v in range(BAND_CHUNKS + 1)])


def _rotary(u, cos2, sin2):
    parts = []
    for h in range(RET_HEADS):
        x = u[:, h * RET_KEY_DIM:(h + 1) * RET_KEY_DIM]
        parts.append(x * cos2 + pltpu.roll(x, HALF_LANES, 1) * sin2)
    return jnp.concatenate(parts, axis=1)


def _inproj_kernel(x_ref, sh_ref, sc_ref, g_ref, w_ref, ac_ref, as_ref, bc_ref, bs_ref, bcn_ref, bsn_ref,
                   qdec_ref, kdec_ref,
                   qa_ref, kx_ref, va_ref, qd_ref, kg_ref, vr_ref, sg_ref, kvt_ref, *, tail):
    i = pl.program_id(1)
    t = x_ref.shape[1]
    sub = min(INPROJ_SUB_BLOCK, t)
    for r0 in range(0, t, sub):
        rows = slice(r0, r0 + sub)
        per_row = lambda ref: ref[rows] if ref.shape[0] == t else ref[...]
        x = x_ref[0, rows]
        h = (_rms(x) * g_ref[...]) * (1.0 + per_row(sc_ref)) + per_row(sh_ref)
        hb = h.astype(BF16)
        cos2 = ac_ref[...] * bc_ref[rows] - as_ref[...] * bs_ref[rows]
        sin2 = as_ref[...] * bcn_ref[rows] + ac_ref[...] * bsn_ref[rows]

        qd_ref[0, rows] = (_rotary(_dot(hb, w_ref[:, OFF_QR:OFF_KR]), cos2, sin2) * qdec_ref[rows]).astype(BF16)
        kg_ref[0, rows] = (_rotary(_dot(hb, w_ref[:, OFF_KR:OFF_VR]), cos2, sin2) * kdec_ref[rows]).astype(BF16)
        sg_ref[0, rows] = jax.nn.silu(_dot(hb, w_ref[:, OFF_GR:IN_W])).astype(BF16)
        kv = _dot(hb, w_ref[:, OFF_KV:OFF_QR])
        kx_ref[0, rows] = jnp.concatenate(_lane_split(kv[:, :ATT_KV_W]), axis=1).astype(BF16)
        va_ref[0, rows] = kv[:, ATT_KV_W:].astype(BF16)
        qa_ref[0, rows] = (_dot(hb, w_ref[:, OFF_QA:OFF_KV]) * (ATT_SCALE * LOG2_E)).astype(BF16)
        vr_ref[0, rows] = _dot(hb, w_ref[:, OFF_VR:OFF_GR]).astype(BF16)

        lo = max(r0, t - tail)
        if lo < r0 + sub:
            @pl.when(i == pl.num_programs(1) - 1)
            def _(kv=kv, lo=lo, r0=r0):
                kvt_ref[0, lo - (t - tail):r0 + sub - (t - tail)] = kv[lo - r0:]


MOD_SHIFT1, MOD_SCALE1, MOD_GATE1, MOD_SHIFT2, MOD_SCALE2, MOD_GATE2 = range(N_MOD)


def _mod_specs(mods, layer, which, t):
    if mods.ndim == 4:
        return [pl.BlockSpec((None, None, 1, D_MODEL), lambda b, i, m=m: (layer, b, 0, m)) for m in which]
    return [pl.BlockSpec((None, t, D_MODEL), lambda b, i, m=m: (layer, i, m)) for m in which]


def _layer_spec(arr, layer):
    tail_zeros = (0,) * (arr.ndim - 1)
    return pl.BlockSpec((None,) + arr.shape[1:], lambda *_: (layer,) + tail_zeros, pipeline_mode=pl.Buffered(1))


def _inproj_call(x, mods, gains, w_in, layer, rot, decay, tail):
    bsz, seq, d = x.shape
    t = min(TOKEN_BLOCK, seq)
    nb = seq // t
    base_cos, base_sin, off = rot
    assert base_cos.shape == (nb, 1, LANES) and off[0].shape == (t, LANES)
    assert all(a.shape == (t, RET_W) for a in decay)
    tok = lambda w: pl.BlockSpec((1, t, w), lambda b, i: (b, i, 0))
    out_bf = lambda w: jax.ShapeDtypeStruct((bsz, seq, w), BF16)
    base = pl.BlockSpec((None, 1, LANES), lambda b, i: (i, 0, 0))
    return pl.pallas_call(
        functools.partial(_inproj_kernel, tail=tail),
        grid=(bsz, nb),
        in_specs=[
            tok(d), *_mod_specs(mods, layer, (MOD_SHIFT1, MOD_SCALE1), t),
            _layer_spec(gains, layer), _const_spec(w_in.shape),
            base, base, *[_const_spec((t, LANES))] * 4, *[_const_spec((t, RET_W))] * 2,
        ],
        out_specs=[
            tok(ATT_Q_W), tok(4 * LANES), tok(ATT_KV_W), tok(RET_W), tok(RET_W), tok(RET_W), tok(RET_W),
            pl.BlockSpec((1, tail, 2 * ATT_KV_W), lambda b, i: (b, 0, 0)),
        ],
        out_shape=[
            out_bf(ATT_Q_W), out_bf(4 * LANES), out_bf(ATT_KV_W), out_bf(RET_W), out_bf(RET_W),
            out_bf(RET_W), out_bf(RET_W),
            jax.ShapeDtypeStruct((bsz, tail, 2 * ATT_KV_W), F32),
        ],
        compiler_params=_params("arbitrary", "arbitrary"),
        name="inproj",
    )(x, mods, mods, gains, w_in, base_cos, base_sin, *off, *decay)


def _lane_split(x):
    lo = lax.broadcasted_iota(jnp.int32, x.shape, 1) < HALF_LANES
    sw = pltpu.roll(x, HALF_LANES, 1)
    zero = jnp.zeros_like(x)
    return (jnp.where(lo, x, zero), jnp.where(lo, zero, sw), jnp.where(lo, sw, zero), jnp.where(lo, zero, x))


def _sink_softmax2(s, sink):
    m = jnp.maximum(jnp.max(s, axis=-1, keepdims=True), sink)
    e = jnp.exp2(s - m)
    den = jnp.sum(e, axis=-1, keepdims=True) + jnp.exp2(sink - m)
    return e / den


def _mix_prompt_kernel(cdec_ref, qa_ref, ka_ref, va_ref, qd_ref, kg_ref, vr_ref, sg_ref, bias_ref, sink_ref,
                       *rest, t, n_cast):
    cast_src = rest[:n_cast]
    mixed_ref, state_ref = rest[n_cast:n_cast + 2]
    cast_dst = rest[n_cast + 2:2 * n_cast + 2]
    kx_ref, vt_ref, s_ref = rest[2 * n_cast + 2:]
    for src, dst in zip(cast_src, cast_dst):
        dst[...] = src[...].astype(BF16)

    i = pl.program_id(1)
    n_groups = t // GROUP
    hist = BAND - CHUNK

    hd = ATT_HEAD_DIM

    @pl.when(i == 0)
    def _():
        kx_ref[:, 0:hist, :] = jnp.zeros((4, hist, LANES), BF16)
        vt_ref[0, :, 0:hd, 0:GROUP] = jnp.zeros((ATT_KV_HEADS, hd, GROUP), BF16)
        vt_ref[:, :, hd:, :] = jnp.ones((n_groups, ATT_KV_HEADS, ONES_ROWS, 2 * GROUP), BF16)
        s_ref[...] = jnp.zeros(s_ref.shape, F32)

    @pl.when(i > 0)
    def _():
        kx_ref[:, 0:hist, :] = kx_ref[:, t:t + hist, :]
        vt_ref[0, :, 0:hd, 0:GROUP] = vt_ref[n_groups - 1, :, 0:hd, GROUP:]

    for n in range(4):
        kx_ref[n, hist:hist + t, :] = ka_ref[0, :, n * LANES:(n + 1) * LANES]
    for m in range(n_groups):
        tile = va_ref[0, m * GROUP:(m + 1) * GROUP, :].astype(F32).T.astype(BF16)
        for k in range(ATT_KV_HEADS):
            vt_ref[m, k, 0:hd, GROUP:] = tile[k * hd:(k + 1) * hd]
            if m + 1 < n_groups:
                vt_ref[m + 1, k, 0:hd, 0:GROUP] = tile[k * hd:(k + 1) * hd]

    no_keys = jnp.zeros((CHUNK, LANES), BF16)
    causal = (lax.broadcasted_iota(jnp.int32, (GROUP, GROUP), 0)
              >= lax.broadcasted_iota(jnp.int32, (GROUP, GROUP), 1))
    heads = [slice(h * LANES, (h + 1) * LANES) for h in range(RET_HEADS)]
    att = [(j, k, e) for j in range(GROUP // CHUNK) for k in range(ATT_KV_HEADS) for e in range(2)]

    def lead(g):
        grows = slice(g * GROUP, (g + 1) * GROUP)
        raw = {}
        for j in range(GROUP // CHUNK):
            r0 = g * GROUP + j * CHUNK
            for k in range(ATT_KV_HEADS):
                c0 = 2 * LANES * k
                q2 = jnp.concatenate([qa_ref[0, r0:r0 + CHUNK, c0:c0 + LANES],
                                      qa_ref[0, r0:r0 + CHUNK, c0 + LANES:c0 + 2 * LANES]], axis=0)
                keys = jnp.concatenate([kx_ref[2 * k, r0:r0 + BAND, :], kx_ref[2 * k + 1, r0:r0 + BAND, :]], axis=0)
                s2 = _dot_nt(keys, q2)
                raw[j, k, 0], raw[j, k, 1] = s2[:BAND], s2[BAND:]
        states = [s_ref[h] for h in range(RET_HEADS)]
        ret_q = [qd_ref[0, grows, c] for c in heads]
        ret_k = [kg_ref[0, grows, c] for c in heads]
        ret_v = [vr_ref[0, grows, c] for c in heads]
        ret_scores = [_dot_nt(ret_q[h], ret_k[h]) for h in range(RET_HEADS)]
        for h in range(RET_HEADS):
            s_ref[h] = cdec_ref[h] * (states[h] + _dot_tn(ret_k[h], ret_v[h]))
        return raw, states, ret_q, ret_v, ret_scores

    def finish(g, lead_out):
        raw, states, ret_q, ret_v, ret_scores = lead_out
        grows = slice(g * GROUP, (g + 1) * GROUP)
        chunk_rows = lambda j: slice(g * GROUP + j * CHUNK, g * GROUP + (j + 1) * CHUNK)
        probs, sink_terms = {}, {}
        for j, k, e in att:
            n = 2 * k + e
            variant = jnp.minimum(i * (t // CHUNK) + 2 * g + j, BAND_CHUNKS)
            s = raw[j, k, e] + bias_ref[variant, n]
            sink = sink_ref[n]
            m = jnp.maximum(jnp.max(s, axis=0, keepdims=True), sink)
            probs[j, k, e] = jnp.exp2(s - m).astype(BF16)
            sink_terms[j, k, e] = jnp.exp2(sink - m)
        ret_p = [jnp.where(causal, ret_scores[h], 0.0).astype(BF16) for h in range(RET_HEADS)]

        out_t = {}
        for j, k, e in att:
            frame = [probs[j, k, e], no_keys] if j == 0 else [no_keys, probs[j, k, e]]
            out_t[j, k, e] = _dot(vt_ref[g, k], jnp.concatenate(frame, axis=0))
        ret_out = [_dot(jnp.concatenate([ret_p[h], ret_q[h]], axis=1),
                        jnp.concatenate([ret_v[h], states[h].astype(BF16)], axis=0)) for h in range(RET_HEADS)]

        for j in range(GROUP // CHUNK):
            for k in range(ATT_KV_HEADS):
                c0 = 2 * LANES * k
                halves = []
                for e in range(2):
                    o_t = out_t[j, k, e]
                    den = o_t[ATT_HEAD_DIM:ATT_HEAD_DIM + 1] + sink_terms[j, k, e]
                    halves.append(o_t[:ATT_HEAD_DIM] * (1.0 / den))
                o = jnp.concatenate(halves, axis=0).T
                mixed_ref[0, chunk_rows(j), c0:c0 + LANES] = o[:CHUNK].astype(BF16)
                mixed_ref[0, chunk_rows(j), c0 + LANES:c0 + 2 * LANES] = o[CHUNK:].astype(BF16)
        for h in range(RET_HEADS):
            out = _rms(ret_out[h]) * sg_ref[0, grows, heads[h]].astype(F32)
            mixed_ref[0, grows, ATT_Q_W + h * LANES:ATT_Q_W + (h + 1) * LANES] = out.astype(BF16)

    for g in range(n_groups):
        finish(g, lead(g))

    @pl.when(i == pl.num_programs(1) - 1)
    def _():
        state_ref[0] = s_ref[...]


def _mix_prompt_call(qa, kx, va, qd, kg, vr, sg, bias4, sink4, cdec, weights):
    bsz, seq, _ = qa.shape
    t = min(MIX_TOKEN_BLOCK, seq)
    nb = seq // t
    hist = BAND - CHUNK
    assert hist == GROUP and t % GROUP == 0 and seq % t == 0
    tok = lambda w: pl.BlockSpec((1, t, w), lambda b, i: (b, i, 0))
    steps = bsz * nb
    slabs, slab_in, slab_out, slab_shape = [], [], [], []
    for stack, layer in weights:
        depth, rows, cols = stack.shape
        assert rows % (steps * BF16_SUBLANES) == 0
        slabs.append(stack.reshape(depth, steps, rows // steps, cols))
        slab_in.append(pl.BlockSpec((None, None, rows // steps, cols), lambda b, i, layer=layer: (layer, b * nb + i, 0, 0)))
        slab_out.append(pl.BlockSpec((None, rows // steps, cols), lambda b, i: (b * nb + i, 0, 0)))
        slab_shape.append(jax.ShapeDtypeStruct((steps, rows // steps, cols), BF16))
    outs = pl.pallas_call(
        functools.partial(_mix_prompt_kernel, t=t, n_cast=len(weights)),
        grid=(bsz, nb),
        in_specs=[
            pl.BlockSpec(memory_space=pltpu.SMEM),
            tok(ATT_Q_W), tok(4 * LANES), tok(ATT_KV_W), tok(RET_W), tok(RET_W), tok(RET_W), tok(RET_W),
            _const_spec(bias4.shape), _const_spec(sink4.shape),
        ] + slab_in,
        out_specs=[
            tok(D_MODEL),
            pl.BlockSpec((1, RET_HEADS, RET_KEY_DIM, RET_VAL_DIM), lambda b, i: (b, 0, 0, 0)),
        ] + slab_out,
        out_shape=[
            jax.ShapeDtypeStruct((bsz, seq, D_MODEL), BF16),
            jax.ShapeDtypeStruct((bsz, RET_HEADS, RET_KEY_DIM, RET_VAL_DIM), F32),
        ] + slab_shape,
        scratch_shapes=[
            pltpu.VMEM((4, hist + t, LANES), BF16),
            pltpu.VMEM((t // GROUP, ATT_KV_HEADS, ATT_HEAD_DIM + ONES_ROWS, 2 * GROUP), BF16),
            pltpu.VMEM((RET_HEADS, RET_KEY_DIM, RET_VAL_DIM), F32),
        ],
        compiler_params=_params("arbitrary", "arbitrary"),
        name="mix_prompt",
    )(cdec, qa, kx, va, qd, kg, vr, sg, bias4, sink4, *slabs)
    rounded = [o.reshape(stack.shape[1:]) for o, (stack, _) in zip(outs[2:], weights)]
    return outs[0], outs[1], rounded


def _mix_sample_kernel(cdec_ref, qa_ref, kvn_ref, ck_ref, cv_ref, qd_ref, kg_ref, vr_ref, sg_ref, s0_ref,
                       bias_ref, sink_ref, *rest, n_new, n_cache):
    mixed_ref, *stack_refs = rest[-4:]
    done = 0
    if len(rest) > 4:
        done = rest[0].shape[0]
        for prev_ref, out_ref in zip(rest[:3], stack_refs):
            out_ref[0:done] = prev_ref[...]
    state_ref, wk_ref, wv_ref = (r.at[done] for r in stack_refs)
    pad = jnp.zeros((SAMPLE_KEYS - n_cache - n_new, LANES), F32)
    causal = (lax.broadcasted_iota(jnp.int32, (n_new, n_new), 0)
              >= lax.broadcasted_iota(jnp.int32, (n_new, n_new), 1))
    seqs = range(SAMPLE_GROUP)
    heads = [slice(h * LANES, (h + 1) * LANES) for h in range(RET_HEADS)]
    att = [(g, k, e) for g in seqs for k in range(ATT_KV_HEADS) for e in range(2)]
    ret = [(g, h) for g in seqs for h in range(RET_HEADS)]

    v4, raw = {}, {}
    for g in seqs:
        kvn = kvn_ref[g]
        k_all = jnp.concatenate([ck_ref[g], kvn[:, :ATT_KV_W], pad], axis=0)
        v_all = jnp.concatenate([cv_ref[g], kvn[:, ATT_KV_W:], pad], axis=0)
        wk_ref[g] = k_all[n_new:n_new + n_cache]
        wv_ref[g] = v_all[n_new:n_new + n_cache]
        k4 = [p.astype(BF16) for p in _lane_split(k_all)]
        v4[g] = [p.astype(BF16) for p in _lane_split(v_all)]
        for k in range(ATT_KV_HEADS):
            c0 = 2 * LANES * k
            q2 = jnp.concatenate([qa_ref[g, :, c0:c0 + LANES], qa_ref[g, :, c0 + LANES:c0 + 2 * LANES]], axis=0)
            for e in range(2):
                raw[g, k, e] = _dot_nt(q2, k4[2 * k + e])
    qd = {(g, h): qd_ref[g, :, heads[h]] for g, h in ret}
    kg = {(g, h): kg_ref[g, :, heads[h]] for g, h in ret}
    vr = {(g, h): vr_ref[g, :, heads[h]] for g, h in ret}
    state = {(g, h): s0_ref[g, h] for g, h in ret}
    ret_scores = {gh: _dot_nt(qd[gh], kg[gh]) for gh in ret}
    ret_inter = {gh: _dot(qd[gh], state[gh].astype(BF16)) for gh in ret}
    for g, h in ret:
        state_ref[g, h] = cdec_ref[h] * (state[g, h] + _dot_tn(kg[g, h], vr[g, h]))

    probs = {(g, k, e): _sink_softmax2(raw[g, k, e] + bias_ref[2 * k + e], sink_ref[2 * k + e]).astype(BF16)
             for g, k, e in att}
    ret_p = {gh: jnp.where(causal, ret_scores[gh], 0.0).astype(BF16) for gh in ret}

    out = {(g, k, e): _dot(probs[g, k, e], v4[g][2 * k + e]) for g, k, e in att}
    ret_intra = {gh: _dot(ret_p[gh], vr[gh]) for gh in ret}

    for g in seqs:
        for k in range(ATT_KV_HEADS):
            c0 = 2 * LANES * k
            o = out[g, k, 0] + out[g, k, 1]
            mixed_ref[g, :, c0:c0 + LANES] = o[:n_new].astype(BF16)
            mixed_ref[g, :, c0 + LANES:c0 + 2 * LANES] = o[n_new:].astype(BF16)
        for h in range(RET_HEADS):
            ro = _rms(ret_intra[g, h] + ret_inter[g, h]) * sg_ref[g, :, heads[h]].astype(F32)
            mixed_ref[g, :, ATT_Q_W + h * LANES:ATT_Q_W + (h + 1) * LANES] = ro.astype(BF16)


def _mix_sample_call(qa, kvn, cache_k, cache_v, qd, kg, vr, sg, state, layer, stacks, bias4, sink4, cdec):
    dbsz, n_new, _ = qa.shape
    n_cache = cache_k.shape[2]
    grp = lambda *tail: pl.BlockSpec((SAMPLE_GROUP,) + tail, lambda b: (b,) + (0,) * len(tail))
    lay = lambda *tail: pl.BlockSpec((None, SAMPLE_GROUP) + tail, lambda b: (layer, b) + (0,) * len(tail))
    stk = lambda n, *tail: pl.BlockSpec((n, SAMPLE_GROUP) + tail, lambda b: (0, b) + (0,) * len(tail))
    stack_tails = [(RET_HEADS, RET_KEY_DIM, RET_VAL_DIM), (n_cache, ATT_KV_W), (n_cache, ATT_KV_W)]
    operands = [cdec, qa, kvn, cache_k, cache_v, qd, kg, vr, sg, state, bias4, sink4]
    prev_specs = []
    if stacks is not None:
        assert all(s.shape[0] == layer for s in stacks)
        operands += list(stacks)
        prev_specs = [stk(layer, *tail) for tail in stack_tails]
    outs = pl.pallas_call(
        functools.partial(_mix_sample_kernel, n_new=n_new, n_cache=n_cache),
        grid=(dbsz // SAMPLE_GROUP,),
        in_specs=[
            pl.BlockSpec(memory_space=pltpu.SMEM),
            grp(n_new, ATT_Q_W), grp(n_new, 2 * ATT_KV_W), lay(n_cache, ATT_KV_W), lay(n_cache, ATT_KV_W),
            grp(n_new, RET_W), grp(n_new, RET_W), grp(n_new, RET_W), grp(n_new, RET_W),
            lay(RET_HEADS, RET_KEY_DIM, RET_VAL_DIM),
            _const_spec(bias4.shape), _const_spec(sink4.shape),
        ] + prev_specs,
        out_specs=[grp(n_new, D_MODEL)] + [stk(layer + 1, *tail) for tail in stack_tails],
        out_shape=[jax.ShapeDtypeStruct((dbsz, n_new, D_MODEL), BF16)]
        + [jax.ShapeDtypeStruct((layer + 1, dbsz) + tail, F32) for tail in stack_tails],
        compiler_params=_params("arbitrary"),
        name="mix_sample",
    )(*operands)
    return outs[0], outs[1:]


def _dense_kernel(x_ref, mixed_ref, ga1_ref, sh2_ref, sc2_ref, ga2_ref, g_ref, gf_ref,
                  wo_ref, wu_ref, wd_ref, o_ref, *, final):
    x1 = x_ref[0] + ga1_ref[...] * _dot(mixed_ref[0], wo_ref[...])
    h = ((_rms(x1) * g_ref[...]) * (1.0 + sc2_ref[...]) + sh2_ref[...]).astype(BF16)
    acc = jnp.zeros(x1.shape, F32)
    for j in range(D_FF // FF_BLOCK):
        u = _dot(h, wu_ref[:, j * FF_BLOCK:(j + 1) * FF_BLOCK])
        u = jnp.square(jnp.maximum(u, 0.0)).astype(BF16)
        acc = acc + _dot(u, wd_ref[j * FF_BLOCK:(j + 1) * FF_BLOCK, :])
    x2 = x1 + ga2_ref[...] * acc
    o_ref[0] = _rms(x2) * gf_ref[...] if final else x2


def _dense_call(x, mixed, mods, gains, g_final, w_out, w_up, w_down, layer, final):
    bsz, seq, d = x.shape
    t = min(TOKEN_BLOCK, seq)
    tok = lambda w: pl.BlockSpec((1, t, w), lambda b, i: (b, i, 0))
    return pl.pallas_call(
        functools.partial(_dense_kernel, final=final),
        grid=(bsz, seq // t),
        in_specs=[
            tok(d), tok(d), *_mod_specs(mods, layer, (MOD_GATE1, MOD_SHIFT2, MOD_SCALE2, MOD_GATE2), t),
            _layer_spec(gains, layer), _const_spec((1, d)),
            _const_spec(w_out.shape), _const_spec(w_up.shape), _const_spec(w_down.shape),
        ],
        out_specs=tok(d),
        out_shape=jax.ShapeDtypeStruct((bsz, seq, d), F32),
        compiler_params=_params("arbitrary", "arbitrary"),
        name="dense",
    )(x, mixed, mods, mods, mods, mods, gains, g_final.reshape(1, d), w_out, w_up, w_down)


def _rotary_tables(base_pos, off_pos):
    half = RET_KEY_DIM // 2
    inv = 1.0 / (ROPE_BASE ** (np.arange(half, dtype=np.float64) / half))
    sign = np.concatenate([-np.ones((half,)), np.ones((half,))])

    def both(pos):
        ang = np.asarray(pos, np.float64)[:, None] * inv[None, :]
        cos, sin = np.cos(ang), np.sin(ang)
        return np.concatenate([cos, cos], axis=1), np.concatenate([sin, sin], axis=1)

    base_cos, base_sin = both(base_pos)
    off_cos, off_sin = both(off_pos)
    f32 = lambda a: jnp.asarray(a, F32)
    return (f32(base_cos[:, None, :]), f32(base_sin[:, None, :]),
            tuple(f32(a) for a in (off_cos, off_sin, sign * off_cos, sign * off_sin)))


def _retention_tables(n, rows):
    log_g = np.log(1.0 - 2.0 ** (-5.0 - np.arange(RET_HEADS, dtype=np.float64)))
    steps = (np.arange(rows) % n).astype(np.float64) + 1.0
    expo = steps[:, None] * np.repeat(log_g, RET_KEY_DIM)[None, :]
    f32 = lambda a: jnp.asarray(a, F32)
    return (f32(np.exp(expo)), f32(np.exp(-expo) * (RET_KEY_DIM ** -0.5))), f32(np.exp(n * log_g))


def kernel(x_prompt, x_sample, c_prompt, c_sample, cache_win_k, cache_win_v, state_ret, g_mix, g_mlp, w_ada,
           b_ada, w_in, w_out, att_sinks, rel_bias, w_up, w_down, g_final):
    depth = w_in.shape[0]
    bsz, seq, d = x_prompt.shape
    dbsz, dec_seq, _ = x_sample.shape
    n_cache = cache_win_k.shape[2]
    n_rows = dbsz * dec_seq
    keep_p = min(WINDOW, seq)

    w_in_b = w_in[0].astype(BF16)
    g_mix3 = g_mix.reshape(depth, 1, d)
    g_mlp3 = g_mlp.reshape(depth, 1, d)
    cache_k = cache_win_k.reshape(depth, dbsz, n_cache, ATT_KV_W)
    cache_v = cache_win_v.reshape(depth, dbsz, n_cache, ATT_KV_W)

    c_all = jnp.concatenate([c_prompt, c_sample], axis=0)
    c_all = jnp.pad(c_all, ((0, -c_all.shape[0] % 8), (0, 0)))
    mods = _ada_call(c_all, w_ada, b_ada)
    mods_p = mods[:, :bsz, None, :]
    mods_s = jnp.repeat(mods[:, bsz:bsz + dbsz], dec_seq, axis=1)

    t_p = min(TOKEN_BLOCK, seq)
    rot_p = _rotary_tables(np.arange(seq // t_p) * t_p, np.arange(t_p))
    rot_s = _rotary_tables(np.full((1,), PAST_LEN), np.tile(np.arange(dec_seq), dbsz))
    decay_p, cdec_p = _retention_tables(GROUP, t_p)
    decay_s, cdec_s = _retention_tables(dec_seq, n_rows)

    rel_p = (np.arange(BAND) - BAND_CHUNKS * CHUNK)[None, :] - np.arange(CHUNK)[:, None]
    bias_p = _banded_bias(_bias_call(rel_bias, jnp.asarray(_rel_bucket(rel_p))) * LOG2_E)
    kpos = PAST_LEN - n_cache + np.arange(n_cache + dec_seq)
    qpos = PAST_LEN + np.arange(dec_seq)
    bias_s = _bias_call(rel_bias, jnp.asarray(_rel_bucket(kpos[None, :] - qpos[:, None]))) * LOG2_E
    bias_s = jnp.pad(bias_s, ((0, 0), (0, 0), (0, SAMPLE_KEYS - n_cache - dec_seq)), constant_values=-jnp.inf)
    bias_s = _pair_rows(bias_s)

    xp = x_prompt
    xs = x_sample.reshape(1, n_rows, d)
    pk, pv, pr = [], [], []
    stacks = None
    for l in range(depth):
        sinks = att_sinks[l].astype(F32)[:, None, None] * LOG2_E
        sink_p = _pair_lanes(jnp.broadcast_to(sinks, (ATT_HEADS, 1, CHUNK)))
        sink_s = _pair_rows(jnp.broadcast_to(sinks, (ATT_HEADS, dec_seq, 1)))
        final = l == depth - 1

        qa, kx, va, qd, kg, vr, sg, kvt = _inproj_call(xp, mods_p, g_mix3, w_in_b, l, rot_p, decay_p, keep_p)
        to_round = [(w_out, l), (w_up, l), (w_down, l)] + ([] if final else [(w_in, l + 1)])
        mixed, st, rounded = _mix_prompt_call(qa, kx, va, qd, kg, vr, sg, bias_p, sink_p, cdec_p, to_round)
        w_out_b, w_up_b, w_down_b = rounded[:3]
        w_in_next = None if final else rounded[3]
        xp = _dense_call(xp, mixed, mods_p, g_mlp3, g_final, w_out_b, w_up_b, w_down_b, l, final)
        pk.append(kvt[:, :, :ATT_KV_W].reshape(bsz, keep_p, ATT_KV_HEADS, ATT_HEAD_DIM))
        pv.append(kvt[:, :, ATT_KV_W:].reshape(bsz, keep_p, ATT_KV_HEADS, ATT_HEAD_DIM))
        pr.append(st)

        qa, _, _, qd, kg, vr, sg, kvn = _inproj_call(xs, mods_s, g_mix3, w_in_b, l, rot_s, decay_s, n_rows)
        per_seq = lambda a: a.reshape(dbsz, dec_seq, a.shape[-1])
        mixed, stacks = _mix_sample_call(per_seq(qa), per_seq(kvn), cache_k, cache_v, per_seq(qd), per_seq(kg),
                                         per_seq(vr), per_seq(sg), state_ret, l, stacks, bias_s, sink_s, cdec_s)
        xs = _dense_call(xs, mixed.reshape(1, n_rows, d), mods_s, g_mlp3, g_final, w_out_b, w_up_b, w_down_b, l,
                         final)
        w_in_b = w_in_next

    sr, sk, sv = stacks
    win_shape = (depth, dbsz, n_cache, ATT_KV_HEADS, ATT_HEAD_DIM)
    return (xp, xs.reshape(dbsz, dec_seq, d), jnp.stack(pk), jnp.stack(pv), jnp.stack(pr),
            sk.reshape(win_shape), sv.reshape(win_shape), sr)
```

```python
import functools
import math

import jax
import jax.numpy as jnp
import numpy as np
from jax import lax
from jax.experimental import pallas as pl
from jax.experimental.pallas import tpu as pltpu

F32 = jnp.float32
BF16 = jnp.bfloat16

D_MODEL = 1024
CHUNK = 64
ATT_HEADS = 8
ATT_KV_HEADS = 2
ATT_HEAD_DIM = 64
ATT_Q_W = ATT_HEADS * ATT_HEAD_DIM
ATT_KV_W = ATT_KV_HEADS * ATT_HEAD_DIM
ATT_SCALE = ATT_HEAD_DIM ** -0.5
WINDOW = 128
BAND_CHUNKS = -(-WINDOW // CHUNK)
BAND = (BAND_CHUNKS + 1) * CHUNK
REL_BUCKETS = 32
REL_MAX_DIST = 128
RET_HEADS = 4
RET_KEY_DIM = 128
RET_VAL_DIM = 128
RET_W = RET_HEADS * RET_KEY_DIM
ROPE_BASE = 10000.0
IN_W = ATT_Q_W + 2 * ATT_KV_W + 4 * RET_W
D_FF = 4 * D_MODEL
N_MOD = 6
EPS = 1e-6
PAST_LEN = 4096
LOG2_E = math.log2(math.e)

LANES = 128
HALF_LANES = LANES // 2
BF16_SUBLANES = 16
ONES_ROWS = BF16_SUBLANES
VMEM_LIMIT_BYTES = 56 * 1024 * 1024

OFF_QA = 0
OFF_KV = OFF_QA + ATT_Q_W
OFF_QR = OFF_KV + 2 * ATT_KV_W
OFF_KR = OFF_QR + RET_W
OFF_VR = OFF_KR + RET_W
OFF_GR = OFF_VR + RET_W

TOKEN_BLOCK = 1024
INPROJ_SUB_BLOCK = 256
MIX_TOKEN_BLOCK = 1024
GROUP = 128
FF_BLOCK = 1024
ADA_COL_BLOCK = 1024
SAMPLE_GROUP = 4
SAMPLE_KEYS = 2 * LANES


def _params(*sem):
    return pltpu.CompilerParams(dimension_semantics=sem, vmem_limit_bytes=VMEM_LIMIT_BYTES)


def _const_spec(shape):
    zeros = (0,) * len(shape)
    return pl.BlockSpec(shape, lambda *_: zeros, pipeline_mode=pl.Buffered(1))


def _rms(x):
    return x * lax.rsqrt(jnp.mean(x * x, axis=-1, keepdims=True) + EPS)


def _dot(a, b):
    return jnp.dot(a, b, preferred_element_type=F32)


def _dot_nt(a, b):
    return lax.dot_general(a, b, (((1,), (1,)), ((), ())), preferred_element_type=F32)


def _dot_tn(a, b):
    return lax.dot_general(a, b, (((0,), (0,)), ((), ())), preferred_element_type=F32)


def _ada_kernel(c_ref, w_ref, b_ref, o_ref):
    a = jax.nn.silu(c_ref[...]).astype(BF16)
    o_ref[0] = _dot(a, w_ref[0].astype(BF16)) + b_ref[0]


def _ada_call(c_all, w_ada, b_ada):
    depth, d, n = w_ada.shape
    rows = c_all.shape[0]
    return pl.pallas_call(
        _ada_kernel,
        grid=(depth, n // ADA_COL_BLOCK),
        in_specs=[
            pl.BlockSpec((rows, d), lambda l, j: (0, 0)),
            pl.BlockSpec((1, d, ADA_COL_BLOCK), lambda l, j: (l, 0, j)),
            pl.BlockSpec((1, 1, ADA_COL_BLOCK), lambda l, j: (l, 0, j)),
        ],
        out_specs=pl.BlockSpec((1, rows, ADA_COL_BLOCK), lambda l, j: (l, 0, j)),
        out_shape=jax.ShapeDtypeStruct((depth, rows, n), F32),
        compiler_params=_params("arbitrary", "arbitrary"),
        name="ada",
    )(c_all, w_ada, b_ada.reshape(depth, 1, n))


def _bias_kernel(tab_ref, idx_ref, o_ref):
    h = pl.program_id(0)
    idx = idx_ref[...]
    acc = jnp.zeros(idx.shape, F32)
    for b in range(REL_BUCKETS):
        acc = jnp.where(idx == b, tab_ref[b, h], acc)
    o_ref[0] = acc


def _bias_call(rel_bias, bucket):
    q, j = bucket.shape
    return pl.pallas_call(
        _bias_kernel,
        grid=(ATT_HEADS,),
        in_specs=[
            pl.BlockSpec(memory_space=pltpu.SMEM),
            pl.BlockSpec((q, j), lambda h: (0, 0)),
        ],
        out_specs=pl.BlockSpec((1, q, j), lambda h: (h, 0, 0)),
        out_shape=jax.ShapeDtypeStruct((ATT_HEADS, q, j), F32),
        compiler_params=_params("arbitrary"),
        name="rel_bias",
    )(rel_bias, bucket)


def _rel_bucket(rel):
    nb = REL_BUCKETS // 2
    n = -np.asarray(rel, np.int64)
    ret = np.where(n < 0, nb, 0)
    n = np.abs(n)
    max_exact = nb // 2
    nf = np.maximum(n, 1).astype(np.float64)
    large = max_exact + (np.log(nf / max_exact) / math.log(REL_MAX_DIST / max_exact)
                         * (nb - max_exact)).astype(np.int64)
    large = np.minimum(large, nb - 1)
    return (ret + np.where(n < max_exact, n, large)).astype(np.int32)


def _pair_rows(per_head):
    out = []
    for k in range(ATT_KV_HEADS):
        for e in range(2):
            out.append(jnp.concatenate([per_head[4 * k + e], per_head[4 * k + 2 + e]], axis=0))
    return jnp.stack(out)


def _pair_lanes(per_head):
    out = []
    for k in range(ATT_KV_HEADS):
        for e in range(2):
            out.append(jnp.concatenate([per_head[4 * k + e], per_head[4 * k + 2 + e]], axis=-1))
    return jnp.stack(out)


def _banded_bias(per_head):
    base = _pair_lanes(jnp.swapaxes(per_head, 1, 2))
    key = jnp.arange(BAND)[None, :, None]
    return jnp.stack([jnp.where(key < (BAND_CHUNKS - v) * CHUNK, -jnp.inf, base)
                      for v in range(BAND_CHUNKS + 1)])


def _rotary(u, cos2, sin2):
    parts = []
    for h in range(RET_HEADS):
        x = u[:, h * RET_KEY_DIM:(h + 1) * RET_KEY_DIM]
        parts.append(x * cos2 + pltpu.roll(x, HALF_LANES, 1) * sin2)
    return jnp.concatenate(parts, axis=1)


def _inproj_kernel(x_ref, sh_ref, sc_ref, g_ref, w_ref, ac_ref, as_ref, bc_ref, bs_ref, bcn_ref, bsn_ref,
                   qdec_ref, kdec_ref,
                   qa_ref, kx_ref, va_ref, qd_ref, kg_ref, vr_ref, sg_ref, kvt_ref, *, tail):
    i = pl.program_id(1)
    t = x_ref.shape[1]
    sub = min(INPROJ_SUB_BLOCK, t)
    for r0 in range(0, t, sub):
        rows = slice(r0, r0 + sub)
        per_row = lambda ref: ref[rows] if ref.shape[0] == t else ref[...]
        x = x_ref[0, rows]
        h = (_rms(x) * g_ref[...]) * (1.0 + per_row(sc_ref)) + per_row(sh_ref)
        hb = h.astype(BF16)
        cos2 = ac_ref[...] * bc_ref[rows] - as_ref[...] * bs_ref[rows]
        sin2 = as_ref[...] * bcn_ref[rows] + ac_ref[...] * bsn_ref[rows]

        qd_ref[0, rows] = (_rotary(_dot(hb, w_ref[:, OFF_QR:OFF_KR]), cos2, sin2) * qdec_ref[rows]).astype(BF16)
        kg_ref[0, rows] = (_rotary(_dot(hb, w_ref[:, OFF_KR:OFF_VR]), cos2, sin2) * kdec_ref[rows]).astype(BF16)
        sg_ref[0, rows] = jax.nn.silu(_dot(hb, w_ref[:, OFF_GR:IN_W])).astype(BF16)
        kv = _dot(hb, w_ref[:, OFF_KV:OFF_QR])
        kx_ref[0, rows] = jnp.concatenate(_lane_split(kv[:, :ATT_KV_W]), axis=1).astype(BF16)
        va_ref[0, rows] = kv[:, ATT_KV_W:].astype(BF16)
        qa_ref[0, rows] = (_dot(hb, w_ref[:, OFF_QA:OFF_KV]) * (ATT_SCALE * LOG2_E)).astype(BF16)
        vr_ref[0, rows] = _dot(hb, w_ref[:, OFF_VR:OFF_GR]).astype(BF16)

        lo = max(r0, t - tail)
        if lo < r0 + sub:
            @pl.when(i == pl.num_programs(1) - 1)
            def _(kv=kv, lo=lo, r0=r0):
                kvt_ref[0, lo - (t - tail):r0 + sub - (t - tail)] = kv[lo - r0:]


MOD_SHIFT1, MOD_SCALE1, MOD_GATE1, MOD_SHIFT2, MOD_SCALE2, MOD_GATE2 = range(N_MOD)


def _mod_specs(mods, layer, which, t):
    if mods.ndim == 4:
        return [pl.BlockSpec((None, None, 1, D_MODEL), lambda b, i, m=m: (layer, b, 0, m)) for m in which]
    return [pl.BlockSpec((None, t, D_MODEL), lambda b, i, m=m: (layer, i, m)) for m in which]


def _layer_spec(arr, layer):
    tail_zeros = (0,) * (arr.ndim - 1)
    return pl.BlockSpec((None,) + arr.shape[1:], lambda *_: (layer,) + tail_zeros, pipeline_mode=pl.Buffered(1))


def _inproj_call(x, mods, gains, w_in, layer, rot, decay, tail):
    bsz, seq, d = x.shape
    t = min(TOKEN_BLOCK, seq)
    nb = seq // t
    base_cos, base_sin, off = rot
    assert base_cos.shape == (nb, 1, LANES) and off[0].shape == (t, LANES)
    assert all(a.shape == (t, RET_W) for a in decay)
    tok = lambda w: pl.BlockSpec((1, t, w), lambda b, i: (b, i, 0))
    out_bf = lambda w: jax.ShapeDtypeStruct((bsz, seq, w), BF16)
    base = pl.BlockSpec((None, 1, LANES), lambda b, i: (i, 0, 0))
    return pl.pallas_call(
        functools.partial(_inproj_kernel, tail=tail),
        grid=(bsz, nb),
        in_specs=[
            tok(d), *_mod_specs(mods, layer, (MOD_SHIFT1, MOD_SCALE1), t),
            _layer_spec(gains, layer), _const_spec(w_in.shape),
            base, base, *[_const_spec((t, LANES))] * 4, *[_const_spec((t, RET_W))] * 2,
        ],
        out_specs=[
            tok(ATT_Q_W), tok(4 * LANES), tok(ATT_KV_W), tok(RET_W), tok(RET_W), tok(RET_W), tok(RET_W),
            pl.BlockSpec((1, tail, 2 * ATT_KV_W), lambda b, i: (b, 0, 0)),
        ],
        out_shape=[
            out_bf(ATT_Q_W), out_bf(4 * LANES), out_bf(ATT_KV_W), out_bf(RET_W), out_bf(RET_W),
            out_bf(RET_W), out_bf(RET_W),
            jax.ShapeDtypeStruct((bsz, tail, 2 * ATT_KV_W), F32),
        ],
        compiler_params=_params("arbitrary", "arbitrary"),
        name="inproj",
    )(x, mods, mods, gains, w_in, base_cos, base_sin, *off, *decay)


def _lane_split(x):
    lo = lax.broadcasted_iota(jnp.int32, x.shape, 1) < HALF_LANES
    sw = pltpu.roll(x, HALF_LANES, 1)
    zero = jnp.zeros_like(x)
    return (jnp.where(lo, x, zero), jnp.where(lo, zero, sw), jnp.where(lo, sw, zero), jnp.where(lo, zero, x))


def _sink_softmax2(s, sink):
    m = jnp.maximum(jnp.max(s, axis=-1, keepdims=True), sink)
    e = jnp.exp2(s - m)
    den = jnp.sum(e, axis=-1, keepdims=True) + jnp.exp2(sink - m)
    return e / den


def _mix_prompt_kernel(cdec_ref, qa_ref, ka_ref, va_ref, qd_ref, kg_ref, vr_ref, sg_ref, bias_ref, sink_ref,
                       *rest, t, n_cast):
    cast_src = rest[:n_cast]
    mixed_ref, state_ref = rest[n_cast:n_cast + 2]
    cast_dst = rest[n_cast + 2:2 * n_cast + 2]
    kx_ref, vt_ref, s_ref = rest[2 * n_cast + 2:]
    for src, dst in zip(cast_src, cast_dst):
        dst[...] = src[...].astype(BF16)

    i = pl.program_id(1)
    n_groups = t // GROUP
    hist = BAND - CHUNK

    hd = ATT_HEAD_DIM

    @pl.when(i == 0)
    def _():
        kx_ref[:, 0:hist, :] = jnp.zeros((4, hist, LANES), BF16)
        vt_ref[0, :, 0:hd, 0:GROUP] = jnp.zeros((ATT_KV_HEADS, hd, GROUP), BF16)
        vt_ref[:, :, hd:, :] = jnp.ones((n_groups, ATT_KV_HEADS, ONES_ROWS, 2 * GROUP), BF16)
        s_ref[...] = jnp.zeros(s_ref.shape, F32)

    @pl.when(i > 0)
    def _():
        kx_ref[:, 0:hist, :] = kx_ref[:, t:t + hist, :]
        vt_ref[0, :, 0:hd, 0:GROUP] = vt_ref[n_groups - 1, :, 0:hd, GROUP:]

    for n in range(4):
        kx_ref[n, hist:hist + t, :] = ka_ref[0, :, n * LANES:(n + 1) * LANES]
    for m in range(n_groups):
        tile = va_ref[0, m * GROUP:(m + 1) * GROUP, :].astype(F32).T.astype(BF16)
        for k in range(ATT_KV_HEADS):
            vt_ref[m, k, 0:hd, GROUP:] = tile[k * hd:(k + 1) * hd]
            if m + 1 < n_groups:
                vt_ref[m + 1, k, 0:hd, 0:GROUP] = tile[k * hd:(k + 1) * hd]

    no_keys = jnp.zeros((CHUNK, LANES), BF16)
    causal = (lax.broadcasted_iota(jnp.int32, (GROUP, GROUP), 0)
              >= lax.broadcasted_iota(jnp.int32, (GROUP, GROUP), 1))
    heads = [slice(h * LANES, (h + 1) * LANES) for h in range(RET_HEADS)]
    att = [(j, k, e) for j in range(GROUP // CHUNK) for k in range(ATT_KV_HEADS) for e in range(2)]

    def lead(g):
        grows = slice(g * GROUP, (g + 1) * GROUP)
        raw = {}
        for j in range(GROUP // CHUNK):
            r0 = g * GROUP + j * CHUNK
            for k in range(ATT_KV_HEADS):
                c0 = 2 * LANES * k
                q2 = jnp.concatenate([qa_ref[0, r0:r0 + CHUNK, c0:c0 + LANES],
                                      qa_ref[0, r0:r0 + CHUNK, c0 + LANES:c0 + 2 * LANES]], axis=0)
                keys = jnp.concatenate([kx_ref[2 * k, r0:r0 + BAND, :], kx_ref[2 * k + 1, r0:r0 + BAND, :]], axis=0)
                s2 = _dot_nt(keys, q2)
                raw[j, k, 0], raw[j, k, 1] = s2[:BAND], s2[BAND:]
        states = [s_ref[h] for h in range(RET_HEADS)]
        ret_q = [qd_ref[0, grows, c] for c in heads]
        ret_k = [kg_ref[0, grows, c] for c in heads]
        ret_v = [vr_ref[0, grows, c] for c in heads]
        ret_scores = [_dot_nt(ret_q[h], ret_k[h]) for h in range(RET_HEADS)]
        for h in range(RET_HEADS):
            s_ref[h] = cdec_ref[h] * (states[h] + _dot_tn(ret_k[h], ret_v[h]))
        return raw, states, ret_q, ret_v, ret_scores

    def finish(g, lead_out):
        raw, states, ret_q, ret_v, ret_scores = lead_out
        grows = slice(g * GROUP, (g + 1) * GROUP)
        chunk_rows = lambda j: slice(g * GROUP + j * CHUNK, g * GROUP + (j + 1) * CHUNK)
        probs, sink_terms = {}, {}
        for j, k, e in att:
            n = 2 * k + e
            variant = jnp.minimum(i * (t // CHUNK) + 2 * g + j, BAND_CHUNKS)
            s = raw[j, k, e] + bias_ref[variant, n]
            sink = sink_ref[n]
            m = jnp.maximum(jnp.max(s, axis=0, keepdims=True), sink)
            probs[j, k, e] = jnp.exp2(s - m).astype(BF16)
            sink_terms[j, k, e] = jnp.exp2(sink - m)
        ret_p = [jnp.where(causal, ret_scores[h], 0.0).astype(BF16) for h in range(RET_HEADS)]

        ret_out = [_dot(jnp.concatenate([ret_p[h], ret_q[h]], axis=1),
                        jnp.concatenate([ret_v[h], states[h].astype(BF16)], axis=0)) for h in range(RET_HEADS)]
        out_t = {}
        for j, k, e in att:
            frame = [probs[j, k, e], no_keys] if j == 0 else [no_keys, probs[j, k, e]]
            out_t[j, k, e] = _dot(vt_ref[g, k], jnp.concatenate(frame, axis=0))

        for j in range(GROUP // CHUNK):
            for k in range(ATT_KV_HEADS):
                c0 = 2 * LANES * k
                halves = []
                for e in range(2):
                    o_t = out_t[j, k, e]
                    den = o_t[ATT_HEAD_DIM:ATT_HEAD_DIM + 1] + sink_terms[j, k, e]
                    halves.append(o_t[:ATT_HEAD_DIM] * (1.0 / den))
                o = jnp.concatenate(halves, axis=0).T
                mixed_ref[0, chunk_rows(j), c0:c0 + LANES] = o[:CHUNK].astype(BF16)
                mixed_ref[0, chunk_rows(j), c0 + LANES:c0 + 2 * LANES] = o[CHUNK:].astype(BF16)
        for h in range(RET_HEADS):
            out = _rms(ret_out[h]) * sg_ref[0, grows, heads[h]].astype(F32)
            mixed_ref[0, grows, ATT_Q_W + h * LANES:ATT_Q_W + (h + 1) * LANES] = out.astype(BF16)

    for g in range(n_groups):
        finish(g, lead(g))

    @pl.when(i == pl.num_programs(1) - 1)
    def _():
        state_ref[0] = s_ref[...]


def _mix_prompt_call(qa, kx, va, qd, kg, vr, sg, bias4, sink4, cdec, weights):
    bsz, seq, _ = qa.shape
    t = min(MIX_TOKEN_BLOCK, seq)
    nb = seq // t
    hist = BAND - CHUNK
    assert hist == GROUP and t % GROUP == 0 and seq % t == 0
    tok = lambda w: pl.BlockSpec((1, t, w), lambda b, i: (b, i, 0))
    steps = bsz * nb
    slabs, slab_in, slab_out, slab_shape = [], [], [], []
    for stack, layer in weights:
        depth, rows, cols = stack.shape
        assert rows % (steps * BF16_SUBLANES) == 0
        slabs.append(stack.reshape(depth, steps, rows // steps, cols))
        slab_in.append(pl.BlockSpec((None, None, rows // steps, cols), lambda b, i, layer=layer: (layer, b * nb + i, 0, 0)))
        slab_out.append(pl.BlockSpec((None, rows // steps, cols), lambda b, i: (b * nb + i, 0, 0)))
        slab_shape.append(jax.ShapeDtypeStruct((steps, rows // steps, cols), BF16))
    outs = pl.pallas_call(
        functools.partial(_mix_prompt_kernel, t=t, n_cast=len(weights)),
        grid=(bsz, nb),
        in_specs=[
            pl.BlockSpec(memory_space=pltpu.SMEM),
            tok(ATT_Q_W), tok(4 * LANES), tok(ATT_KV_W), tok(RET_W), tok(RET_W), tok(RET_W), tok(RET_W),
            _const_spec(bias4.shape), _const_spec(sink4.shape),
        ] + slab_in,
        out_specs=[
            tok(D_MODEL),
            pl.BlockSpec((1, RET_HEADS, RET_KEY_DIM, RET_VAL_DIM), lambda b, i: (b, 0, 0, 0)),
        ] + slab_out,
        out_shape=[
            jax.ShapeDtypeStruct((bsz, seq, D_MODEL), BF16),
            jax.ShapeDtypeStruct((bsz, RET_HEADS, RET_KEY_DIM, RET_VAL_DIM), F32),
        ] + slab_shape,
        scratch_shapes=[
            pltpu.VMEM((4, hist + t, LANES), BF16),
            pltpu.VMEM((t // GROUP, ATT_KV_HEADS, ATT_HEAD_DIM + ONES_ROWS, 2 * GROUP), BF16),
            pltpu.VMEM((RET_HEADS, RET_KEY_DIM, RET_VAL_DIM), F32),
        ],
        compiler_params=_params("arbitrary", "arbitrary"),
        name="mix_prompt",
    )(cdec, qa, kx, va, qd, kg, vr, sg, bias4, sink4, *slabs)
    rounded = [o.reshape(stack.shape[1:]) for o, (stack, _) in zip(outs[2:], weights)]
    return outs[0], outs[1], rounded


def _mix_sample_kernel(cdec_ref, qa_ref, kvn_ref, ck_ref, cv_ref, qd_ref, kg_ref, vr_ref, sg_ref, s0_ref,
                       bias_ref, sink_ref, *rest, n_new, n_cache):
    mixed_ref, *stack_refs = rest[-4:]
    done = 0
    if len(rest) > 4:
        done = rest[0].shape[0]
        for prev_ref, out_ref in zip(rest[:3], stack_refs):
            out_ref[0:done] = prev_ref[...]
    state_ref, wk_ref, wv_ref = (r.at[done] for r in stack_refs)
    pad = jnp.zeros((SAMPLE_KEYS - n_cache - n_new, LANES), F32)
    causal = (lax.broadcasted_iota(jnp.int32, (n_new, n_new), 0)
              >= lax.broadcasted_iota(jnp.int32, (n_new, n_new), 1))
    seqs = range(SAMPLE_GROUP)
    heads = [slice(h * LANES, (h + 1) * LANES) for h in range(RET_HEADS)]
    att = [(g, k, e) for g in seqs for k in range(ATT_KV_HEADS) for e in range(2)]
    ret = [(g, h) for g in seqs for h in range(RET_HEADS)]

    v4, raw = {}, {}
    for g in seqs:
        kvn = kvn_ref[g]
        k_all = jnp.concatenate([ck_ref[g], kvn[:, :ATT_KV_W], pad], axis=0)
        v_all = jnp.concatenate([cv_ref[g], kvn[:, ATT_KV_W:], pad], axis=0)
        wk_ref[g] = k_all[n_new:n_new + n_cache]
        wv_ref[g] = v_all[n_new:n_new + n_cache]
        k4 = [p.astype(BF16) for p in _lane_split(k_all)]
        v4[g] = [p.astype(BF16) for p in _lane_split(v_all)]
        for k in range(ATT_KV_HEADS):
            c0 = 2 * LANES * k
            q2 = jnp.concatenate([qa_ref[g, :, c0:c0 + LANES], qa_ref[g, :, c0 + LANES:c0 + 2 * LANES]], axis=0)
            for e in range(2):
                raw[g, k, e] = _dot_nt(q2, k4[2 * k + e])
    qd = {(g, h): qd_ref[g, :, heads[h]] for g, h in ret}
    kg = {(g, h): kg_ref[g, :, heads[h]] for g, h in ret}
    vr = {(g, h): vr_ref[g, :, heads[h]] for g, h in ret}
    state = {(g, h): s0_ref[g, h] for g, h in ret}
    ret_scores = {gh: _dot_nt(qd[gh], kg[gh]) for gh in ret}
    ret_inter = {gh: _dot(qd[gh], state[gh].astype(BF16)) for gh in ret}
    for g, h in ret:
        state_ref[g, h] = cdec_ref[h] * (state[g, h] + _dot_tn(kg[g, h], vr[g, h]))

    probs = {(g, k, e): _sink_softmax2(raw[g, k, e] + bias_ref[2 * k + e], sink_ref[2 * k + e]).astype(BF16)
             for g, k, e in att}
    ret_p = {gh: jnp.where(causal, ret_scores[gh], 0.0).astype(BF16) for gh in ret}

    out = {(g, k, e): _dot(probs[g, k, e], v4[g][2 * k + e]) for g, k, e in att}
    ret_intra = {gh: _dot(ret_p[gh], vr[gh]) for gh in ret}

    for g in seqs:
        for k in range(ATT_KV_HEADS):
            c0 = 2 * LANES * k
            o = out[g, k, 0] + out[g, k, 1]
            mixed_ref[g, :, c0:c0 + LANES] = o[:n_new].astype(BF16)
            mixed_ref[g, :, c0 + LANES:c0 + 2 * LANES] = o[n_new:].astype(BF16)
        for h in range(RET_HEADS):
            ro = _rms(ret_intra[g, h] + ret_inter[g, h]) * sg_ref[g, :, heads[h]].astype(F32)
            mixed_ref[g, :, ATT_Q_W + h * LANES:ATT_Q_W + (h + 1) * LANES] = ro.astype(BF16)


def _mix_sample_call(qa, kvn, cache_k, cache_v, qd, kg, vr, sg, state, layer, stacks, bias4, sink4, cdec):
    dbsz, n_new, _ = qa.shape
    n_cache = cache_k.shape[2]
    grp = lambda *tail: pl.BlockSpec((SAMPLE_GROUP,) + tail, lambda b: (b,) + (0,) * len(tail))
    lay = lambda *tail: pl.BlockSpec((None, SAMPLE_GROUP) + tail, lambda b: (layer, b) + (0,) * len(tail))
    stk = lambda n, *tail: pl.BlockSpec((n, SAMPLE_GROUP) + tail, lambda b: (0, b) + (0,) * len(tail))
    stack_tails = [(RET_HEADS, RET_KEY_DIM, RET_VAL_DIM), (n_cache, ATT_KV_W), (n_cache, ATT_KV_W)]
    operands = [cdec, qa, kvn, cache_k, cache_v, qd, kg, vr, sg, state, bias4, sink4]
    prev_specs = []
    if stacks is not None:
        assert all(s.shape[0] == layer for s in stacks)
        operands += list(stacks)
        prev_specs = [stk(layer, *tail) for tail in stack_tails]
    outs = pl.pallas_call(
        functools.partial(_mix_sample_kernel, n_new=n_new, n_cache=n_cache),
        grid=(dbsz // SAMPLE_GROUP,),
        in_specs=[
            pl.BlockSpec(memory_space=pltpu.SMEM),
            grp(n_new, ATT_Q_W), grp(n_new, 2 * ATT_KV_W), lay(n_cache, ATT_KV_W), lay(n_cache, ATT_KV_W),
            grp(n_new, RET_W), grp(n_new, RET_W), grp(n_new, RET_W), grp(n_new, RET_W),
            lay(RET_HEADS, RET_KEY_DIM, RET_VAL_DIM),
            _const_spec(bias4.shape), _const_spec(sink4.shape),
        ] + prev_specs,
        out_specs=[grp(n_new, D_MODEL)] + [stk(layer + 1, *tail) for tail in stack_tails],
        out_shape=[jax.ShapeDtypeStruct((dbsz, n_new, D_MODEL), BF16)]
        + [jax.ShapeDtypeStruct((layer + 1, dbsz) + tail, F32) for tail in stack_tails],
        compiler_params=_params("arbitrary"),
        name="mix_sample",
    )(*operands)
    return outs[0], outs[1:]


def _dense_kernel(x_ref, mixed_ref, ga1_ref, sh2_ref, sc2_ref, ga2_ref, g_ref, gf_ref,
                  wo_ref, wu_ref, wd_ref, o_ref, *, final):
    x1 = x_ref[0] + ga1_ref[...] * _dot(mixed_ref[0], wo_ref[...])
    h = ((_rms(x1) * g_ref[...]) * (1.0 + sc2_ref[...]) + sh2_ref[...]).astype(BF16)
    acc = jnp.zeros(x1.shape, F32)
    for j in range(D_FF // FF_BLOCK):
        u = _dot(h, wu_ref[:, j * FF_BLOCK:(j + 1) * FF_BLOCK])
        u = jnp.square(jnp.maximum(u, 0.0)).astype(BF16)
        acc = acc + _dot(u, wd_ref[j * FF_BLOCK:(j + 1) * FF_BLOCK, :])
    x2 = x1 + ga2_ref[...] * acc
    o_ref[0] = _rms(x2) * gf_ref[...] if final else x2


def _dense_call(x, mixed, mods, gains, g_final, w_out, w_up, w_down, layer, final):
    bsz, seq, d = x.shape
    t = min(TOKEN_BLOCK, seq)
    tok = lambda w: pl.BlockSpec((1, t, w), lambda b, i: (b, i, 0))
    return pl.pallas_call(
        functools.partial(_dense_kernel, final=final),
        grid=(bsz, seq // t),
        in_specs=[
            tok(d), tok(d), *_mod_specs(mods, layer, (MOD_GATE1, MOD_SHIFT2, MOD_SCALE2, MOD_GATE2), t),
            _layer_spec(gains, layer), _const_spec((1, d)),
            _const_spec(w_out.shape), _const_spec(w_up.shape), _const_spec(w_down.shape),
        ],
        out_specs=tok(d),
        out_shape=jax.ShapeDtypeStruct((bsz, seq, d), F32),
        compiler_params=_params("arbitrary", "arbitrary"),
        name="dense",
    )(x, mixed, mods, mods, mods, mods, gains, g_final.reshape(1, d), w_out, w_up, w_down)


def _rotary_tables(base_pos, off_pos):
    half = RET_KEY_DIM // 2
    inv = 1.0 / (ROPE_BASE ** (np.arange(half, dtype=np.float64) / half))
    sign = np.concatenate([-np.ones((half,)), np.ones((half,))])

    def both(pos):
        ang = np.asarray(pos, np.float64)[:, None] * inv[None, :]
        cos, sin = np.cos(ang), np.sin(ang)
        return np.concatenate([cos, cos], axis=1), np.concatenate([sin, sin], axis=1)

    base_cos, base_sin = both(base_pos)
    off_cos, off_sin = both(off_pos)
    f32 = lambda a: jnp.asarray(a, F32)
    return (f32(base_cos[:, None, :]), f32(base_sin[:, None, :]),
            tuple(f32(a) for a in (off_cos, off_sin, sign * off_cos, sign * off_sin)))


def _retention_tables(n, rows):
    log_g = np.log(1.0 - 2.0 ** (-5.0 - np.arange(RET_HEADS, dtype=np.float64)))
    steps = (np.arange(rows) % n).astype(np.float64) + 1.0
    expo = steps[:, None] * np.repeat(log_g, RET_KEY_DIM)[None, :]
    f32 = lambda a: jnp.asarray(a, F32)
    return (f32(np.exp(expo)), f32(np.exp(-expo) * (RET_KEY_DIM ** -0.5))), f32(np.exp(n * log_g))


def kernel(x_prompt, x_sample, c_prompt, c_sample, cache_win_k, cache_win_v, state_ret, g_mix, g_mlp, w_ada,
           b_ada, w_in, w_out, att_sinks, rel_bias, w_up, w_down, g_final):
    depth = w_in.shape[0]
    bsz, seq, d = x_prompt.shape
    dbsz, dec_seq, _ = x_sample.shape
    n_cache = cache_win_k.shape[2]
    n_rows = dbsz * dec_seq
    keep_p = min(WINDOW, seq)

    w_in_b = w_in[0].astype(BF16)
    g_mix3 = g_mix.reshape(depth, 1, d)
    g_mlp3 = g_mlp.reshape(depth, 1, d)
    cache_k = cache_win_k.reshape(depth, dbsz, n_cache, ATT_KV_W)
    cache_v = cache_win_v.reshape(depth, dbsz, n_cache, ATT_KV_W)

    c_all = jnp.concatenate([c_prompt, c_sample], axis=0)
    c_all = jnp.pad(c_all, ((0, -c_all.shape[0] % 8), (0, 0)))
    mods = _ada_call(c_all, w_ada, b_ada)
    mods_p = mods[:, :bsz, None, :]
    mods_s = jnp.repeat(mods[:, bsz:bsz + dbsz], dec_seq, axis=1)

    t_p = min(TOKEN_BLOCK, seq)
    rot_p = _rotary_tables(np.arange(seq // t_p) * t_p, np.arange(t_p))
    rot_s = _rotary_tables(np.full((1,), PAST_LEN), np.tile(np.arange(dec_seq), dbsz))
    decay_p, cdec_p = _retention_tables(GROUP, t_p)
    decay_s, cdec_s = _retention_tables(dec_seq, n_rows)

    rel_p = (np.arange(BAND) - BAND_CHUNKS * CHUNK)[None, :] - np.arange(CHUNK)[:, None]
    bias_p = _banded_bias(_bias_call(rel_bias, jnp.asarray(_rel_bucket(rel_p))) * LOG2_E)
    kpos = PAST_LEN - n_cache + np.arange(n_cache + dec_seq)
    qpos = PAST_LEN + np.arange(dec_seq)
    bias_s = _bias_call(rel_bias, jnp.asarray(_rel_bucket(kpos[None, :] - qpos[:, None]))) * LOG2_E
    bias_s = jnp.pad(bias_s, ((0, 0), (0, 0), (0, SAMPLE_KEYS - n_cache - dec_seq)), constant_values=-jnp.inf)
    bias_s = _pair_rows(bias_s)

    xp = x_prompt
    xs = x_sample.reshape(1, n_rows, d)
    pk, pv, pr = [], [], []
    stacks = None
    for l in range(depth):
        sinks = att_sinks[l].astype(F32)[:, None, None] * LOG2_E
        sink_p = _pair_lanes(jnp.broadcast_to(sinks, (ATT_HEADS, 1, CHUNK)))
        sink_s = _pair_rows(jnp.broadcast_to(sinks, (ATT_HEADS, dec_seq, 1)))
        final = l == depth - 1

        qa, kx, va, qd, kg, vr, sg, kvt = _inproj_call(xp, mods_p, g_mix3, w_in_b, l, rot_p, decay_p, keep_p)
        to_round = [(w_out, l), (w_up, l), (w_down, l)] + ([] if final else [(w_in, l + 1)])
        mixed, st, rounded = _mix_prompt_call(qa, kx, va, qd, kg, vr, sg, bias_p, sink_p, cdec_p, to_round)
        w_out_b, w_up_b, w_down_b = rounded[:3]
        w_in_next = None if final else rounded[3]
        xp = _dense_call(xp, mixed, mods_p, g_mlp3, g_final, w_out_b, w_up_b, w_down_b, l, final)
        pk.append(kvt[:, :, :ATT_KV_W].reshape(bsz, keep_p, ATT_KV_HEADS, ATT_HEAD_DIM))
        pv.append(kvt[:, :, ATT_KV_W:].reshape(bsz, keep_p, ATT_KV_HEADS, ATT_HEAD_DIM))
        pr.append(st)

        qa, _, _, qd, kg, vr, sg, kvn = _inproj_call(xs, mods_s, g_mix3, w_in_b, l, rot_s, decay_s, n_rows)
        per_seq = lambda a: a.reshape(dbsz, dec_seq, a.shape[-1])
        mixed, stacks = _mix_sample_call(per_seq(qa), per_seq(kvn), cache_k, cache_v, per_seq(qd), per_seq(kg),
                                         per_seq(vr), per_seq(sg), state_ret, l, stacks, bias_s, sink_s, cdec_s)
        xs = _dense_call(xs, mixed.reshape(1, n_rows, d), mods_s, g_mlp3, g_final, w_out_b, w_up_b, w_down_b, l,
                         final)
        w_in_b = w_in_next

    sr, sk, sv = stacks
    win_shape = (depth, dbsz, n_cache, ATT_KV_HEADS, ATT_HEAD_DIM)
    return (xp, xs.reshape(dbsz, dec_seq, d), jnp.stack(pk), jnp.stack(pv), jnp.stack(pr),
            sk.reshape(win_shape), sv.reshape(win_shape), sr)
```
